```python
import math
import jax
import jax.numpy as jnp
from jax import lax
import numpy as np

D_MODEL = 1024
BATCH = 4
SEQ = 8192
DEPTH = 4
DEC_BATCH = 32
DEC_SEQ = 2048
PAST_LEN = 128

HEAD_DIM = 64
N_HEADS = D_MODEL // (2 * HEAD_DIM)
V_DIM = 2 * HEAD_DIM
QK_WIDTH = N_HEADS * 2 * HEAD_DIM
V_WIDTH = N_HEADS * V_DIM
CONV_DIM = D_MODEL
CONV_K = 3
IN_WIDTH = 2 * QK_WIDTH + V_WIDTH + 3 * CONV_DIM + 2 * D_MODEL
D_FF = 2816
N_EXPERTS = 8
TOP_K = 2
D_EXPERT = 3584
N_DENSE = (DEPTH + 1) // 2
N_MOE = DEPTH // 2
Q_BLOCK = 128
MOE_BLOCK = 512
ALIBI_MAX = 8.0
NORM_EPS = 1e-6

kernel_name = 'hybrid_diffattn_shortconv_moe_encoder'


def rms_norm(x, g):
    xf = x.astype(jnp.float32)
    inv = lax.rsqrt(jnp.mean(xf * xf, axis=-1, keepdims=True) + NORM_EPS)
    return (xf * inv).astype(x.dtype) * g


def alibi_slopes():
    return 2.0 ** (-ALIBI_MAX * jnp.arange(1, N_HEADS + 1, dtype=jnp.float32) / N_HEADS)


def diff_attention(q, k, v, lam):
    bsz, seq = q.shape[0], q.shape[1]
    n_blk = seq // Q_BLOCK
    slopes = alibi_slopes()
    key_pos = jnp.arange(seq, dtype=jnp.float32)
    q_blocks = q.reshape(bsz, n_blk, Q_BLOCK, N_HEADS, 2, HEAD_DIM).swapaxes(0, 1)
    starts = jnp.arange(n_blk, dtype=jnp.float32) * Q_BLOCK
    scale = HEAD_DIM ** -0.5

    def block(args):
        qb, start = args
        scores = jnp.einsum('bqhmd,bkhmd->bhmqk', qb, k).astype(jnp.float32) * scale
        dist = jnp.abs(start + jnp.arange(Q_BLOCK, dtype=jnp.float32)[:, None] - key_pos[None, :])
        scores = scores - slopes[:, None, None, None] * dist
        p = jax.nn.softmax(scores, axis=-1)
        a = p[:, :, 0] - lam * p[:, :, 1]
        return jnp.einsum('bhqk,bkhe->bqhe', a.astype(v.dtype), v)

    out = lax.map(block, (q_blocks, starts))
    return out.swapaxes(0, 1).reshape(bsz, seq, N_HEADS, V_DIM)


def token_mixer(h, w_in, lq1, lk1, lq2, lk2, g_subln, conv_w, w_out, lam_init):
    bsz, seq, _ = h.shape
    z = h @ w_in
    splits = np.cumsum([QK_WIDTH, QK_WIDTH, V_WIDTH, CONV_DIM, CONV_DIM, CONV_DIM, D_MODEL]).tolist()
    q, k, v, conv_b, conv_c, conv_x, gate_a, gate_c = jnp.split(z, splits, axis=-1)
    f32 = jnp.float32
    lam = (jnp.exp(jnp.sum(lq1.astype(f32) * lk1.astype(f32)))
           - jnp.exp(jnp.sum(lq2.astype(f32) * lk2.astype(f32))) + lam_init)
    attn = diff_attention(q.reshape(bsz, seq, N_HEADS, 2, HEAD_DIM),
                          k.reshape(bsz, seq, N_HEADS, 2, HEAD_DIM),
                          v.reshape(bsz, seq, N_HEADS, V_DIM), lam)
    attn = (rms_norm(attn, g_subln) * (1.0 - lam_init)).reshape(bsz, seq, V_WIDTH)
    u = conv_c * conv_x
    pad = CONV_K // 2
    up = jnp.pad(u, ((0, 0), (pad, pad), (0, 0)))
    conv = sum(up[:, j:j + seq] * conv_w[j] for j in range(CONV_K))
    short = conv_b * conv
    merged = jax.nn.sigmoid(gate_a) * attn + jax.nn.sigmoid(gate_c) * short
    return merged @ w_out


def swiglu(h, w_gate, w_up, w_down):
    return (jax.nn.silu(h @ w_gate) * (h @ w_up)) @ w_down


def moe_swiglu(h, w_router, w_gate, w_up, w_down):
    n_tok = h.shape[0]
    n_assign = n_tok * TOP_K
    n_blocks = -(-(n_assign + N_EXPERTS * (MOE_BLOCK - 1)) // MOE_BLOCK)
    n_pad = n_blocks * MOE_BLOCK
    logits = (h @ w_router).astype(jnp.float32)
    top_logit, top_e = lax.top_k(logits, TOP_K)
    top_w = jax.nn.softmax(top_logit, axis=-1)
    flat_e = top_e.reshape(-1).astype(jnp.int32)
    flat_tok = jnp.repeat(jnp.arange(n_tok, dtype=jnp.int32), TOP_K)
    order = jnp.argsort(flat_e)
    sorted_e = flat_e[order]
    counts = jnp.bincount(flat_e, length=N_EXPERTS).astype(jnp.int32)
    padded = (counts + MOE_BLOCK - 1) // MOE_BLOCK * MOE_BLOCK
    end_pad = jnp.cumsum(padded)
    start_pad = end_pad - padded
    start = jnp.cumsum(counts) - counts
    dest = start_pad[sorted_e] + jnp.arange(n_assign, dtype=jnp.int32) - start[sorted_e]
    row_tok = jnp.zeros((n_pad,), jnp.int32).at[dest].set(flat_tok[order])
    row_w = jnp.zeros((n_pad,), h.dtype).at[dest].set(top_w.reshape(-1)[order].astype(h.dtype))
    block_e = jnp.minimum(
        jnp.searchsorted(end_pad, jnp.arange(n_blocks, dtype=jnp.int32) * MOE_BLOCK, side='right'),
        N_EXPERTS - 1)
    x_rows = h[row_tok].reshape(n_blocks, MOE_BLOCK, h.shape[1])

    def expert_block(args):
        xb, e = args
        return swiglu(xb, w_gate[e], w_up[e], w_down[e])

    y_rows = lax.map(expert_block, (x_rows, block_e)).reshape(n_pad, h.shape[1])
    return jax.ops.segment_sum(y_rows * row_w[:, None], row_tok, num_segments=n_tok)


def encoder_trunk(x, c, w_ada, b_ada, g_mix_pre, g_mix_post, g_ffn_pre, g_ffn_post, w_in,
                  lam_q1, lam_k1, lam_q2, lam_k2, g_subln, conv_w, w_out,
                  w_ffn_gate, w_ffn_up, w_ffn_down, w_router, w_exp_gate, w_exp_up, w_exp_down):
    cond = jax.nn.silu(c)
    for layer in range(DEPTH):
        mod = cond @ w_ada[layer] + b_ada[layer]
        shift_m, scale_m, gate_m, shift_f, scale_f, gate_f = [m[:, None, :] for m in jnp.split(mod, 6, axis=-1)]
        lam_init = 0.8 - 0.6 * math.exp(-0.3 * layer)
        h = rms_norm(x, g_mix_pre[layer]) * (1 + scale_m) + shift_m
        o = token_mixer(h, w_in[layer], lam_q1[layer], lam_k1[layer], lam_q2[layer], lam_k2[layer],
                        g_subln[layer], conv_w[layer], w_out[layer], lam_init)
        x = x + gate_m * rms_norm(o, g_mix_post[layer])
        h = rms_norm(x, g_ffn_pre[layer]) * (1 + scale_f) + shift_f
        i = layer // 2
        if layer % 2 == 0:
            o = swiglu(h, w_ffn_gate[i], w_ffn_up[i], w_ffn_down[i])
        else:
            o = moe_swiglu(h.reshape(-1, D_MODEL), w_router[i], w_exp_gate[i], w_exp_up[i],
                           w_exp_down[i]).reshape(h.shape)
        x = x + gate_f * rms_norm(o, g_ffn_post[layer])
    return x


def setup_inputs(seed: int = 0) -> dict:
    key = jax.random.key(seed)
    ks = iter(jax.random.split(key, 32))

    def nrm(shape, scale):
        return jax.random.normal(next(ks), shape, jnp.float32) * scale

    def gain(shape):
        return 1.0 + nrm(shape, 0.05)

    d = D_MODEL
    return {
        'x_prompt': nrm((BATCH, SEQ, d), 1.0),
        'x_sample': nrm((DEC_BATCH, DEC_SEQ, d), 1.0),
        'c_prompt': nrm((BATCH, d), 1.0),
        'c_sample': nrm((DEC_BATCH, d), 1.0),
        'w_ada': nrm((DEPTH, d, 6 * d), 0.5 * d ** -0.5),
        'b_ada': nrm((DEPTH, 6 * d), 0.02),
        'g_mix_pre': gain((DEPTH, d)),
        'g_mix_post': gain((DEPTH, d)),
        'g_ffn_pre': gain((DEPTH, d)),
        'g_ffn_post': gain((DEPTH, d)),
        'w_in': nrm((DEPTH, d, IN_WIDTH), d ** -0.5),
        'lam_q1': nrm((DEPTH, HEAD_DIM), 0.1),
        'lam_k1': nrm((DEPTH, HEAD_DIM), 0.1),
        'lam_q2': nrm((DEPTH, HEAD_DIM), 0.1),
        'lam_k2': nrm((DEPTH, HEAD_DIM), 0.1),
        'g_subln': gain((DEPTH, V_DIM)),
        'conv_w': nrm((DEPTH, CONV_K, CONV_DIM), CONV_K ** -0.5),
        'w_out': nrm((DEPTH, d, d), d ** -0.5),
        'w_ffn_gate': nrm((N_DENSE, d, D_FF), d ** -0.5),
        'w_ffn_up': nrm((N_DENSE, d, D_FF), d ** -0.5),
        'w_ffn_down': nrm((N_DENSE, D_FF, d), D_FF ** -0.5),
        'w_router': nrm((N_MOE, d, N_EXPERTS), d ** -0.5),
        'w_exp_gate': nrm((N_MOE, N_EXPERTS, d, D_EXPERT), d ** -0.5),
        'w_exp_up': nrm((N_MOE, N_EXPERTS, d, D_EXPERT), d ** -0.5),
        'w_exp_down': nrm((N_MOE, N_EXPERTS, D_EXPERT, d), D_EXPERT ** -0.5),
    }


def reference(x_prompt, x_sample, c_prompt, c_sample, w_ada, b_ada, g_mix_pre, g_mix_post,
              g_ffn_pre, g_ffn_post, w_in, lam_q1, lam_k1, lam_q2, lam_k2, g_subln, conv_w, w_out,
              w_ffn_gate, w_ffn_up, w_ffn_down, w_router, w_exp_gate, w_exp_up, w_exp_down):
    weights = (w_ada, b_ada, g_mix_pre, g_mix_post, g_ffn_pre, g_ffn_post, w_in,
               lam_q1, lam_k1, lam_q2, lam_k2, g_subln, conv_w, w_out,
               w_ffn_gate, w_ffn_up, w_ffn_down, w_router, w_exp_gate, w_exp_up, w_exp_down)
    y_prompt = encoder_trunk(x_prompt, c_prompt, *weights)
    y_sample = encoder_trunk(x_sample, c_sample, *weights)
    return (y_prompt, y_sample)
```

```python
import functools
import math

import numpy as np
import jax
import jax.numpy as jnp
from jax import lax
from jax.experimental import pallas as pl
from jax.experimental.pallas import tpu as pltpu

F32 = jnp.float32
BF16 = jnp.bfloat16

ALIBI_MAX = 8.0
NORM_EPS = 1e-6
TOP_K = 2
N_MOD = 6

VMEM_LIMIT_BYTES = 56 * 1024 * 1024
ROW_ALIGN = 16

INPROJ_ROWS, INPROJ_COLS = 1024, 2048
ATTN_Q_ROWS, ATTN_KV_CHUNK, ATTN_KV_SUB = 512, 2048, 512
OUTPROJ_ROWS = 256
FFN_ROWS, FFN_COLS = 512, 1408
MOE_BLOCK, MOE_COLS = 1024, 896
GATHER_ROWS = 1024
COMBINE_ROWS = 512
SMEM_INDEX_ALIGN = 1024


def _tile(n, pref, align=1):
    t = min(n, pref)
    while t > 0:
        if n % t == 0 and t % align == 0:
            return t
        t -= 1
    raise ValueError(f"no tile for {n} (pref {pref}, align {align})")


def _params(sem):
    return pltpu.CompilerParams(dimension_semantics=sem, vmem_limit_bytes=VMEM_LIMIT_BYTES)


def _rms(x):
    return x * lax.rsqrt(jnp.mean(x * x, axis=-1, keepdims=True) + NORM_EPS)


def _ada_kernel(c_ref, w_ref, b_ref, o_ref):
    c = c_ref[...]
    cond = c * jax.nn.sigmoid(c)
    o_ref[0] = jnp.dot(cond, w_ref[0], preferred_element_type=F32,
                       precision=lax.Precision.HIGHEST) + b_ref[0]


def _ada(c_all, w_ada, b_ada):
    depth, d, _ = w_ada.shape
    bp = c_all.shape[0]
    b3 = b_ada.reshape(depth * N_MOD, 1, d)
    out = pl.pallas_call(
        _ada_kernel,
        grid=(depth, N_MOD),
        in_specs=[
            pl.BlockSpec((bp, d), lambda l, k: (0, 0)),
            pl.BlockSpec((1, d, d), lambda l, k: (l, 0, k)),
            pl.BlockSpec((1, 1, d), lambda l, k: (l * N_MOD + k, 0, 0)),
        ],
        out_specs=pl.BlockSpec((1, bp, d), lambda l, k: (l * N_MOD + k, 0, 0)),
        out_shape=jax.ShapeDtypeStruct((depth * N_MOD, bp, d), F32),
        compiler_params=_params(("arbitrary", "arbitrary")),
        name="ada_mod",
    )(c_all, w_ada, b3)
    return out.reshape(depth * N_MOD * bp, 1, d)


class _Geom:
    def __init__(self, bp, sp, bs, ss, d):
        self.bp, self.sp, self.bs, self.ss, self.d = bp, sp, bs, ss, d
        self.tp = bp * sp
        self.t = self.tp + bs * ss
        self.nb = bp + bs
        self.nb_pad = -(-self.nb // 8) * 8

    def batch_of_tile(self, i, rows):
        npt = self.tp // rows
        return jnp.where(i < npt, i // (self.sp // rows), self.bp + (i - npt) // (self.ss // rows))

    def mod_index(self, layer, k, i, rows):
        return (layer * N_MOD + k) * self.nb_pad + self.batch_of_tile(i, rows)


def _inproj_kernel(x_ref, g_ref, scale_ref, shift_ref, w_ref, z_ref, h_scr):
    @pl.when(pl.program_id(1) == 0)
    def _():
        h = _rms(x_ref[...]) * g_ref[0] * (1.0 + scale_ref[0]) + shift_ref[0]
        h_scr[...] = h.astype(BF16)

    z_ref[...] = jnp.dot(h_scr[...], w_ref[...], preferred_element_type=F32).astype(BF16)


def _inproj(geom, layer, x, g_pre, mod, w_in_l):
    t, d = x.shape
    n = w_in_l.shape[1]
    tm = _tile(math.gcd(geom.sp, geom.ss), INPROJ_ROWS, ROW_ALIGN)
    tn = _tile(n, INPROJ_COLS, 128)
    return pl.pallas_call(
        _inproj_kernel,
        grid=(t // tm, n // tn),
        in_specs=[
            pl.BlockSpec((tm, d), lambda i, j: (i, 0)),
            pl.BlockSpec((1, 1, d), lambda i, j: (layer, 0, 0)),
            pl.BlockSpec((1, 1, d), lambda i, j: (geom.mod_index(layer, 1, i, tm), 0, 0)),
            pl.BlockSpec((1, 1, d), lambda i, j: (geom.mod_index(layer, 0, i, tm), 0, 0)),
            pl.BlockSpec((d, tn), lambda i, j: (0, j)),
        ],
        out_specs=pl.BlockSpec((tm, tn), lambda i, j: (i, j)),
        out_shape=jax.ShapeDtypeStruct((t, n), BF16),
        scratch_shapes=[pltpu.VMEM((tm, d), BF16)],
        compiler_params=_params(("arbitrary", "arbitrary")),
        name="in_proj",
    )(x, g_pre, mod, mod, w_in_l)


def _attn_kernel(qblk_ref, kvblk_ref, first_ref, last_ref, qpos_ref, kpos_ref,
                 q_ref, k_ref, v_ref, slope_ref, lam_ref, gsub_ref, o_ref,
                 m_scr, l_scr, acc_scr, *, head_dim, kv_sub):
    it = pl.program_id(1)
    tq = q_ref.shape[0]
    n_sub = k_ref.shape[0] // kv_sub

    @pl.when(first_ref[it] == 1)
    def _():
        m_scr[...] = jnp.full(m_scr.shape, -jnp.inf, F32)
        l_scr[...] = jnp.zeros(l_scr.shape, F32)
        acc_scr[...] = jnp.zeros(acc_scr.shape, F32)

    q = q_ref[...]
    lane = lax.broadcasted_iota(jnp.int32, q.shape, 1)
    scale = jnp.asarray(head_dim ** -0.5, BF16)
    zero = jnp.zeros_like(q)
    q_maps = (jnp.where(lane < head_dim, q, zero) * scale, jnp.where(lane >= head_dim, q, zero) * scale)
    slope = slope_ref[0][:, :1]
    rel = (lax.broadcasted_iota(jnp.int32, (tq, kv_sub), 0)
           - lax.broadcasted_iota(jnp.int32, (tq, kv_sub), 1))
    off = qpos_ref[it] - kpos_ref[it]

    for c in range(n_sub):
        k = k_ref[c * kv_sub:(c + 1) * kv_sub, :]
        v = v_ref[c * kv_sub:(c + 1) * kv_sub, :]
        bias = slope * jnp.abs(rel + (off - c * kv_sub)).astype(F32)
        for m in range(2):
            s = lax.dot_general(q_maps[m], k, (((1,), (1,)), ((), ())), preferred_element_type=F32) - bias
            m_old = m_scr[m]
            m_new = jnp.maximum(m_old, jnp.max(s, axis=-1, keepdims=True))
            p = jnp.exp(s - m_new)
            alpha = jnp.exp(m_old - m_new)
            l_scr[m] = alpha * l_scr[m] + jnp.sum(p, axis=-1, keepdims=True)
            acc_scr[m] = alpha * acc_scr[m] + jnp.dot(p.astype(BF16), v, preferred_element_type=F32)
            m_scr[m] = m_new

    @pl.when(last_ref[it] == 1)
    def _():
        lp = lam_ref[0]
        lam_init = lp[4:5, :1]
        lam = (jnp.exp(jnp.sum(lp[0:1] * lp[1:2], axis=-1, keepdims=True))
               - jnp.exp(jnp.sum(lp[2:3] * lp[3:4], axis=-1, keepdims=True)) + lam_init)
        o = acc_scr[0] / l_scr[0] - lam * (acc_scr[1] / l_scr[1])
        o_ref[...] = (_rms(o) * gsub_ref[0] * (1.0 - lam_init)).astype(o_ref.dtype)


def _attn_items(geom, tq, chunk):
    qblk, kvblk, first, last, qpos, kpos = [], [], [], [], [], []
    for off, nb, s in ((0, geom.bp, geom.sp), (geom.tp, geom.bs, geom.ss)):
        for b in range(nb):
            for qt in range(s // tq):
                for c in range(s // chunk):
                    qblk.append((off + b * s + qt * tq) // tq)
                    kvblk.append((off + b * s + c * chunk) // chunk)
                    first.append(int(c == 0))
                    last.append(int(c == s // chunk - 1))
                    qpos.append(qt * tq)
                    kpos.append(c * chunk)
    return [jnp.asarray(np.asarray(a, np.int32)) for a in (qblk, kvblk, first, last, qpos, kpos)]


def _attention(geom, layer, z, slopes, lam_pack, g_subln, n_heads, head_dim):
    t = z.shape[0]
    v_dim = 2 * head_dim
    g = math.gcd(geom.sp, geom.ss)
    tq = _tile(g, ATTN_Q_ROWS, ROW_ALIGN)
    chunk = _tile(g, ATTN_KV_CHUNK, ROW_ALIGN)
    kv_sub = _tile(chunk, ATTN_KV_SUB, ROW_ALIGN)
    items = _attn_items(geom, tq, chunk)
    n_items = items[0].shape[0]
    k_col0 = n_heads
    v_col0 = 2 * n_heads
    grid_spec = pltpu.PrefetchScalarGridSpec(
        num_scalar_prefetch=6,
        grid=(n_heads, n_items),
        in_specs=[
            pl.BlockSpec((tq, v_dim), lambda h, it, qb, kb, *_: (qb[it], h)),
            pl.BlockSpec((chunk, v_dim), lambda h, it, qb, kb, *_: (kb[it], k_col0 + h)),
            pl.BlockSpec((chunk, v_dim), lambda h, it, qb, kb, *_: (kb[it], v_col0 + h)),
            pl.BlockSpec((1, 1, v_dim), lambda h, it, *_: (h, 0, 0)),
            pl.BlockSpec((1, 8, head_dim), lambda h, it, *_: (layer, 0, 0)),
            pl.BlockSpec((1, 1, v_dim), lambda h, it, *_: (layer, 0, 0)),
        ],
        out_specs=pl.BlockSpec((tq, v_dim), lambda h, it, qb, kb, *_: (qb[it], h)),
        scratch_shapes=[
            pltpu.VMEM((2, tq, 1), F32),
            pltpu.VMEM((2, tq, 1), F32),
            pltpu.VMEM((2, tq, v_dim), F32),
        ],
    )
    return pl.pallas_call(
        functools.partial(_attn_kernel, head_dim=head_dim, kv_sub=kv_sub),
        grid_spec=grid_spec,
        out_shape=jax.ShapeDtypeStruct((t, n_heads * v_dim), BF16),
        compiler_params=_params(("arbitrary", "arbitrary")),
        name="diff_attn",
    )(*items, z, z, z, slopes, lam_pack, g_subln)


def _outproj_kernel(attn_ref, cb_ref, cc_ref, cx_ref, ga_ref, gc_ref,
                    ccp_ref, cxp_ref, ccn_ref, cxn_ref,
                    x_ref, convw_ref, wout_ref, gpost_ref, gatem_ref, gpre_ref, scalef_ref, shiftf_ref,
                    *rest, geom, moe):
    if moe:
        wr_ref, xo_ref, h_ref, tope_ref, topw_ref = rest
    else:
        xo_ref, h_ref = rest
    tm = x_ref.shape[0]
    row0 = pl.program_id(0) * tm
    seq = jnp.where(row0 < geom.tp, geom.sp, geom.ss)
    rel0 = jnp.where(row0 < geom.tp, row0, row0 - geom.tp)
    not_start = (rel0 % seq != 0).astype(F32)
    not_end = ((rel0 + tm) % seq != 0).astype(F32)

    u = cc_ref[...].astype(F32) * cx_ref[...].astype(F32)
    halo = ccp_ref.shape[0]
    u_prev = (ccp_ref[halo - 1:halo, :].astype(F32) * cxp_ref[halo - 1:halo, :].astype(F32)) * not_start
    u_next = (ccn_ref[0:1, :].astype(F32) * cxn_ref[0:1, :].astype(F32)) * not_end
    row = lax.broadcasted_iota(jnp.int32, u.shape, 0)
    u_m1 = jnp.where(row == 0, u_prev, pltpu.roll(u, 1, 0))
    u_p1 = jnp.where(row == tm - 1, u_next, pltpu.roll(u, tm - 1, 0))
    cw = convw_ref[0]
    conv = u_m1 * cw[0:1] + u * cw[1:2] + u_p1 * cw[2:3]
    short = cb_ref[...].astype(F32) * conv
    merged = (jax.nn.sigmoid(ga_ref[...].astype(F32)) * attn_ref[...].astype(F32)
              + jax.nn.sigmoid(gc_ref[...].astype(F32)) * short)
    o = jnp.dot(merged.astype(BF16), wout_ref[...], preferred_element_type=F32)
    x_new = x_ref[...] + gatem_ref[0] * (_rms(o) * gpost_ref[0])
    xo_ref[...] = x_new
    h = _rms(x_new) * gpre_ref[0] * (1.0 + scalef_ref[0]) + shiftf_ref[0]
    h_ref[...] = h.astype(h_ref.dtype)

    if moe:
        n_e = wr_ref.shape[1]
        logits = jnp.dot(h, wr_ref[...], preferred_element_type=F32, precision=lax.Precision.HIGHEST)
        lane = lax.broadcasted_iota(jnp.int32, logits.shape, 1).astype(F32)
        m1 = jnp.max(logits, axis=-1, keepdims=True)
        i1 = jnp.min(jnp.where(logits == m1, lane, float(n_e)), axis=-1, keepdims=True)
        rest_l = jnp.where(lane == i1, -jnp.inf, logits)
        m2 = jnp.max(rest_l, axis=-1, keepdims=True)
        i2 = jnp.min(jnp.where(rest_l == m2, lane, float(n_e)), axis=-1, keepdims=True)
        e = jnp.exp(m2 - m1)
        w1 = 1.0 / (1.0 + e)
        w2 = e / (1.0 + e)
        tope_ref[...] = jnp.where(lane == 0.0, i1, jnp.where(lane == 1.0, i2, 0.0)).astype(jnp.int32)
        topw_ref[...] = jnp.where(lane == 0.0, w1, jnp.where(lane == 1.0, w2, 0.0))


def _outproj(geom, layer, attn, z, x, conv_w, w_out_l, g_post, g_pre, mod, w_router_l):
    t, d = x.shape
    moe = w_router_l is not None
    tm = _tile(math.gcd(geom.sp, geom.ss), OUTPROJ_ROWS, ROW_ALIGN)
    halo = ROW_ALIGN
    nh = tm // halo
    last_halo = t // halo - 1
    zcol = lambda c: pl.BlockSpec((tm, d), lambda i: (i, c))
    vec = lambda fn: pl.BlockSpec((1, 1, d), fn)
    modv = lambda k: vec(lambda i: (geom.mod_index(layer, k, i, tm), 0, 0))
    in_specs = [
        pl.BlockSpec((tm, d), lambda i: (i, 0)),
        zcol(3), zcol(4), zcol(5), zcol(6), zcol(7),
        pl.BlockSpec((halo, d), lambda i: (jnp.maximum(i * nh - 1, 0), 4)),
        pl.BlockSpec((halo, d), lambda i: (jnp.maximum(i * nh - 1, 0), 5)),
        pl.BlockSpec((halo, d), lambda i: (jnp.minimum((i + 1) * nh, last_halo), 4)),
        pl.BlockSpec((halo, d), lambda i: (jnp.minimum((i + 1) * nh, last_halo), 5)),
        pl.BlockSpec((tm, d), lambda i: (i, 0)),
        pl.BlockSpec((1, conv_w.shape[1], d), lambda i: (layer, 0, 0)),
        pl.BlockSpec((d, d), lambda i: (0, 0)),
        vec(lambda i: (layer, 0, 0)),
        modv(2),
        vec(lambda i: (layer, 0, 0)),
        modv(4), modv(3),
    ]
    args = [attn, z, z, z, z, z, z, z, z, z, x, conv_w, w_out_l, g_post, mod, g_pre, mod, mod]
    out_specs = [pl.BlockSpec((tm, d), lambda i: (i, 0)), pl.BlockSpec((tm, d), lambda i: (i, 0))]
    out_shape = [jax.ShapeDtypeStruct((t, d), F32), jax.ShapeDtypeStruct((t, d), F32 if moe else BF16)]
    if moe:
        n_e = w_router_l.shape[1]
        in_specs.append(pl.BlockSpec((d, n_e), lambda i: (0, 0)))
        args.append(w_router_l)
        out_specs += [pl.BlockSpec((tm, n_e), lambda i: (i, 0))] * 2
        out_shape += [jax.ShapeDtypeStruct((t, n_e), jnp.int32), jax.ShapeDtypeStruct((t, n_e), F32)]
    return pl.pallas_call(
        functools.partial(_outproj_kernel, geom=geom, moe=moe),
        grid=(t // tm,),
        in_specs=in_specs,
        out_specs=out_specs,
        out_shape=out_shape,
        compiler_params=_params(("arbitrary",)),
        name="out_proj_moe" if moe else "out_proj",
    )(*args)


def _swiglu_partial(h, wg, wu, wd):
    g = jnp.dot(h, wg, preferred_element_type=F32)
    u = jnp.dot(h, wu, preferred_element_type=F32)
    a = (g * jax.nn.sigmoid(g)) * u
    return jnp.dot(a.astype(BF16), wd, preferred_element_type=F32)


def _ffn_kernel(h_ref, x_ref, wg_ref, wu_ref, wd_ref, gpost_ref, gate_ref, xo_ref, acc_scr):
    f = pl.program_id(1)
    part = _swiglu_partial(h_ref[...], wg_ref[...], wu_ref[...], wd_ref[...])

    @pl.when(f == 0)
    def _():
        acc_scr[...] = part

    @pl.when(f > 0)
    def _():
        acc_scr[...] += part

    @pl.when(f == pl.num_programs(1) - 1)
    def _():
        xo_ref[...] = x_ref[...] + gate_ref[0] * (_rms(acc_scr[...]) * gpost_ref[0])


def _ffn(geom, layer, h, x, wg, wu, wd, g_post, mod):
    t, d = x.shape
    ff = wg.shape[1]
    tm = _tile(math.gcd(geom.sp, geom.ss), FFN_ROWS, ROW_ALIGN)
    tf = _tile(ff, FFN_COLS, 128)
    return pl.pallas_call(
        _ffn_kernel,
        grid=(t // tm, ff // tf),
        in_specs=[
            pl.BlockSpec((tm, d), lambda i, f: (i, 0)),
            pl.BlockSpec((tm, d), lambda i, f: (i, 0)),
            pl.BlockSpec((d, tf), lambda i, f: (0, f)),
            pl.BlockSpec((d, tf), lambda i, f: (0, f)),
            pl.BlockSpec((tf, d), lambda i, f: (f, 0)),
            pl.BlockSpec((1, 1, d), lambda i, f: (layer, 0, 0)),
            pl.BlockSpec((1, 1, d), lambda i, f: (geom.mod_index(layer, 5, i, tm), 0, 0)),
        ],
        out_specs=pl.BlockSpec((tm, d), lambda i, f: (i, 0)),
        out_shape=jax.ShapeDtypeStruct((t, d), F32),
        scratch_shapes=[pltpu.VMEM((tm, d), F32)],
        compiler_params=_params(("arbitrary", "arbitrary")),
        name="dense_ffn",
    )(h, x, wg, wu, wd, g_post, mod)


def _gather_kernel(idx_ref, src_ref, dst_ref, sem):
    rows = idx_ref.shape[0]
    base = pl.program_id(0) * rows

    def issue(r, carry):
        pltpu.make_async_copy(src_ref.at[pl.ds(idx_ref[r], 1)], dst_ref.at[pl.ds(base + r, 1)], sem).start()
        return carry

    lax.fori_loop(0, rows, issue, 0)
    pltpu.make_async_copy(src_ref.at[pl.ds(0, rows)], dst_ref.at[pl.ds(base, rows)], sem).wait()


def _gather_rows(src, row_tok):
    n_pad = row_tok.shape[0]
    rows = _tile(n_pad, GATHER_ROWS, SMEM_INDEX_ALIGN)
    return pl.pallas_call(
        _gather_kernel,
        grid=(n_pad // rows,),
        in_specs=[
            pl.BlockSpec((rows,), lambda i: (i,), memory_space=pltpu.SMEM),
            pl.BlockSpec(memory_space=pl.ANY),
        ],
        out_specs=pl.BlockSpec(memory_space=pl.ANY),
        out_shape=jax.ShapeDtypeStruct((n_pad, src.shape[1]), src.dtype),
        scratch_shapes=[pltpu.SemaphoreType.DMA(())],
        compiler_params=_params(("arbitrary",)),
        name="moe_gather",
    )(row_tok, src)


def _expert_kernel(be_ref, nused_ref, x_ref, wg_ref, wu_ref, wd_ref, y_ref, xb_scr, acc_scr):
    b = pl.program_id(0)
    f = pl.program_id(1)

    @pl.when(b < nused_ref[0])
    def _():
        @pl.when(f == 0)
        def _():
            xb_scr[...] = x_ref[...].astype(BF16)

        part = _swiglu_partial(xb_scr[...], wg_ref[0], wu_ref[0], wd_ref[0])

        @pl.when(f == 0)
        def _():
            acc_scr[...] = part

        @pl.when(f > 0)
        def _():
            acc_scr[...] += part

        @pl.when(f == pl.num_programs(1) - 1)
        def _():
            y_ref[...] = acc_scr[...]

    @pl.when(jnp.logical_and(b >= nused_ref[0], f == pl.num_programs(1) - 1))
    def _():
        y_ref[...] = jnp.zeros(y_ref.shape, y_ref.dtype)


def _experts(x_rows, block_e, n_used, wg, wu, wd, blk):
    n_pad, d = x_rows.shape
    ff = wg.shape[2]
    tf = _tile(ff, MOE_COLS, 128)
    grid_spec = pltpu.PrefetchScalarGridSpec(
        num_scalar_prefetch=2,
        grid=(n_pad // blk, ff // tf),
        in_specs=[
            pl.BlockSpec((blk, d), lambda b, f, be, nu: (b, 0)),
            pl.BlockSpec((1, d, tf), lambda b, f, be, nu: (be[b], 0, f)),
            pl.BlockSpec((1, d, tf), lambda b, f, be, nu: (be[b], 0, f)),
            pl.BlockSpec((1, tf, d), lambda b, f, be, nu: (be[b], f, 0)),
        ],
        out_specs=pl.BlockSpec((blk, d), lambda b, f, be, nu: (b, 0)),
        scratch_shapes=[pltpu.VMEM((blk, d), BF16), pltpu.VMEM((blk, d), F32)],
    )
    return pl.pallas_call(
        _expert_kernel,
        grid_spec=grid_spec,
        out_shape=jax.ShapeDtypeStruct((n_pad, d), F32),
        compiler_params=_params(("arbitrary", "arbitrary")),
        name="expert_ffn",
    )(block_e, n_used, x_rows, wg, wu, wd)


def _combine_kernel(dest_ref, y_ref, w_ref, x_ref, gpost_ref, gate_ref, xo_ref, buf, sem):
    tc = x_ref.shape[0]

    def issue(r, carry):
        for k in range(TOP_K):
            pltpu.make_async_copy(y_ref.at[pl.ds(dest_ref[TOP_K * r + k], 1)], buf.at[k, pl.ds(r, 1)], sem).start()
        return carry

    lax.fori_loop(0, tc, issue, 0)
    for k in range(TOP_K):
        pltpu.make_async_copy(y_ref.at[pl.ds(0, tc)], buf.at[k], sem).wait()
    w = w_ref[...]
    o = w[:, 0:1] * buf[0]
    for k in range(1, TOP_K):
        o = o + w[:, k:k + 1] * buf[k]
    xo_ref[...] = x_ref[...] + gate_ref[0] * (_rms(o) * gpost_ref[0])


def _combine(geom, layer, y_rows, dest, top_w, x, g_post, mod):
    t, d = x.shape
    n_e = top_w.shape[1]
    tc = _tile(math.gcd(geom.sp, geom.ss), COMBINE_ROWS, SMEM_INDEX_ALIGN // TOP_K)
    return pl.pallas_call(
        _combine_kernel,
        grid=(t // tc,),
        in_specs=[
            pl.BlockSpec((TOP_K * tc,), lambda i: (i,), memory_space=pltpu.SMEM),
            pl.BlockSpec(memory_space=pl.ANY),
            pl.BlockSpec((tc, n_e), lambda i: (i, 0)),
            pl.BlockSpec((tc, d), lambda i: (i, 0)),
            pl.BlockSpec((1, 1, d), lambda i: (layer, 0, 0)),
            pl.BlockSpec((1, 1, d), lambda i: (geom.mod_index(layer, 5, i, tc), 0, 0)),
        ],
        out_specs=pl.BlockSpec((tc, d), lambda i: (i, 0)),
        out_shape=jax.ShapeDtypeStruct((t, d), F32),
        scratch_shapes=[pltpu.VMEM((TOP_K, tc, d), F32), pltpu.SemaphoreType.DMA(())],
        compiler_params=_params(("arbitrary",)),
        name="moe_combine",
    )(dest, y_rows, top_w, x, g_post, mod)


def _route(top_e, n_experts, blk):
    t = top_e.shape[0]
    n_assign = t * TOP_K
    n_blocks = -(-(n_assign + n_experts * (blk - 1)) // blk)
    n_pad = n_blocks * blk
    flat_e = top_e[:, :TOP_K].reshape(-1)
    onehot = (flat_e[:, None] == jnp.arange(n_experts, dtype=jnp.int32)[None, :]).astype(jnp.int32)
    rank = jnp.sum((jnp.cumsum(onehot, axis=0) - onehot) * onehot, axis=1)
    counts = jnp.sum(onehot, axis=0)
    padded = (counts + blk - 1) // blk * blk
    end_pad = jnp.cumsum(padded)
    start_pad = end_pad - padded
    dest = (start_pad[flat_e] + rank).astype(jnp.int32)
    flat_tok = jnp.arange(n_assign, dtype=jnp.int32) // TOP_K
    row_tok = jnp.zeros((n_pad,), jnp.int32).at[dest].set(flat_tok)
    block_e = jnp.minimum(
        jnp.searchsorted(end_pad, jnp.arange(n_blocks, dtype=jnp.int32) * blk, side='right'),
        n_experts - 1).astype(jnp.int32)
    n_used = (end_pad[-1:] // blk).astype(jnp.int32)
    return dest, row_tok, block_e, n_used


def kernel(x_prompt, x_sample, c_prompt, c_sample, w_ada, b_ada, g_mix_pre, g_mix_post, g_ffn_pre, g_ffn_post,
           w_in, lam_q1, lam_k1, lam_q2, lam_k2, g_subln, conv_w, w_out, w_ffn_gate, w_ffn_up, w_ffn_down,
           w_router, w_exp_gate, w_exp_up, w_exp_down):
    bp, sp, d = x_prompt.shape
    bs, ss, _ = x_sample.shape
    depth = w_in.shape[0]
    head_dim = lam_q1.shape[1]
    v_dim = g_subln.shape[1]
    n_heads = d // v_dim
    n_experts = w_router.shape[2]
    assert v_dim == 2 * head_dim and w_in.shape[2] == 8 * d
    geom = _Geom(bp, sp, bs, ss, d)

    x = jnp.concatenate([x_prompt.reshape(bp * sp, d), x_sample.reshape(bs * ss, d)], axis=0)
    c_all = jnp.concatenate([c_prompt, c_sample, jnp.zeros((geom.nb_pad - geom.nb, d), F32)], axis=0)
    mod = _ada(c_all, w_ada, b_ada)

    vec3 = lambda a: a.reshape(a.shape[0], 1, a.shape[1])
    g_mix_pre3, g_mix_post3, g_ffn_pre3, g_ffn_post3, g_subln3 = map(
        vec3, (g_mix_pre, g_mix_post, g_ffn_pre, g_ffn_post, g_subln))
    slopes = 2.0 ** (-ALIBI_MAX * np.arange(1, n_heads + 1, dtype=np.float64) / n_heads)
    slopes = jnp.asarray(np.broadcast_to(slopes[:, None, None], (n_heads, 1, v_dim)).astype(np.float32))
    lam_init = np.asarray([0.8 - 0.6 * math.exp(-0.3 * l) for l in range(depth)], np.float32)
    lam_pack = jnp.stack(
        [lam_q1, lam_k1, lam_q2, lam_k2, jnp.broadcast_to(jnp.asarray(lam_init)[:, None], lam_q1.shape)]
        + [jnp.zeros_like(lam_q1)] * 3, axis=1)

    for layer in range(depth):
        i = layer // 2
        moe = layer % 2 == 1
        z = _inproj(geom, layer, x, g_mix_pre3, mod, w_in[layer].astype(BF16))
        attn = _attention(geom, layer, z, slopes, lam_pack, g_subln3, n_heads, head_dim)
        outs = _outproj(geom, layer, attn, z, x, conv_w, w_out[layer].astype(BF16), g_mix_post3, g_ffn_pre3, mod,
                        w_router[i] if moe else None)
        if not moe:
            x, h = outs
            x = _ffn(geom, layer, h, x, w_ffn_gate[i].astype(BF16), w_ffn_up[i].astype(BF16),
                     w_ffn_down[i].astype(BF16), g_ffn_post3, mod)
        else:
            x, h, top_e, top_w = outs
            dest, row_tok, block_e, n_used = _route(top_e, n_experts, MOE_BLOCK)
            x_rows = _gather_rows(h, row_tok)
            y_rows = _experts(x_rows, block_e, n_used, w_exp_gate[i].astype(BF16), w_exp_up[i].astype(BF16),
                              w_exp_down[i].astype(BF16), MOE_BLOCK)
            x = _combine(geom, layer, y_rows, dest, top_w, x, g_ffn_post3, mod)

    y_prompt = x[:geom.tp].reshape(bp, sp, d)
    y_sample = x[geom.tp:].reshape(bs, ss, d)
    return (y_prompt, y_sample)
```

```python
import functools
import math

import numpy as np
import jax
import jax.numpy as jnp
from jax import lax
from jax.experimental import pallas as pl
from jax.experimental.pallas import tpu as pltpu

F32 = jnp.float32
BF16 = jnp.bfloat16

ALIBI_MAX = 8.0
NORM_EPS = 1e-6
TOP_K = 2
N_MOD = 6
LOG2E = math.log2(math.e)

VMEM_LIMIT_BYTES = 56 * 1024 * 1024
ROW_ALIGN = 16
LANES = 128
BF16_EXACT_INT = 256
SLOPE_PARTS = 3

INPROJ_ROWS, INPROJ_COLS = 1024, 2048
ATTN_Q_ROWS, ATTN_KV_CHUNK = 512, 2048
OUTPROJ_ROWS = 256
FFN_ROWS, FFN_COLS = 512, 1408
MOE_BLOCK, MOE_COLS = 1024, 896
GATHER_ROWS = 1024
COMBINE_ROWS = 512
SMEM_INDEX_ALIGN = 1024


def _tile(n, pref, align=1):
    t = min(n, pref)
    while t > 0:
        if n % t == 0 and t % align == 0:
            return t
        t -= 1
    raise ValueError(f"no tile for {n} (pref {pref}, align {align})")


def _params(sem):
    return pltpu.CompilerParams(dimension_semantics=sem, vmem_limit_bytes=VMEM_LIMIT_BYTES)


def _rms(x):
    return x * lax.rsqrt(jnp.mean(x * x, axis=-1, keepdims=True) + NORM_EPS)


def _row_spec(d, fn=None):
    return pl.BlockSpec((1, d), fn if fn is not None else (lambda *_: (0, 0)))


def _ada_kernel(c_ref, w_ref, b_ref, o_ref):
    c = c_ref[...]
    cond = c * jax.nn.sigmoid(c)
    o_ref[0] = jnp.dot(cond, w_ref[0], preferred_element_type=F32,
                       precision=lax.Precision.HIGHEST) + b_ref[0]


def _ada(c_all, w_ada, b_ada):
    depth, d, _ = w_ada.shape
    bp = c_all.shape[0]
    b3 = b_ada.reshape(depth * N_MOD, 1, d)
    out = pl.pallas_call(
        _ada_kernel,
        grid=(depth, N_MOD),
        in_specs=[
            pl.BlockSpec((bp, d), lambda l, k: (0, 0)),
            pl.BlockSpec((1, d, d), lambda l, k: (l, 0, k)),
            pl.BlockSpec((1, 1, d), lambda l, k: (l * N_MOD + k, 0, 0)),
        ],
        out_specs=pl.BlockSpec((1, bp, d), lambda l, k: (l * N_MOD + k, 0, 0)),
        out_shape=jax.ShapeDtypeStruct((depth * N_MOD, bp, d), F32),
        compiler_params=_params(("arbitrary", "arbitrary")),
        name="ada_mod",
    )(c_all, w_ada, b3)
    return out.reshape(depth, N_MOD * bp, 1, d)


class _Geom:
    def __init__(self, bp, sp, bs, ss, d):
        self.bp, self.sp, self.bs, self.ss, self.d = bp, sp, bs, ss, d
        self.tp = bp * sp
        self.t = self.tp + bs * ss
        self.nb = bp + bs
        self.nb_pad = -(-self.nb // 8) * 8
        self.row_gcd = math.gcd(sp, ss)

    def batch_of_tile(self, i, rows):
        npt = self.tp // rows
        return jnp.where(i < npt, i // (self.sp // rows), self.bp + (i - npt) // (self.ss // rows))

    def mod_spec(self, k, rows):
        return pl.BlockSpec((1, 1, self.d), lambda i, *_: (k * self.nb_pad + self.batch_of_tile(i, rows), 0, 0))


def _inproj_kernel(x_ref, g_ref, scale_ref, shift_ref, w_ref, cs_ref, z_ref, h_scr):
    @pl.when(pl.program_id(1) == 0)
    def _():
        h = _rms(x_ref[...]) * g_ref[...] * (1.0 + scale_ref[0]) + shift_ref[0]
        h_scr[...] = h.astype(BF16)

    z = jnp.dot(h_scr[...], w_ref[...], preferred_element_type=F32)
    z_ref[...] = (z * cs_ref[...]).astype(BF16)


def _inproj(geom, x, g_pre, mod_l, w_in_l, col_scale):
    t, d = x.shape
    n = w_in_l.shape[1]
    tm = _tile(geom.row_gcd, INPROJ_ROWS, ROW_ALIGN)
    tn = _tile(n, INPROJ_COLS, LANES)
    return pl.pallas_call(
        _inproj_kernel,
        grid=(t // tm, n // tn),
        in_specs=[
            pl.BlockSpec((tm, d), lambda i, j: (i, 0)),
            _row_spec(d),
            geom.mod_spec(1, tm),
            geom.mod_spec(0, tm),
            pl.BlockSpec((d, tn), lambda i, j: (0, j)),
            pl.BlockSpec((1, tn), lambda i, j: (0, j)),
        ],
        out_specs=pl.BlockSpec((tm, tn), lambda i, j: (i, j)),
        out_shape=jax.ShapeDtypeStruct((t, n), BF16),
        scratch_shapes=[pltpu.VMEM((tm, d), BF16)],
        compiler_params=_params(("arbitrary", "arbitrary")),
        name="in_proj",
    )(x, g_pre, mod_l, mod_l, w_in_l, col_scale)


def _attn_kernel(qblk_ref, kvblk_ref, first_ref, last_ref, d0_ref, dc_ref,
                 q_ref, k_ref, v_ref, qc_ref, kc_ref, bd_ref, ed0_ref, lam_ref, gsub_ref, o_ref,
                 s_scr, m_scr, acc_scr, *, head_dim):
    it = pl.program_id(1)
    tq, v_dim = q_ref.shape
    chunk = k_ref.shape[0]
    n_sub = chunk // tq

    @pl.when(first_ref[it] == 1)
    def _():
        m_scr[...] = jnp.full(m_scr.shape, -jnp.inf, F32)
        acc_scr[...] = jnp.zeros(acc_scr.shape, F32)

    q = q_ref[...]
    lane = lax.broadcasted_iota(jnp.int32, (tq, v_dim), 1)
    half = (lane < head_dim, lane >= head_dim)
    d0 = d0_ref[it]
    d0_v = jnp.full((1, v_dim), d0, jnp.int32).astype(F32)
    zero = jnp.zeros_like(q)
    for m in range(2):
        q_aug = jnp.where(half[m], q, zero) + qc_ref[0, m] + (d0_v * ed0_ref[m]).astype(BF16)
        for c in range(n_sub):
            delta = d0 - c * tq
            sgn = (delta > 0).astype(jnp.int32) - (delta < 0).astype(jnp.int32)
            sgn_v = jnp.full((1, v_dim), sgn, jnp.int32).astype(F32).astype(BF16)
            k_aug = jnp.where(half[m], k_ref[c * tq:(c + 1) * tq, :], sgn_v * kc_ref[0, m, c * tq:(c + 1) * tq, :])
            s_scr[m, :, c * tq:(c + 1) * tq] = lax.dot_general(
                q_aug, k_aug, (((1,), (1,)), ((), ())), preferred_element_type=F32)

    dc = dc_ref[it]
    flag = jnp.full((1, 1), (dc >= 0).astype(jnp.int32), jnp.int32).astype(F32)
    off = pl.multiple_of(jnp.maximum(dc, 0) * tq, tq)
    bias = flag * bd_ref[0]
    for m in range(2):
        s_scr[m, :, pl.ds(off, tq)] = s_scr[m, :, pl.ds(off, tq)] - bias

    ones_col = jnp.where(lax.broadcasted_iota(jnp.int32, (chunk, v_dim), 1) == 0, 1.0, 0.0).astype(BF16)
    v_aug = jnp.concatenate([v_ref[...], ones_col], axis=1)
    for m in range(2):
        s = s_scr[m]
        m_old = m_scr[m]
        m_new = jnp.maximum(m_old, jnp.max(s, axis=-1, keepdims=True))
        p = jnp.exp2(s - m_new).astype(BF16)
        alpha = jnp.exp2(m_old - m_new)
        acc_scr[m] = alpha * acc_scr[m] + jnp.dot(p, v_aug, preferred_element_type=F32)
        m_scr[m] = m_new

    @pl.when(last_ref[it] == 1)
    def _():
        lp = lam_ref[...]
        lam_init = lp[4:5, :1]
        lam = (jnp.exp(jnp.sum(lp[0:1] * lp[1:2], axis=-1, keepdims=True))
               - jnp.exp(jnp.sum(lp[2:3] * lp[3:4], axis=-1, keepdims=True)) + lam_init)
        a0 = acc_scr[0]
        a1 = acc_scr[1]
        o = a0[:, :v_dim] / a0[:, v_dim:v_dim + 1] - lam * (a1[:, :v_dim] / a1[:, v_dim:v_dim + 1])
        o_ref[...] = (_rms(o) * gsub_ref[...] * (1.0 - lam_init)).astype(o_ref.dtype)


def _attn_items(geom, tq, chunk):
    qblk, kvblk, first, last, d0s, dcs = [], [], [], [], [], []
    for off, nb, s in ((0, geom.bp, geom.sp), (geom.tp, geom.bs, geom.ss)):
        for b in range(nb):
            for qt in range(s // tq):
                for c in range(s // chunk):
                    qblk.append((off + b * s + qt * tq) // tq)
                    kvblk.append((off + b * s + c * chunk) // chunk)
                    first.append(int(c == 0))
                    last.append(int(c == s // chunk - 1))
                    d0 = qt * tq - c * chunk
                    d0s.append(d0)
                    dcs.append(d0 // tq if 0 <= d0 < chunk else -1)
    return [jnp.asarray(np.asarray(a, np.int32)) for a in (qblk, kvblk, first, last, d0s, dcs)]


def _bf16_parts(x, n):
    parts, rest = [], np.asarray(x, np.float64)
    for _ in range(n):
        p = rest.astype(BF16).astype(np.float64)
        parts.append(p)
        rest = rest - p
    return parts


def _attn_consts(n_heads, head_dim, tq, chunk, max_d0):
    v_dim = 2 * head_dim
    slopes = 2.0 ** (-ALIBI_MAX * np.arange(1, n_heads + 1, dtype=np.float64) / n_heads) * LOG2E
    sl = _bf16_parts(slopes, SLOPE_PARTS)
    r = np.arange(tq)
    j = np.arange(chunk)
    i_parts = [None, (r // BF16_EXACT_INT) * BF16_EXACT_INT, r % BF16_EXACT_INT]
    j_parts = [(j // BF16_EXACT_INT) * BF16_EXACT_INT, j % BF16_EXACT_INT]
    assert chunk <= BF16_EXACT_INT ** 2 and max_d0 % BF16_EXACT_INT == 0 and max_d0 <= BF16_EXACT_INT ** 2
    n_cols = SLOPE_PARTS * (len(i_parts) + len(j_parts))
    assert n_cols <= head_dim
    qc = np.zeros((n_heads, 2, tq, v_dim), np.float64)
    kc = np.zeros((n_heads, 2, chunk, v_dim), np.float64)
    ed0 = np.zeros((2, 1, v_dim), np.float32)
    for m in range(2):
        base = head_dim if m == 0 else 0
        col = base
        for a in range(SLOPE_PARTS):
            for jp in j_parts:
                qc[:, m, :, col] = sl[a][:, None]
                kc[:, m, :, col] = jp[None, :]
                col += 1
            for ip in i_parts:
                if ip is None:
                    ed0[m, 0, col] = 1.0
                else:
                    qc[:, m, :, col] = ip[None, :]
                kc[:, m, :, col] = -sl[a][:, None]
                col += 1
    bd = slopes[:, None, None] * np.abs(r[:, None] - r[None, :])[None]
    for arr in (qc, kc):
        assert np.array_equal(arr.astype(BF16).astype(np.float64), arr)
    return (jnp.asarray(qc.astype(BF16)), jnp.asarray(kc.astype(BF16)),
            jnp.asarray(bd.astype(np.float32)), jnp.asarray(ed0))


def _attention(geom, z, consts, items, lam_pack_l, g_subln_l, n_heads, head_dim, tq, chunk):
    t = z.shape[0]
    v_dim = 2 * head_dim
    qc, kc, bd, ed0 = consts
    n_items = items[0].shape[0]
    k_col0 = n_heads
    v_col0 = 2 * n_heads
    grid_spec = pltpu.PrefetchScalarGridSpec(
        num_scalar_prefetch=6,
        grid=(n_heads, n_items),
        in_specs=[
            pl.BlockSpec((tq, v_dim), lambda h, it, qb, kb, *_: (qb[it], h)),
            pl.BlockSpec((chunk, v_dim), lambda h, it, qb, kb, *_: (kb[it], k_col0 + h)),
            pl.BlockSpec((chunk, v_dim), lambda h, it, qb, kb, *_: (kb[it], v_col0 + h)),
            pl.BlockSpec((1, 2, tq, v_dim), lambda h, it, *_: (h, 0, 0, 0)),
            pl.BlockSpec((1, 2, chunk, v_dim), lambda h, it, *_: (h, 0, 0, 0)),
            pl.BlockSpec((1, tq, tq), lambda h, it, *_: (h, 0, 0)),
            pl.BlockSpec((2, 1, v_dim), lambda h, it, *_: (0, 0, 0)),
            pl.BlockSpec((8, head_dim), lambda h, it, *_: (0, 0)),
            _row_spec(v_dim),
        ],
        out_specs=pl.BlockSpec((tq, v_dim), lambda h, it, qb, kb, *_: (qb[it], h)),
        scratch_shapes=[
            pltpu.VMEM((2, tq, chunk), F32),
            pltpu.VMEM((2, tq, 1), F32),
            pltpu.VMEM((2, tq, 2 * v_dim), F32),
        ],
    )
    return pl.pallas_call(
        functools.partial(_attn_kernel, head_dim=head_dim),
        grid_spec=grid_spec,
        out_shape=jax.ShapeDtypeStruct((t, n_heads * v_dim), BF16),
        compiler_params=_params(("arbitrary", "arbitrary")),
        name="diff_attn",
    )(*items, z, z, z, qc, kc, bd, ed0, lam_pack_l, g_subln_l)


def _outproj_kernel(attn_ref, cb_ref, cc_ref, cx_ref, ga_ref, gc_ref,
                    ccp_ref, cxp_ref, ccn_ref, cxn_ref,
                    x_ref, convw_ref, wout_ref, gpost_ref, gatem_ref, gpre_ref, scalef_ref, shiftf_ref,
                    *rest, geom, moe):
    if moe:
        wr_ref, xo_ref, h_ref, tope_ref, topw_ref = rest
    else:
        xo_ref, h_ref = rest
    tm = x_ref.shape[0]
    row0 = pl.program_id(0) * tm
    seq = jnp.where(row0 < geom.tp, geom.sp, geom.ss)
    rel0 = jnp.where(row0 < geom.tp, row0, row0 - geom.tp)
    not_start = (rel0 % seq != 0).astype(F32)
    not_end = ((rel0 + tm) % seq != 0).astype(F32)

    u = cc_ref[...].astype(F32) * cx_ref[...].astype(F32)
    halo = ccp_ref.shape[0]
    u_prev = (ccp_ref[halo - 1:halo, :].astype(F32) * cxp_ref[halo - 1:halo, :].astype(F32)) * not_start
    u_next = (ccn_ref[0:1, :].astype(F32) * cxn_ref[0:1, :].astype(F32)) * not_end
    row = lax.broadcasted_iota(jnp.int32, u.shape, 0)
    u_m1 = jnp.where(row == 0, u_prev, pltpu.roll(u, 1, 0))
    u_p1 = jnp.where(row == tm - 1, u_next, pltpu.roll(u, tm - 1, 0))
    cw = convw_ref[...]
    conv = u_m1 * cw[0:1] + u * cw[1:2] + u_p1 * cw[2:3]
    short = cb_ref[...].astype(F32) * conv
    merged = (jax.nn.sigmoid(ga_ref[...].astype(F32)) * attn_ref[...].astype(F32)
              + jax.nn.sigmoid(gc_ref[...].astype(F32)) * short)
    o = jnp.dot(merged.astype(BF16), wout_ref[...], preferred_element_type=F32)
    x_new = x_ref[...] + gatem_ref[0] * (_rms(o) * gpost_ref[...])
    xo_ref[...] = x_new
    h = _rms(x_new) * gpre_ref[...] * (1.0 + scalef_ref[0]) + shiftf_ref[0]
    h_ref[...] = h.astype(h_ref.dtype)

    if moe:
        n_e = wr_ref.shape[1]
        logits = jnp.dot(h, wr_ref[...], preferred_element_type=F32, precision=lax.Precision.HIGHEST)
        lane = lax.broadcasted_iota(jnp.int32, logits.shape, 1).astype(F32)
        m1 = jnp.max(logits, axis=-1, keepdims=True)
        i1 = jnp.min(jnp.where(logits == m1, lane, float(n_e)), axis=-1, keepdims=True)
        rest_l = jnp.where(lane == i1, -jnp.inf, logits)
        m2 = jnp.max(rest_l, axis=-1, keepdims=True)
        i2 = jnp.min(jnp.where(rest_l == m2, lane, float(n_e)), axis=-1, keepdims=True)
        e = jnp.exp(m2 - m1)
        w1 = 1.0 / (1.0 + e)
        w2 = e / (1.0 + e)
        tope_ref[...] = jnp.where(lane == 0.0, i1, jnp.where(lane == 1.0, i2, 0.0)).astype(jnp.int32)
        topw_ref[...] = jnp.where(lane == 0.0, w1, jnp.where(lane == 1.0, w2, 0.0))


def _outproj(geom, attn, z, x, conv_w_l, w_out_l, g_post, g_pre, mod_l, w_router_l):
    t, d = x.shape
    moe = w_router_l is not None
    tm = _tile(geom.row_gcd, OUTPROJ_ROWS, ROW_ALIGN)
    halo = ROW_ALIGN
    nh = tm // halo
    last_halo = t // halo - 1
    zcol = lambda c: pl.BlockSpec((tm, d), lambda i: (i, c))
    in_specs = [
        pl.BlockSpec((tm, d), lambda i: (i, 0)),
        zcol(3), zcol(4), zcol(5), zcol(6), zcol(7),
        pl.BlockSpec((halo, d), lambda i: (jnp.maximum(i * nh - 1, 0), 4)),
        pl.BlockSpec((halo, d), lambda i: (jnp.maximum(i * nh - 1, 0), 5)),
        pl.BlockSpec((halo, d), lambda i: (jnp.minimum((i + 1) * nh, last_halo), 4)),
        pl.BlockSpec((halo, d), lambda i: (jnp.minimum((i + 1) * nh, last_halo), 5)),
        pl.BlockSpec((tm, d), lambda i: (i, 0)),
        pl.BlockSpec(conv_w_l.shape, lambda i: (0, 0)),
        pl.BlockSpec((d, d), lambda i: (0, 0)),
        _row_spec(d),
        geom.mod_spec(2, tm),
        _row_spec(d),
        geom.mod_spec(4, tm), geom.mod_spec(3, tm),
    ]
    args = [attn, z, z, z, z, z, z, z, z, z, x, conv_w_l, w_out_l, g_post, mod_l, g_pre, mod_l, mod_l]
    out_specs = [pl.BlockSpec((tm, d), lambda i: (i, 0)), pl.BlockSpec((tm, d), lambda i: (i, 0))]
    out_shape = [jax.ShapeDtypeStruct((t, d), F32), jax.ShapeDtypeStruct((t, d), F32 if moe else BF16)]
    if moe:
        n_e = w_router_l.shape[1]
        in_specs.append(pl.BlockSpec((d, n_e), lambda i: (0, 0)))
        args.append(w_router_l)
        out_specs += [pl.BlockSpec((tm, n_e), lambda i: (i, 0))] * 2
        out_shape += [jax.ShapeDtypeStruct((t, n_e), jnp.int32), jax.ShapeDtypeStruct((t, n_e), F32)]
    return pl.pallas_call(
        functools.partial(_outproj_kernel, geom=geom, moe=moe),
        grid=(t // tm,),
        in_specs=in_specs,
        out_specs=out_specs,
        out_shape=out_shape,
        compiler_params=_params(("arbitrary",)),
        name="out_proj_moe" if moe else "out_proj",
    )(*args)


def _swiglu_partial(h, wg, wu, wd):
    g = jnp.dot(h, wg, preferred_element_type=F32)
    u = jnp.dot(h, wu, preferred_element_type=F32)
    a = (g * jax.nn.sigmoid(g)) * u
    return jnp.dot(a.astype(BF16), wd, preferred_element_type=F32)


def _ffn_kernel(h_ref, x_ref, wg_ref, wu_ref, wd_ref, gpost_ref, gate_ref, xo_ref, acc_scr):
    f = pl.program_id(1)
    part = _swiglu_partial(h_ref[...], wg_ref[...], wu_ref[...], wd_ref[...])

    @pl.when(f == 0)
    def _():
        acc_scr[...] = part

    @pl.when(f > 0)
    def _():
        acc_scr[...] += part

    @pl.when(f == pl.num_programs(1) - 1)
    def _():
        xo_ref[...] = x_ref[...] + gate_ref[0] * (_rms(acc_scr[...]) * gpost_ref[...])


def _ffn(geom, h, x, wg, wu, wd, g_post, mod_l):
    t, d = x.shape
    ff = wg.shape[1]
    tm = _tile(geom.row_gcd, FFN_ROWS, ROW_ALIGN)
    tf = _tile(ff, FFN_COLS, LANES)
    return pl.pallas_call(
        _ffn_kernel,
        grid=(t // tm, ff // tf),
        in_specs=[
            pl.BlockSpec((tm, d), lambda i, f: (i, 0)),
            pl.BlockSpec((tm, d), lambda i, f: (i, 0)),
            pl.BlockSpec((d, tf), lambda i, f: (0, f)),
            pl.BlockSpec((d, tf), lambda i, f: (0, f)),
            pl.BlockSpec((tf, d), lambda i, f: (f, 0)),
            _row_spec(d),
            geom.mod_spec(5, tm),
        ],
        out_specs=pl.BlockSpec((tm, d), lambda i, f: (i, 0)),
        out_shape=jax.ShapeDtypeStruct((t, d), F32),
        scratch_shapes=[pltpu.VMEM((tm, d), F32)],
        compiler_params=_params(("arbitrary", "arbitrary")),
        name="dense_ffn",
    )(h, x, wg, wu, wd, g_post, mod_l)


def _gather_kernel(idx_ref, src_ref, o_ref, buf, sem):
    rows = buf.shape[0]

    def issue(r, carry):
        pltpu.make_async_copy(src_ref.at[pl.ds(idx_ref[r], 1)], buf.at[pl.ds(r, 1)], sem).start()
        return carry

    lax.fori_loop(0, rows, issue, 0)
    pltpu.make_async_copy(src_ref.at[pl.ds(0, rows)], buf, sem).wait()
    o_ref[...] = buf[...].astype(o_ref.dtype)


def _gather_rows(src, row_tok):
    n_pad = row_tok.shape[0]
    d = src.shape[1]
    rows = _tile(n_pad, GATHER_ROWS, SMEM_INDEX_ALIGN)
    return pl.pallas_call(
        _gather_kernel,
        grid=(n_pad // rows,),
        in_specs=[
            pl.BlockSpec((rows,), lambda i: (i,), memory_space=pltpu.SMEM),
            pl.BlockSpec(memory_space=pl.ANY),
        ],
        out_specs=pl.BlockSpec((rows, d), lambda i: (i, 0)),
        out_shape=jax.ShapeDtypeStruct((n_pad, d), BF16),
        scratch_shapes=[pltpu.VMEM((rows, d), src.dtype), pltpu.SemaphoreType.DMA(())],
        compiler_params=_params(("arbitrary",)),
        name="moe_gather",
    )(row_tok, src)


def _expert_kernel(be_ref, nused_ref, x_ref, wg_ref, wu_ref, wd_ref, y_ref, acc_scr):
    b = pl.program_id(0)
    f = pl.program_id(1)
    last = pl.num_programs(1) - 1

    @pl.when(b < nused_ref[0])
    def _():
        part = _swiglu_partial(x_ref[...], wg_ref[0], wu_ref[0], wd_ref[0])

        @pl.when(f == 0)
        def _():
            acc_scr[...] = part

        @pl.when(f > 0)
        def _():
            acc_scr[...] += part

        @pl.when(f == last)
        def _():
            y_ref[...] = acc_scr[...]

    @pl.when(jnp.logical_and(b >= nused_ref[0], f == last))
    def _():
        y_ref[...] = jnp.zeros(y_ref.shape, y_ref.dtype)


def _experts(x_rows, block_e, n_used, wg, wu, wd, blk):
    n_pad, d = x_rows.shape
    ff = wg.shape[2]
    tf = _tile(ff, MOE_COLS, LANES)
    grid_spec = pltpu.PrefetchScalarGridSpec(
        num_scalar_prefetch=2,
        grid=(n_pad // blk, ff // tf),
        in_specs=[
            pl.BlockSpec((blk, d), lambda b, f, be, nu: (b, 0)),
            pl.BlockSpec((1, d, tf), lambda b, f, be, nu: (be[b], 0, f)),
            pl.BlockSpec((1, d, tf), lambda b, f, be, nu: (be[b], 0, f)),
            pl.BlockSpec((1, tf, d), lambda b, f, be, nu: (be[b], f, 0)),
        ],
        out_specs=pl.BlockSpec((blk, d), lambda b, f, be, nu: (b, 0)),
        scratch_shapes=[pltpu.VMEM((blk, d), F32)],
    )
    return pl.pallas_call(
        _expert_kernel,
        grid_spec=grid_spec,
        out_shape=jax.ShapeDtypeStruct((n_pad, d), F32),
        compiler_params=_params(("arbitrary", "arbitrary")),
        name="expert_ffn",
    )(block_e, n_used, x_rows, wg, wu, wd)


def _combine_kernel(dest_ref, y_ref, w_ref, x_ref, gpost_ref, gate_ref, xo_ref, buf, sem):
    tc = x_ref.shape[0]

    def issue(r, carry):
        for k in range(TOP_K):
            pltpu.make_async_copy(y_ref.at[pl.ds(dest_ref[TOP_K * r + k], 1)], buf.at[k, pl.ds(r, 1)], sem).start()
        return carry

    lax.fori_loop(0, tc, issue, 0)
    for k in range(TOP_K):
        pltpu.make_async_copy(y_ref.at[pl.ds(0, tc)], buf.at[k], sem).wait()
    w = w_ref[...]
    o = w[:, 0:1] * buf[0]
    for k in range(1, TOP_K):
        o = o + w[:, k:k + 1] * buf[k]
    xo_ref[...] = x_ref[...] + gate_ref[0] * (_rms(o) * gpost_ref[...])


def _combine(geom, y_rows, dest, top_w, x, g_post, mod_l):
    t, d = x.shape
    n_e = top_w.shape[1]
    tc = _tile(geom.row_gcd, COMBINE_ROWS, SMEM_INDEX_ALIGN // TOP_K)
    return pl.pallas_call(
        _combine_kernel,
        grid=(t // tc,),
        in_specs=[
            pl.BlockSpec((TOP_K * tc,), lambda i: (i,), memory_space=pltpu.SMEM),
            pl.BlockSpec(memory_space=pl.ANY),
            pl.BlockSpec((tc, n_e), lambda i: (i, 0)),
            pl.BlockSpec((tc, d), lambda i: (i, 0)),
            _row_spec(d),
            geom.mod_spec(5, tc),
        ],
        out_specs=pl.BlockSpec((tc, d), lambda i: (i, 0)),
        out_shape=jax.ShapeDtypeStruct((t, d), F32),
        scratch_shapes=[pltpu.VMEM((TOP_K, tc, d), F32), pltpu.SemaphoreType.DMA(())],
        compiler_params=_params(("arbitrary",)),
        name="moe_combine",
    )(dest, y_rows, top_w, x, g_post, mod_l)


def _route(top_e, n_experts, blk):
    t = top_e.shape[0]
    n_assign = t * TOP_K
    n_blocks = -(-(n_assign + n_experts * (blk - 1)) // blk)
    n_pad = n_blocks * blk
    flat_e = top_e[:, :TOP_K].reshape(-1)
    onehot = (flat_e[:, None] == jnp.arange(n_experts, dtype=jnp.int32)[None, :]).astype(jnp.int32)
    rank = jnp.sum((jnp.cumsum(onehot, axis=0) - onehot) * onehot, axis=1)
    counts = jnp.sum(onehot, axis=0)
    padded = (counts + blk - 1) // blk * blk
    end_pad = jnp.cumsum(padded)
    start_pad = end_pad - padded
    dest = (start_pad[flat_e] + rank).astype(jnp.int32)
    flat_tok = jnp.arange(n_assign, dtype=jnp.int32) // TOP_K
    row_tok = jnp.zeros((n_pad,), jnp.int32).at[dest].set(flat_tok)
    block_e = jnp.minimum(
        jnp.searchsorted(end_pad, jnp.arange(n_blocks, dtype=jnp.int32) * blk, side='right'),
        n_experts - 1).astype(jnp.int32)
    n_used = (end_pad[-1:] // blk).astype(jnp.int32)
    return dest, row_tok, block_e, n_used


def kernel(x_prompt, x_sample, c_prompt, c_sample, w_ada, b_ada, g_mix_pre, g_mix_post, g_ffn_pre, g_ffn_post,
           w_in, lam_q1, lam_k1, lam_q2, lam_k2, g_subln, conv_w, w_out, w_ffn_gate, w_ffn_up, w_ffn_down,
           w_router, w_exp_gate, w_exp_up, w_exp_down):
    bp, sp, d = x_prompt.shape
    bs, ss, _ = x_sample.shape
    depth = w_in.shape[0]
    head_dim = lam_q1.shape[1]
    v_dim = g_subln.shape[1]
    n_heads = d // v_dim
    n_experts = w_router.shape[2]
    n_in = w_in.shape[2]
    assert v_dim == 2 * head_dim and n_in == 8 * d
    geom = _Geom(bp, sp, bs, ss, d)

    x = jnp.concatenate([x_prompt.reshape(bp * sp, d), x_sample.reshape(bs * ss, d)], axis=0)
    c_all = jnp.concatenate([c_prompt, c_sample, jnp.zeros((geom.nb_pad - geom.nb, d), F32)], axis=0)
    mod = _ada(c_all, w_ada, b_ada)

    tq = _tile(geom.row_gcd, ATTN_Q_ROWS, ROW_ALIGN)
    chunk = _tile(geom.row_gcd, ATTN_KV_CHUNK, tq)
    attn_consts = _attn_consts(n_heads, head_dim, tq, chunk, max(sp, ss))
    attn_items = _attn_items(geom, tq, chunk)
    col_scale = jnp.asarray(np.where(np.arange(n_in) < n_heads * v_dim, head_dim ** -0.5 * LOG2E, 1.0)
                            .astype(np.float32))[None, :]
    lam_init = np.asarray([0.8 - 0.6 * math.exp(-0.3 * l) for l in range(depth)], np.float32)
    lam_pack = jnp.stack(
        [lam_q1, lam_k1, lam_q2, lam_k2, jnp.broadcast_to(jnp.asarray(lam_init)[:, None], lam_q1.shape)]
        + [jnp.zeros_like(lam_q1)] * 3, axis=1)

    per_layer = dict(mod=mod, g_mix_pre=g_mix_pre[:, None], g_mix_post=g_mix_post[:, None],
                     g_ffn_pre=g_ffn_pre[:, None], g_ffn_post=g_ffn_post[:, None], g_subln=g_subln[:, None],
                     w_in=w_in, lam=lam_pack, conv_w=conv_w, w_out=w_out)

    def mixer(x, p, w_router_l):
        z = _inproj(geom, x, p["g_mix_pre"], p["mod"], p["w_in"].astype(BF16), col_scale)
        attn = _attention(geom, z, attn_consts, attn_items, p["lam"], p["g_subln"], n_heads, head_dim, tq, chunk)
        return _outproj(geom, attn, z, x, p["conv_w"], p["w_out"].astype(BF16), p["g_mix_post"], p["g_ffn_pre"],
                        p["mod"], w_router_l)

    def dense_layer(x, p, w):
        x, h = mixer(x, p, None)
        return _ffn(geom, h, x, w["gate"].astype(BF16), w["up"].astype(BF16), w["down"].astype(BF16),
                    p["g_ffn_post"], p["mod"])

    def expert_layer(x, p, w):
        x, h, top_e, top_w = mixer(x, p, w["router"])
        dest, row_tok, block_e, n_used = _route(top_e, n_experts, MOE_BLOCK)
        x_rows = _gather_rows(h, row_tok)
        y_rows = _experts(x_rows, block_e, n_used, w["gate"].astype(BF16), w["up"].astype(BF16),
                          w["down"].astype(BF16), MOE_BLOCK)
        return _combine(geom, y_rows, dest, top_w, x, p["g_ffn_post"], p["mod"])

    n_pairs = depth // 2
    even = jax.tree.map(lambda a: a[0:2 * n_pairs:2], per_layer)
    odd = jax.tree.map(lambda a: a[1:2 * n_pairs:2], per_layer)
    dense_w = dict(gate=w_ffn_gate, up=w_ffn_up, down=w_ffn_down)
    exp_w = dict(router=w_router, gate=w_exp_gate, up=w_exp_up, down=w_exp_down)

    def pair(x, xs):
        p_even, p_odd, dw, ew = xs
        x = dense_layer(x, p_even, dw)
        x = expert_layer(x, p_odd, ew)
        return x, None

    x, _ = lax.scan(pair, x, (even, odd, jax.tree.map(lambda a: a[:n_pairs], dense_w), exp_w))
    if depth % 2 == 1:
        x = dense_layer(x, jax.tree.map(lambda a: a[depth - 1], per_layer),
                        jax.tree.map(lambda a: a[n_pairs], dense_w))

    y_prompt = x[:geom.tp].reshape(bp, sp, d)
    y_sample = x[geom.tp:].reshape(bs, ss, d)
    return (y_prompt, y_sample)
```

```python
import functools
import math

import numpy as np
import jax
import jax.numpy as jnp
from jax import lax
from jax.experimental import pallas as pl
from jax.experimental.pallas import tpu as pltpu

F32 = jnp.float32
BF16 = jnp.bfloat16

ALIBI_MAX = 8.0
NORM_EPS = 1e-6
TOP_K = 2
N_MOD = 6
LOG2E = math.log2(math.e)

VMEM_LIMIT_BYTES = 56 * 1024 * 1024
ROW_ALIGN = 16
LANES = 128
BF16_EXACT_INT = 256
SLOPE_PARTS = 3

INPROJ_ROWS, INPROJ_COLS = 1024, 2048
ATTN_Q_ROWS, ATTN_KV_CHUNK = 512, 2048
OUTPROJ_ROWS = 256
FFN_ROWS, FFN_COLS = 512, 1408
MOE_BLOCK, MOE_COLS = 1024, 896
GATHER_ROWS = 1024
COMBINE_ROWS = 512
SMEM_INDEX_ALIGN = 1024


def _tile(n, pref, align=1):
    t = min(n, pref)
    while t > 0:
        if n % t == 0 and t % align == 0:
            return t
        t -= 1
    raise ValueError(f"no tile for {n} (pref {pref}, align {align})")


def _params(sem):
    return pltpu.CompilerParams(dimension_semantics=sem, vmem_limit_bytes=VMEM_LIMIT_BYTES)


def _rms(x):
    return x * lax.rsqrt(jnp.mean(x * x, axis=-1, keepdims=True) + NORM_EPS)


def _row_spec(d, fn=None):
    return pl.BlockSpec((1, d), fn if fn is not None else (lambda *_: (0, 0)))


def _ada_kernel(c_ref, w_ref, b_ref, o_ref):
    c = c_ref[...]
    cond = c * jax.nn.sigmoid(c)
    o_ref[0] = jnp.dot(cond, w_ref[0], preferred_element_type=F32,
                       precision=lax.Precision.HIGHEST) + b_ref[0]


def _ada(c_all, w_ada, b_ada):
    depth, d, _ = w_ada.shape
    bp = c_all.shape[0]
    b3 = b_ada.reshape(depth * N_MOD, 1, d)
    out = pl.pallas_call(
        _ada_kernel,
        grid=(depth, N_MOD),
        in_specs=[
            pl.BlockSpec((bp, d), lambda l, k: (0, 0)),
            pl.BlockSpec((1, d, d), lambda l, k: (l, 0, k)),
            pl.BlockSpec((1, 1, d), lambda l, k: (l * N_MOD + k, 0, 0)),
        ],
        out_specs=pl.BlockSpec((1, bp, d), lambda l, k: (l * N_MOD + k, 0, 0)),
        out_shape=jax.ShapeDtypeStruct((depth * N_MOD, bp, d), F32),
        compiler_params=_params(("arbitrary", "arbitrary")),
        name="ada_mod",
    )(c_all, w_ada, b3)
    return out.reshape(depth, N_MOD * bp, 1, d)


class _Geom:
    def __init__(self, bp, sp, bs, ss, d):
        self.bp, self.sp, self.bs, self.ss, self.d = bp, sp, bs, ss, d
        self.tp = bp * sp
        self.t = self.tp + bs * ss
        self.nb = bp + bs
        self.nb_pad = -(-self.nb // 8) * 8
        self.row_gcd = math.gcd(sp, ss)

    def batch_of_tile(self, i, rows):
        npt = self.tp // rows
        return jnp.where(i < npt, i // (self.sp // rows), self.bp + (i - npt) // (self.ss // rows))

    def mod_spec(self, k, rows):
        return pl.BlockSpec((1, 1, self.d), lambda i, *_: (k * self.nb_pad + self.batch_of_tile(i, rows), 0, 0))


def _inproj_kernel(x_ref, g_ref, scale_ref, shift_ref, w_ref, cs_ref, z_ref, h_scr):
    @pl.when(pl.program_id(1) == 0)
    def _():
        h = _rms(x_ref[...]) * g_ref[...] * (1.0 + scale_ref[0]) + shift_ref[0]
        h_scr[...] = h.astype(BF16)

    z = jnp.dot(h_scr[...], w_ref[...], preferred_element_type=F32)
    z_ref[...] = (z * cs_ref[...]).astype(BF16)


def _inproj(geom, x, g_pre, mod_l, w_in_l, col_scale):
    t, d = x.shape
    n = w_in_l.shape[1]
    tm = _tile(geom.row_gcd, INPROJ_ROWS, ROW_ALIGN)
    tn = _tile(n, INPROJ_COLS, LANES)
    return pl.pallas_call(
        _inproj_kernel,
        grid=(t // tm, n // tn),
        in_specs=[
            pl.BlockSpec((tm, d), lambda i, j: (i, 0)),
            _row_spec(d),
            geom.mod_spec(1, tm),
            geom.mod_spec(0, tm),
            pl.BlockSpec((d, tn), lambda i, j: (0, j)),
            pl.BlockSpec((1, tn), lambda i, j: (0, j)),
        ],
        out_specs=pl.BlockSpec((tm, tn), lambda i, j: (i, j)),
        out_shape=jax.ShapeDtypeStruct((t, n), BF16),
        scratch_shapes=[pltpu.VMEM((tm, d), BF16)],
        compiler_params=_params(("arbitrary", "arbitrary")),
        name="in_proj",
    )(x, g_pre, mod_l, mod_l, w_in_l, col_scale)


def _attn_scores(s_ref, it, q_ref, k_ref, qc_ref, kc_ref, bd_ref, ed0_ref, d0_ref, dc_ref, head_dim):
    tq, v_dim = q_ref.shape
    n_sub = k_ref.shape[0] // tq
    q = q_ref[...]
    lane = lax.broadcasted_iota(jnp.int32, (tq, v_dim), 1)
    half = (lane < head_dim, lane >= head_dim)
    d0 = d0_ref[it]
    d0_v = jnp.full((1, v_dim), d0, jnp.int32).astype(F32)
    zero = jnp.zeros_like(q)
    for m in range(2):
        q_aug = jnp.where(half[m], q, zero) + qc_ref[0, m] + (d0_v * ed0_ref[m]).astype(BF16)
        for c in range(n_sub):
            delta = d0 - c * tq
            sgn = (delta > 0).astype(jnp.int32) - (delta < 0).astype(jnp.int32)
            sgn_v = jnp.full((1, v_dim), sgn, jnp.int32).astype(F32).astype(BF16)
            k_aug = jnp.where(half[m], k_ref[c * tq:(c + 1) * tq, :], sgn_v * kc_ref[0, m, c * tq:(c + 1) * tq, :])
            s_ref[m, :, c * tq:(c + 1) * tq] = lax.dot_general(
                q_aug, k_aug, (((1,), (1,)), ((), ())), preferred_element_type=F32)
    dc = dc_ref[it]
    flag = jnp.full((1, 1), (dc >= 0).astype(jnp.int32), jnp.int32).astype(F32)
    off = pl.multiple_of(jnp.maximum(dc, 0) * tq, tq)
    bias = flag * bd_ref[0]
    for m in range(2):
        s_ref[m, :, pl.ds(off, tq)] = s_ref[m, :, pl.ds(off, tq)] - bias


def _attn_softmax_pv(s_ref, is_first, v_ref, m_scr, acc_scr):
    chunk, v_dim = v_ref.shape
    ones_col = jnp.where(lax.broadcasted_iota(jnp.int32, (chunk, v_dim), 1) == 0, 1.0, 0.0).astype(BF16)
    v_aug = jnp.concatenate([v_ref[...], ones_col], axis=1)
    for m in range(2):
        s = s_ref[m]
        m_old = jnp.where(is_first, -jnp.inf, m_scr[m])
        acc_old = jnp.where(is_first, 0.0, acc_scr[m])
        m_new = jnp.maximum(m_old, jnp.max(s, axis=-1, keepdims=True))
        p = jnp.exp2(s - m_new).astype(BF16)
        alpha = jnp.exp2(m_old - m_new)
        acc_scr[m] = alpha * acc_old + jnp.dot(p, v_aug, preferred_element_type=F32)
        m_scr[m] = m_new


def _attn_kernel(qblk_ref, kvblk_ref, first_ref, last_ref, d0_ref, dc_ref,
                 q_ref, k_ref, v_ref, qc_ref, kc_ref, bd_ref, ed0_ref, lam_ref, gsub_ref, o_ref,
                 s_even, s_odd, m_scr, acc_scr, *, head_dim, n_items):
    g = pl.program_id(0)
    n_entries = pl.num_programs(0) - 1
    it_a = jnp.minimum(g, n_entries - 1) % n_items
    it_b = jnp.maximum(g - 1, 0) % n_items
    v_dim = q_ref.shape[1]

    @pl.when(g == 0)
    def _():
        s_odd[...] = jnp.zeros(s_odd.shape, F32)
        m_scr[...] = jnp.zeros(m_scr.shape, F32)
        acc_scr[...] = jnp.zeros(acc_scr.shape, F32)

    is_first = jnp.full((1, 1), first_ref[it_b], jnp.int32) > 0

    def step(s_write, s_read):
        _attn_scores(s_write, it_a, q_ref, k_ref, qc_ref, kc_ref, bd_ref, ed0_ref, d0_ref, dc_ref, head_dim)
        _attn_softmax_pv(s_read, is_first, v_ref, m_scr, acc_scr)

    @pl.when(g % 2 == 0)
    def _():
        step(s_even, s_odd)

    @pl.when(g % 2 == 1)
    def _():
        step(s_odd, s_even)

    @pl.when(last_ref[it_b] == 1)
    def _():
        lp = lam_ref[...]
        lam_init = lp[4:5, :1]
        lam = (jnp.exp(jnp.sum(lp[0:1] * lp[1:2], axis=-1, keepdims=True))
               - jnp.exp(jnp.sum(lp[2:3] * lp[3:4], axis=-1, keepdims=True)) + lam_init)
        a0 = acc_scr[0]
        a1 = acc_scr[1]
        o = a0[:, :v_dim] / a0[:, v_dim:v_dim + 1] - lam * (a1[:, :v_dim] / a1[:, v_dim:v_dim + 1])
        o_ref[...] = (_rms(o) * gsub_ref[...] * (1.0 - lam_init)).astype(o_ref.dtype)


def _attn_items(geom, tq, chunk):
    qblk, kvblk, first, last, d0s, dcs = [], [], [], [], [], []
    for off, nb, s in ((0, geom.bp, geom.sp), (geom.tp, geom.bs, geom.ss)):
        for b in range(nb):
            for qt in range(s // tq):
                for c in range(s // chunk):
                    qblk.append((off + b * s + qt * tq) // tq)
                    kvblk.append((off + b * s + c * chunk) // chunk)
                    first.append(int(c == 0))
                    last.append(int(c == s // chunk - 1))
                    d0 = qt * tq - c * chunk
                    d0s.append(d0)
                    dcs.append(d0 // tq if 0 <= d0 < chunk else -1)
    return [jnp.asarray(np.asarray(a, np.int32)) for a in (qblk, kvblk, first, last, d0s, dcs)]


def _bf16_parts(x, n):
    parts, rest = [], np.asarray(x, np.float64)
    for _ in range(n):
        p = rest.astype(BF16).astype(np.float64)
        parts.append(p)
        rest = rest - p
    return parts


def _attn_consts(n_heads, head_dim, tq, chunk, max_d0):
    v_dim = 2 * head_dim
    slopes = 2.0 ** (-ALIBI_MAX * np.arange(1, n_heads + 1, dtype=np.float64) / n_heads) * LOG2E
    sl = _bf16_parts(slopes, SLOPE_PARTS)
    r = np.arange(tq)
    j = np.arange(chunk)
    i_parts = [None, (r // BF16_EXACT_INT) * BF16_EXACT_INT, r % BF16_EXACT_INT]
    j_parts = [(j // BF16_EXACT_INT) * BF16_EXACT_INT, j % BF16_EXACT_INT]
    assert chunk <= BF16_EXACT_INT ** 2 and max_d0 % BF16_EXACT_INT == 0 and max_d0 <= BF16_EXACT_INT ** 2
    n_cols = SLOPE_PARTS * (len(i_parts) + len(j_parts))
    assert n_cols <= head_dim
    qc = np.zeros((n_heads, 2, tq, v_dim), np.float64)
    kc = np.zeros((n_heads, 2, chunk, v_dim), np.float64)
    ed0 = np.zeros((2, 1, v_dim), np.float32)
    for m in range(2):
        base = head_dim if m == 0 else 0
        col = base
        for a in range(SLOPE_PARTS):
            for jp in j_parts:
                qc[:, m, :, col] = sl[a][:, None]
                kc[:, m, :, col] = jp[None, :]
                col += 1
            for ip in i_parts:
                if ip is None:
                    ed0[m, 0, col] = 1.0
                else:
                    qc[:, m, :, col] = ip[None, :]
                kc[:, m, :, col] = -sl[a][:, None]
                col += 1
    bd = slopes[:, None, None] * np.abs(r[:, None] - r[None, :])[None]
    for arr in (qc, kc):
        assert np.array_equal(arr.astype(BF16).astype(np.float64), arr)
    return (jnp.asarray(qc.astype(BF16)), jnp.asarray(kc.astype(BF16)),
            jnp.asarray(bd.astype(np.float32)), jnp.asarray(ed0))


def _attention(geom, z, consts, items, lam_pack_l, g_subln_l, n_heads, head_dim, tq, chunk):
    t = z.shape[0]
    v_dim = 2 * head_dim
    qc, kc, bd, ed0 = consts
    n_items = items[0].shape[0]
    k_col0 = n_heads
    v_col0 = 2 * n_heads
    n_entries = n_heads * n_items

    def score_entry(g):
        e = jnp.minimum(g, n_entries - 1)
        return e // n_items, e % n_items

    def finish_entry(g):
        e = jnp.maximum(g - 1, 0)
        return e // n_items, e % n_items

    def q_map(g, qb, kb, *_):
        h, it = score_entry(g)
        return qb[it], h

    def k_map(g, qb, kb, *_):
        h, it = score_entry(g)
        return kb[it], k_col0 + h

    def v_map(g, qb, kb, *_):
        h, it = finish_entry(g)
        return kb[it], v_col0 + h

    def o_map(g, qb, kb, *_):
        h, it = finish_entry(g)
        return qb[it], h

    grid_spec = pltpu.PrefetchScalarGridSpec(
        num_scalar_prefetch=6,
        grid=(n_entries + 1,),
        in_specs=[
            pl.BlockSpec((tq, v_dim), q_map),
            pl.BlockSpec((chunk, v_dim), k_map),
            pl.BlockSpec((chunk, v_dim), v_map),
            pl.BlockSpec((1, 2, tq, v_dim), lambda g, *_: (score_entry(g)[0], 0, 0, 0)),
            pl.BlockSpec((1, 2, chunk, v_dim), lambda g, *_: (score_entry(g)[0], 0, 0, 0)),
            pl.BlockSpec((1, tq, tq), lambda g, *_: (score_entry(g)[0], 0, 0)),
            pl.BlockSpec((2, 1, v_dim), lambda g, *_: (0, 0, 0)),
            pl.BlockSpec((8, head_dim), lambda g, *_: (0, 0)),
            _row_spec(v_dim),
        ],
        out_specs=pl.BlockSpec((tq, v_dim), o_map),
        scratch_shapes=[
            pltpu.VMEM((2, tq, chunk), F32),
            pltpu.VMEM((2, tq, chunk), F32),
            pltpu.VMEM((2, tq, 1), F32),
            pltpu.VMEM((2, tq, 2 * v_dim), F32),
        ],
    )
    return pl.pallas_call(
        functools.partial(_attn_kernel, head_dim=head_dim, n_items=n_items),
        grid_spec=grid_spec,
        out_shape=jax.ShapeDtypeStruct((t, n_heads * v_dim), BF16),
        compiler_params=_params(("arbitrary",)),
        name="diff_attn",
    )(*items, z, z, z, qc, kc, bd, ed0, lam_pack_l, g_subln_l)


def _outproj_kernel(attn_ref, cb_ref, cc_ref, cx_ref, ga_ref, gc_ref,
                    ccp_ref, cxp_ref, ccn_ref, cxn_ref,
                    x_ref, convw_ref, wout_ref, gpost_ref, gatem_ref, gpre_ref, scalef_ref, shiftf_ref,
                    *rest, geom, moe):
    if moe:
        wr_ref, xo_ref, h_ref, tope_ref, topw_ref = rest
    else:
        xo_ref, h_ref = rest
    tm = x_ref.shape[0]
    row0 = pl.program_id(0) * tm
    seq = jnp.where(row0 < geom.tp, geom.sp, geom.ss)
    rel0 = jnp.where(row0 < geom.tp, row0, row0 - geom.tp)
    not_start = (rel0 % seq != 0).astype(F32)
    not_end = ((rel0 + tm) % seq != 0).astype(F32)

    u = cc_ref[...].astype(F32) * cx_ref[...].astype(F32)
    halo = ccp_ref.shape[0]
    u_prev = (ccp_ref[halo - 1:halo, :].astype(F32) * cxp_ref[halo - 1:halo, :].astype(F32)) * not_start
    u_next = (ccn_ref[0:1, :].astype(F32) * cxn_ref[0:1, :].astype(F32)) * not_end
    row = lax.broadcasted_iota(jnp.int32, u.shape, 0)
    u_m1 = jnp.where(row == 0, u_prev, pltpu.roll(u, 1, 0))
    u_p1 = jnp.where(row == tm - 1, u_next, pltpu.roll(u, tm - 1, 0))
    cw = convw_ref[...]
    conv = u_m1 * cw[0:1] + u * cw[1:2] + u_p1 * cw[2:3]
    short = cb_ref[...].astype(F32) * conv
    merged = (jax.nn.sigmoid(ga_ref[...].astype(F32)) * attn_ref[...].astype(F32)
              + jax.nn.sigmoid(gc_ref[...].astype(F32)) * short)
    o = jnp.dot(merged.astype(BF16), wout_ref[...], preferred_element_type=F32)
    x_new = x_ref[...] + gatem_ref[0] * (_rms(o) * gpost_ref[...])
    xo_ref[...] = x_new
    h = _rms(x_new) * gpre_ref[...] * (1.0 + scalef_ref[0]) + shiftf_ref[0]
    h_ref[...] = h.astype(h_ref.dtype)

    if moe:
        n_e = wr_ref.shape[1]
        logits = jnp.dot(h, wr_ref[...], preferred_element_type=F32, precision=lax.Precision.HIGHEST)
        lane = lax.broadcasted_iota(jnp.int32, logits.shape, 1).astype(F32)
        m1 = jnp.max(logits, axis=-1, keepdims=True)
        i1 = jnp.min(jnp.where(logits == m1, lane, float(n_e)), axis=-1, keepdims=True)
        rest_l = jnp.where(lane == i1, -jnp.inf, logits)
        m2 = jnp.max(rest_l, axis=-1, keepdims=True)
        i2 = jnp.min(jnp.where(rest_l == m2, lane, float(n_e)), axis=-1, keepdims=True)
        e = jnp.exp(m2 - m1)
        w1 = 1.0 / (1.0 + e)
        w2 = e / (1.0 + e)
        tope_ref[...] = jnp.where(lane == 0.0, i1, jnp.where(lane == 1.0, i2, 0.0)).astype(jnp.int32)
        topw_ref[...] = jnp.where(lane == 0.0, w1, jnp.where(lane == 1.0, w2, 0.0))


def _outproj(geom, attn, z, x, conv_w_l, w_out_l, g_post, g_pre, mod_l, w_router_l):
    t, d = x.shape
    moe = w_router_l is not None
    tm = _tile(geom.row_gcd, OUTPROJ_ROWS, ROW_ALIGN)
    halo = ROW_ALIGN
    nh = tm // halo
    last_halo = t // halo - 1
    zcol = lambda c: pl.BlockSpec((tm, d), lambda i: (i, c))
    in_specs = [
        pl.BlockSpec((tm, d), lambda i: (i, 0)),
        zcol(3), zcol(4), zcol(5), zcol(6), zcol(7),
        pl.BlockSpec((halo, d), lambda i: (jnp.maximum(i * nh - 1, 0), 4)),
        pl.BlockSpec((halo, d), lambda i: (jnp.maximum(i * nh - 1, 0), 5)),
        pl.BlockSpec((halo, d), lambda i: (jnp.minimum((i + 1) * nh, last_halo), 4)),
        pl.BlockSpec((halo, d), lambda i: (jnp.minimum((i + 1) * nh, last_halo), 5)),
        pl.BlockSpec((tm, d), lambda i: (i, 0)),
        pl.BlockSpec(conv_w_l.shape, lambda i: (0, 0)),
        pl.BlockSpec((d, d), lambda i: (0, 0)),
        _row_spec(d),
        geom.mod_spec(2, tm),
        _row_spec(d),
        geom.mod_spec(4, tm), geom.mod_spec(3, tm),
    ]
    args = [attn, z, z, z, z, z, z, z, z, z, x, conv_w_l, w_out_l, g_post, mod_l, g_pre, mod_l, mod_l]
    out_specs = [pl.BlockSpec((tm, d), lambda i: (i, 0)), pl.BlockSpec((tm, d), lambda i: (i, 0))]
    out_shape = [jax.ShapeDtypeStruct((t, d), F32), jax.ShapeDtypeStruct((t, d), F32 if moe else BF16)]
    if moe:
        n_e = w_router_l.shape[1]
        in_specs.append(pl.BlockSpec((d, n_e), lambda i: (0, 0)))
        args.append(w_router_l)
        out_specs += [pl.BlockSpec((tm, n_e), lambda i: (i, 0))] * 2
        out_shape += [jax.ShapeDtypeStruct((t, n_e), jnp.int32), jax.ShapeDtypeStruct((t, n_e), F32)]
    return pl.pallas_call(
        functools.partial(_outproj_kernel, geom=geom, moe=moe),
        grid=(t // tm,),
        in_specs=in_specs,
        out_specs=out_specs,
        out_shape=out_shape,
        compiler_params=_params(("arbitrary",)),
        name="out_proj_moe" if moe else "out_proj",
    )(*args)


def _swiglu_partial(h, wg, wu, wd):
    g = jnp.dot(h, wg, preferred_element_type=F32)
    u = jnp.dot(h, wu, preferred_element_type=F32)
    a = (g * jax.nn.sigmoid(g)) * u
    return jnp.dot(a.astype(BF16), wd, preferred_element_type=F32)


def _ffn_kernel(h_ref, x_ref, wg_ref, wu_ref, wd_ref, gpost_ref, gate_ref, xo_ref, acc_scr):
    f = pl.program_id(1)
    part = _swiglu_partial(h_ref[...], wg_ref[...], wu_ref[...], wd_ref[...])

    @pl.when(f == 0)
    def _():
        acc_scr[...] = part

    @pl.when(f > 0)
    def _():
        acc_scr[...] += part

    @pl.when(f == pl.num_programs(1) - 1)
    def _():
        xo_ref[...] = x_ref[...] + gate_ref[0] * (_rms(acc_scr[...]) * gpost_ref[...])


def _ffn(geom, h, x, wg, wu, wd, g_post, mod_l):
    t, d = x.shape
    ff = wg.shape[1]
    tm = _tile(geom.row_gcd, FFN_ROWS, ROW_ALIGN)
    tf = _tile(ff, FFN_COLS, LANES)
    return pl.pallas_call(
        _ffn_kernel,
        grid=(t // tm, ff // tf),
        in_specs=[
            pl.BlockSpec((tm, d), lambda i, f: (i, 0)),
            pl.BlockSpec((tm, d), lambda i, f: (i, 0)),
            pl.BlockSpec((d, tf), lambda i, f: (0, f)),
            pl.BlockSpec((d, tf), lambda i, f: (0, f)),
            pl.BlockSpec((tf, d), lambda i, f: (f, 0)),
            _row_spec(d),
            geom.mod_spec(5, tm),
        ],
        out_specs=pl.BlockSpec((tm, d), lambda i, f: (i, 0)),
        out_shape=jax.ShapeDtypeStruct((t, d), F32),
        scratch_shapes=[pltpu.VMEM((tm, d), F32)],
        compiler_params=_params(("arbitrary", "arbitrary")),
        name="dense_ffn",
    )(h, x, wg, wu, wd, g_post, mod_l)


def _gather_kernel(idx_ref, src_ref, o_ref, buf, sem):
    rows = buf.shape[0]

    def issue(r, carry):
        pltpu.make_async_copy(src_ref.at[pl.ds(idx_ref[r], 1)], buf.at[pl.ds(r, 1)], sem).start()
        return carry

    lax.fori_loop(0, rows, issue, 0)
    pltpu.make_async_copy(src_ref.at[pl.ds(0, rows)], buf, sem).wait()
    o_ref[...] = buf[...].astype(o_ref.dtype)


def _gather_rows(src, row_tok):
    n_pad = row_tok.shape[0]
    d = src.shape[1]
    rows = _tile(n_pad, GATHER_ROWS, SMEM_INDEX_ALIGN)
    return pl.pallas_call(
        _gather_kernel,
        grid=(n_pad // rows,),
        in_specs=[
            pl.BlockSpec((rows,), lambda i: (i,), memory_space=pltpu.SMEM),
            pl.BlockSpec(memory_space=pl.ANY),
        ],
        out_specs=pl.BlockSpec((rows, d), lambda i: (i, 0)),
        out_shape=jax.ShapeDtypeStruct((n_pad, d), BF16),
        scratch_shapes=[pltpu.VMEM((rows, d), src.dtype), pltpu.SemaphoreType.DMA(())],
        compiler_params=_params(("arbitrary",)),
        name="moe_gather",
    )(row_tok, src)


def _expert_kernel(be_ref, nused_ref, x_ref, wg_ref, wu_ref, wd_ref, y_ref, acc_scr):
    b = pl.program_id(0)
    f = pl.program_id(1)
    last = pl.num_programs(1) - 1

    @pl.when(b < nused_ref[0])
    def _():
        part = _swiglu_partial(x_ref[...], wg_ref[0], wu_ref[0], wd_ref[0])

        @pl.when(f == 0)
        def _():
            acc_scr[...] = part

        @pl.when(f > 0)
        def _():
            acc_scr[...] += part

        @pl.when(f == last)
        def _():
            y_ref[...] = acc_scr[...]

    @pl.when(jnp.logical_and(b >= nused_ref[0], f == last))
    def _():
        y_ref[...] = jnp.zeros(y_ref.shape, y_ref.dtype)


def _experts(x_rows, block_e, n_used, wg, wu, wd, blk):
    n_pad, d = x_rows.shape
    ff = wg.shape[2]
    tf = _tile(ff, MOE_COLS, LANES)
    grid_spec = pltpu.PrefetchScalarGridSpec(
        num_scalar_prefetch=2,
        grid=(n_pad // blk, ff // tf),
        in_specs=[
            pl.BlockSpec((blk, d), lambda b, f, be, nu: (b, 0)),
            pl.BlockSpec((1, d, tf), lambda b, f, be, nu: (be[b], 0, f)),
            pl.BlockSpec((1, d, tf), lambda b, f, be, nu: (be[b], 0, f)),
            pl.BlockSpec((1, tf, d), lambda b, f, be, nu: (be[b], f, 0)),
        ],
        out_specs=pl.BlockSpec((blk, d), lambda b, f, be, nu: (b, 0)),
        scratch_shapes=[pltpu.VMEM((blk, d), F32)],
    )
    return pl.pallas_call(
        _expert_kernel,
        grid_spec=grid_spec,
        out_shape=jax.ShapeDtypeStruct((n_pad, d), F32),
        compiler_params=_params(("arbitrary", "arbitrary")),
        name="expert_ffn",
    )(block_e, n_used, x_rows, wg, wu, wd)


def _combine_kernel(dest_ref, y_ref, w_ref, x_ref, gpost_ref, gate_ref, xo_ref, buf, sem):
    tc = x_ref.shape[0]

    def issue(r, carry):
        for k in range(TOP_K):
            pltpu.make_async_copy(y_ref.at[pl.ds(dest_ref[TOP_K * r + k], 1)], buf.at[k, pl.ds(r, 1)], sem).start()
        return carry

    lax.fori_loop(0, tc, issue, 0)
    for k in range(TOP_K):
        pltpu.make_async_copy(y_ref.at[pl.ds(0, tc)], buf.at[k], sem).wait()
    w = w_ref[...]
    o = w[:, 0:1] * buf[0]
    for k in range(1, TOP_K):
        o = o + w[:, k:k + 1] * buf[k]
    xo_ref[...] = x_ref[...] + gate_ref[0] * (_rms(o) * gpost_ref[...])


def _combine(geom, y_rows, dest, top_w, x, g_post, mod_l):
    t, d = x.shape
    n_e = top_w.shape[1]
    tc = _tile(geom.row_gcd, COMBINE_ROWS, SMEM_INDEX_ALIGN // TOP_K)
    return pl.pallas_call(
        _combine_kernel,
        grid=(t // tc,),
        in_specs=[
            pl.BlockSpec((TOP_K * tc,), lambda i: (i,), memory_space=pltpu.SMEM),
            pl.BlockSpec(memory_space=pl.ANY),
            pl.BlockSpec((tc, n_e), lambda i: (i, 0)),
            pl.BlockSpec((tc, d), lambda i: (i, 0)),
            _row_spec(d),
            geom.mod_spec(5, tc),
        ],
        out_specs=pl.BlockSpec((tc, d), lambda i: (i, 0)),
        out_shape=jax.ShapeDtypeStruct((t, d), F32),
        scratch_shapes=[pltpu.VMEM((TOP_K, tc, d), F32), pltpu.SemaphoreType.DMA(())],
        compiler_params=_params(("arbitrary",)),
        name="moe_combine",
    )(dest, y_rows, top_w, x, g_post, mod_l)


def _route(top_e, n_experts, blk):
    t = top_e.shape[0]
    n_assign = t * TOP_K
    n_blocks = -(-(n_assign + n_experts * (blk - 1)) // blk)
    n_pad = n_blocks * blk
    flat_e = top_e[:, :TOP_K].reshape(-1)
    onehot = (flat_e[:, None] == jnp.arange(n_experts, dtype=jnp.int32)[None, :]).astype(jnp.int32)
    rank = jnp.sum((jnp.cumsum(onehot, axis=0) - onehot) * onehot, axis=1)
    counts = jnp.sum(onehot, axis=0)
    padded = (counts + blk - 1) // blk * blk
    end_pad = jnp.cumsum(padded)
    start_pad = end_pad - padded
    dest = (start_pad[flat_e] + rank).astype(jnp.int32)
    flat_tok = jnp.arange(n_assign, dtype=jnp.int32) // TOP_K
    row_tok = jnp.zeros((n_pad,), jnp.int32).at[dest].set(flat_tok)
    block_e = jnp.minimum(
        jnp.searchsorted(end_pad, jnp.arange(n_blocks, dtype=jnp.int32) * blk, side='right'),
        n_experts - 1).astype(jnp.int32)
    n_used = (end_pad[-1:] // blk).astype(jnp.int32)
    return dest, row_tok, block_e, n_used


def kernel(x_prompt, x_sample, c_prompt, c_sample, w_ada, b_ada, g_mix_pre, g_mix_post, g_ffn_pre, g_ffn_post,
           w_in, lam_q1, lam_k1, lam_q2, lam_k2, g_subln, conv_w, w_out, w_ffn_gate, w_ffn_up, w_ffn_down,
           w_router, w_exp_gate, w_exp_up, w_exp_down):
    bp, sp, d = x_prompt.shape
    bs, ss, _ = x_sample.shape
    depth = w_in.shape[0]
    head_dim = lam_q1.shape[1]
    v_dim = g_subln.shape[1]
    n_heads = d // v_dim
    n_experts = w_router.shape[2]
    n_in = w_in.shape[2]
    assert v_dim == 2 * head_dim and n_in == 8 * d
    geom = _Geom(bp, sp, bs, ss, d)

    x = jnp.concatenate([x_prompt.reshape(bp * sp, d), x_sample.reshape(bs * ss, d)], axis=0)
    c_all = jnp.concatenate([c_prompt, c_sample, jnp.zeros((geom.nb_pad - geom.nb, d), F32)], axis=0)
    mod = _ada(c_all, w_ada, b_ada)

    tq = _tile(geom.row_gcd, ATTN_Q_ROWS, ROW_ALIGN)
    chunk = _tile(geom.row_gcd, ATTN_KV_CHUNK, tq)
    attn_consts = _attn_consts(n_heads, head_dim, tq, chunk, max(sp, ss))
    attn_items = _attn_items(geom, tq, chunk)
    col_scale = jnp.asarray(np.where(np.arange(n_in) < n_heads * v_dim, head_dim ** -0.5 * LOG2E, 1.0)
                            .astype(np.float32))[None, :]
    lam_init = np.asarray([0.8 - 0.6 * math.exp(-0.3 * l) for l in range(depth)], np.float32)
    lam_pack = jnp.stack(
        [lam_q1, lam_k1, lam_q2, lam_k2, jnp.broadcast_to(jnp.asarray(lam_init)[:, None], lam_q1.shape)]
        + [jnp.zeros_like(lam_q1)] * 3, axis=1)

    per_layer = dict(mod=mod, g_mix_pre=g_mix_pre[:, None], g_mix_post=g_mix_post[:, None],
                     g_ffn_pre=g_ffn_pre[:, None], g_ffn_post=g_ffn_post[:, None], g_subln=g_subln[:, None],
                     w_in=w_in, lam=lam_pack, conv_w=conv_w, w_out=w_out)

    def mixer(x, p, w_router_l):
        z = _inproj(geom, x, p["g_mix_pre"], p["mod"], p["w_in"].astype(BF16), col_scale)
        attn = _attention(geom, z, attn_consts, attn_items, p["lam"], p["g_subln"], n_heads, head_dim, tq, chunk)
        return _outproj(geom, attn, z, x, p["conv_w"], p["w_out"].astype(BF16), p["g_mix_post"], p["g_ffn_pre"],
                        p["mod"], w_router_l)

    def dense_layer(x, p, w):
        x, h = mixer(x, p, None)
        return _ffn(geom, h, x, w["gate"].astype(BF16), w["up"].astype(BF16), w["down"].astype(BF16),
                    p["g_ffn_post"], p["mod"])

    def expert_layer(x, p, w):
        x, h, top_e, top_w = mixer(x, p, w["router"])
        dest, row_tok, block_e, n_used = _route(top_e, n_experts, MOE_BLOCK)
        x_rows = _gather_rows(h, row_tok)
        y_rows = _experts(x_rows, block_e, n_used, w["gate"].astype(BF16), w["up"].astype(BF16),
                          w["down"].astype(BF16), MOE_BLOCK)
        return _combine(geom, y_rows, dest, top_w, x, p["g_ffn_post"], p["mod"])

    n_pairs = depth // 2
    even = jax.tree.map(lambda a: a[0:2 * n_pairs:2], per_layer)
    odd = jax.tree.map(lambda a: a[1:2 * n_pairs:2], per_layer)
    dense_w = dict(gate=w_ffn_gate, up=w_ffn_up, down=w_ffn_down)
    exp_w = dict(router=w_router, gate=w_exp_gate, up=w_exp_up, down=w_exp_down)

    def pair(x, xs):
        p_even, p_odd, dw, ew = xs
        x = dense_layer(x, p_even, dw)
        x = expert_layer(x, p_odd, ew)
        return x, None

    x, _ = lax.scan(pair, x, (even, odd, jax.tree.map(lambda a: a[:n_pairs], dense_w), exp_w))
    if depth % 2 == 1:
        x = dense_layer(x, jax.tree.map(lambda a: a[depth - 1], per_layer),
                        jax.tree.map(lambda a: a[n_pairs], dense_w))

    y_prompt = x[:geom.tp].reshape(bp, sp, d)
    y_sample = x[geom.tp:].reshape(bs, ss, d)
    return (y_prompt, y_sample)
```

```python
import functools
import math

import numpy as np
import jax
import jax.numpy as jnp
from jax import lax
from jax.experimental import pallas as pl
from jax.experimental.pallas import tpu as pltpu

F32 = jnp.float32
BF16 = jnp.bfloat16

ALIBI_MAX = 8.0
NORM_EPS = 1e-6
TOP_K = 2
N_MOD = 6
LOG2E = math.log2(math.e)

VMEM_LIMIT_BYTES = 56 * 1024 * 1024
ROW_ALIGN = 16
SUBLANES = 8
LANES = 128
BF16_EXACT_INT = 256
SLOPE_PARTS = 3

INPROJ_ROWS, INPROJ_COLS = 1024, 2048
ATTN_Q_ROWS, ATTN_KV_CHUNK = 512, 2048
OUTPROJ_ROWS = 256
FFN_ROWS, FFN_COLS = 512, 1408
MOE_BLOCK = 1024
EXPERT_F_STEPS = 4
COMBINE_ROWS = 512
DMA_ISSUE_UNROLL = 8
SMEM_INDEX_ALIGN = 1024


def _tile(n, pref, align=1):
    t = min(n, pref)
    while t > 0:
        if n % t == 0 and t % align == 0:
            return t
        t -= 1
    raise ValueError(f"no tile for {n} (pref {pref}, align {align})")


def _params(sem):
    return pltpu.CompilerParams(dimension_semantics=sem, vmem_limit_bytes=VMEM_LIMIT_BYTES)


def _rms(x):
    return x * lax.rsqrt(jnp.mean(x * x, axis=-1, keepdims=True) + NORM_EPS)


def _row_spec(d, fn=None):
    return pl.BlockSpec((1, d), fn if fn is not None else (lambda *_: (0, 0)))


def _ada_kernel(c_ref, w_ref, b_ref, o_ref):
    c = c_ref[...]
    cond = c * jax.nn.sigmoid(c)
    o_ref[0] = jnp.dot(cond, w_ref[0], preferred_element_type=F32,
                       precision=lax.Precision.HIGHEST) + b_ref[0]


def _ada(c_all, w_ada, b_ada):
    depth, d, _ = w_ada.shape
    bp = c_all.shape[0]
    b3 = b_ada.reshape(depth * N_MOD, 1, d)
    out = pl.pallas_call(
        _ada_kernel,
        grid=(depth, N_MOD),
        in_specs=[
            pl.BlockSpec((bp, d), lambda l, k: (0, 0)),
            pl.BlockSpec((1, d, d), lambda l, k: (l, 0, k)),
            pl.BlockSpec((1, 1, d), lambda l, k: (l * N_MOD + k, 0, 0)),
        ],
        out_specs=pl.BlockSpec((1, bp, d), lambda l, k: (l * N_MOD + k, 0, 0)),
        out_shape=jax.ShapeDtypeStruct((depth * N_MOD, bp, d), F32),
        compiler_params=_params(("arbitrary", "arbitrary")),
        name="ada_mod",
    )(c_all, w_ada, b3)
    return out.reshape(depth, N_MOD * bp, 1, d)


class _Geom:
    def __init__(self, bp, sp, bs, ss, d):
        self.bp, self.sp, self.bs, self.ss, self.d = bp, sp, bs, ss, d
        self.tp = bp * sp
        self.t = self.tp + bs * ss
        self.nb = bp + bs
        self.nb_pad = -(-self.nb // 8) * 8
        self.row_gcd = math.gcd(sp, ss)

    def batch_of_tile(self, i, rows):
        npt = self.tp // rows
        return jnp.where(i < npt, i // (self.sp // rows), self.bp + (i - npt) // (self.ss // rows))

    def mod_spec(self, k, rows):
        return pl.BlockSpec((1, 1, self.d), lambda i, *_: (k * self.nb_pad + self.batch_of_tile(i, rows), 0, 0))


def _inproj_kernel(x_ref, g_ref, scale_ref, shift_ref, w_ref, cs_ref, z_ref, h_scr):
    @pl.when(pl.program_id(1) == 0)
    def _():
        h = _rms(x_ref[...]) * g_ref[...] * (1.0 + scale_ref[0]) + shift_ref[0]
        h_scr[...] = h.astype(BF16)

    z = jnp.dot(h_scr[...], w_ref[...], preferred_element_type=F32)
    z_ref[...] = (z * cs_ref[...]).astype(BF16)


def _inproj(geom, x, g_pre, mod_l, w_in_l, col_scale):
    t, d = x.shape
    n = w_in_l.shape[1]
    tm = _tile(geom.row_gcd, INPROJ_ROWS, ROW_ALIGN)
    tn = _tile(n, INPROJ_COLS, LANES)
    return pl.pallas_call(
        _inproj_kernel,
        grid=(t // tm, n // tn),
        in_specs=[
            pl.BlockSpec((tm, d), lambda i, j: (i, 0)),
            _row_spec(d),
            geom.mod_spec(1, tm),
            geom.mod_spec(0, tm),
            pl.BlockSpec((d, tn), lambda i, j: (0, j)),
            pl.BlockSpec((1, tn), lambda i, j: (0, j)),
        ],
        out_specs=pl.BlockSpec((tm, tn), lambda i, j: (i, j)),
        out_shape=jax.ShapeDtypeStruct((t, n), BF16),
        scratch_shapes=[pltpu.VMEM((tm, d), BF16)],
        compiler_params=_params(("arbitrary", "arbitrary")),
        name="in_proj",
    )(x, g_pre, mod_l, mod_l, w_in_l, col_scale)


def _attn_scores(s_ref, it, q_ref, k_ref, qc_ref, kc_ref, bd_ref, ed0_ref, d0_ref, dc_ref, head_dim):
    tq, v_dim = q_ref.shape
    n_sub = k_ref.shape[0] // tq
    q = q_ref[...]
    lane = lax.broadcasted_iota(jnp.int32, (tq, v_dim), 1)
    half = (lane < head_dim, lane >= head_dim)
    d0 = d0_ref[it]
    d0_v = jnp.full((1, v_dim), d0, jnp.int32).astype(F32)
    zero = jnp.zeros_like(q)
    for m in range(2):
        q_aug = jnp.where(half[m], q, zero) + qc_ref[0, m] + (d0_v * ed0_ref[m]).astype(BF16)
        for c in range(n_sub):
            delta = d0 - c * tq
            sgn = (delta > 0).astype(jnp.int32) - (delta < 0).astype(jnp.int32)
            sgn_v = jnp.full((1, v_dim), sgn, jnp.int32).astype(F32).astype(BF16)
            k_aug = jnp.where(half[m], k_ref[c * tq:(c + 1) * tq, :], sgn_v * kc_ref[0, m, c * tq:(c + 1) * tq, :])
            s_ref[m, :, c * tq:(c + 1) * tq] = lax.dot_general(
                q_aug, k_aug, (((1,), (1,)), ((), ())), preferred_element_type=F32)
    dc = dc_ref[it]
    flag = jnp.full((1, 1), (dc >= 0).astype(jnp.int32), jnp.int32).astype(F32)
    off = pl.multiple_of(jnp.maximum(dc, 0) * tq, tq)
    bias = flag * bd_ref[0]
    for m in range(2):
        s_ref[m, :, pl.ds(off, tq)] = s_ref[m, :, pl.ds(off, tq)] - bias


def _attn_softmax_pv(s_ref, is_first, v_ref, m_scr, acc_scr):
    chunk, v_dim = v_ref.shape
    ones_col = jnp.where(lax.broadcasted_iota(jnp.int32, (chunk, v_dim), 1) == 0, 1.0, 0.0).astype(BF16)
    v_aug = jnp.concatenate([v_ref[...], ones_col], axis=1)
    for m in range(2):
        s = s_ref[m]
        m_old = jnp.where(is_first, -jnp.inf, m_scr[m])
        acc_old = jnp.where(is_first, 0.0, acc_scr[m])
        m_new = jnp.maximum(m_old, jnp.max(s, axis=-1, keepdims=True))
        p = jnp.exp2(s - m_new).astype(BF16)
        alpha = jnp.exp2(m_old - m_new)
        acc_scr[m] = alpha * acc_old + jnp.dot(p, v_aug, preferred_element_type=F32)
        m_scr[m] = m_new


def _attn_kernel(qblk_ref, kvblk_ref, first_ref, last_ref, d0_ref, dc_ref,
                 q_ref, k_ref, v_ref, qc_ref, kc_ref, bd_ref, ed0_ref, lam_ref, gsub_ref, o_ref,
                 s_even, s_odd, m_scr, acc_scr, *, head_dim, n_items):
    g = pl.program_id(0)
    n_entries = pl.num_programs(0) - 1
    it_a = jnp.minimum(g, n_entries - 1) % n_items
    it_b = jnp.maximum(g - 1, 0) % n_items
    v_dim = q_ref.shape[1]

    @pl.when(g == 0)
    def _():
        s_odd[...] = jnp.zeros(s_odd.shape, F32)
        m_scr[...] = jnp.zeros(m_scr.shape, F32)
        acc_scr[...] = jnp.zeros(acc_scr.shape, F32)

    is_first = jnp.full((1, 1), first_ref[it_b], jnp.int32) > 0

    def step(s_write, s_read):
        _attn_scores(s_write, it_a, q_ref, k_ref, qc_ref, kc_ref, bd_ref, ed0_ref, d0_ref, dc_ref, head_dim)
        _attn_softmax_pv(s_read, is_first, v_ref, m_scr, acc_scr)

    @pl.when(g % 2 == 0)
    def _():
        step(s_even, s_odd)

    @pl.when(g % 2 == 1)
    def _():
        step(s_odd, s_even)

    @pl.when(last_ref[it_b] == 1)
    def _():
        lp = lam_ref[...]
        lam_init = lp[4:5, :1]
        lam = (jnp.exp(jnp.sum(lp[0:1] * lp[1:2], axis=-1, keepdims=True))
               - jnp.exp(jnp.sum(lp[2:3] * lp[3:4], axis=-1, keepdims=True)) + lam_init)
        a0 = acc_scr[0]
        a1 = acc_scr[1]
        o = a0[:, :v_dim] / a0[:, v_dim:v_dim + 1] - lam * (a1[:, :v_dim] / a1[:, v_dim:v_dim + 1])
        o_ref[...] = (_rms(o) * gsub_ref[...] * (1.0 - lam_init)).astype(o_ref.dtype)


def _attn_items(geom, tq, chunk):
    qblk, kvblk, first, last, d0s, dcs = [], [], [], [], [], []
    for off, nb, s in ((0, geom.bp, geom.sp), (geom.tp, geom.bs, geom.ss)):
        for b in range(nb):
            for qt in range(s // tq):
                for c in range(s // chunk):
                    qblk.append((off + b * s + qt * tq) // tq)
                    kvblk.append((off + b * s + c * chunk) // chunk)
                    first.append(int(c == 0))
                    last.append(int(c == s // chunk - 1))
                    d0 = qt * tq - c * chunk
                    d0s.append(d0)
                    dcs.append(d0 // tq if 0 <= d0 < chunk else -1)
    return [jnp.asarray(np.asarray(a, np.int32)) for a in (qblk, kvblk, first, last, d0s, dcs)]


def _bf16_parts(x, n):
    parts, rest = [], np.asarray(x, np.float64)
    for _ in range(n):
        p = rest.astype(BF16).astype(np.float64)
        parts.append(p)
        rest = rest - p
    return parts


def _attn_consts(n_heads, head_dim, tq, chunk, max_d0):
    v_dim = 2 * head_dim
    slopes = 2.0 ** (-ALIBI_MAX * np.arange(1, n_heads + 1, dtype=np.float64) / n_heads) * LOG2E
    sl = _bf16_parts(slopes, SLOPE_PARTS)
    r = np.arange(tq)
    j = np.arange(chunk)
    i_parts = [None, (r // BF16_EXACT_INT) * BF16_EXACT_INT, r % BF16_EXACT_INT]
    j_parts = [(j // BF16_EXACT_INT) * BF16_EXACT_INT, j % BF16_EXACT_INT]
    assert chunk <= BF16_EXACT_INT ** 2 and max_d0 % BF16_EXACT_INT == 0 and max_d0 <= BF16_EXACT_INT ** 2
    n_cols = SLOPE_PARTS * (len(i_parts) + len(j_parts))
    assert n_cols <= head_dim
    qc = np.zeros((n_heads, 2, tq, v_dim), np.float64)
    kc = np.zeros((n_heads, 2, chunk, v_dim), np.float64)
    ed0 = np.zeros((2, 1, v_dim), np.float32)
    for m in range(2):
        base = head_dim if m == 0 else 0
        col = base
        for a in range(SLOPE_PARTS):
            for jp in j_parts:
                qc[:, m, :, col] = sl[a][:, None]
                kc[:, m, :, col] = jp[None, :]
                col += 1
            for ip in i_parts:
                if ip is None:
                    ed0[m, 0, col] = 1.0
                else:
                    qc[:, m, :, col] = ip[None, :]
                kc[:, m, :, col] = -sl[a][:, None]
                col += 1
    bd = slopes[:, None, None] * np.abs(r[:, None] - r[None, :])[None]
    for arr in (qc, kc):
        assert np.array_equal(arr.astype(BF16).astype(np.float64), arr)
    return (jnp.asarray(qc.astype(BF16)), jnp.asarray(kc.astype(BF16)),
            jnp.asarray(bd.astype(np.float32)), jnp.asarray(ed0))


def _attention(geom, z, consts, items, lam_pack_l, g_subln_l, n_heads, head_dim, tq, chunk):
    t = z.shape[0]
    v_dim = 2 * head_dim
    qc, kc, bd, ed0 = consts
    n_items = items[0].shape[0]
    k_col0 = n_heads
    v_col0 = 2 * n_heads
    n_entries = n_heads * n_items

    def score_entry(g):
        e = jnp.minimum(g, n_entries - 1)
        return e // n_items, e % n_items

    def finish_entry(g):
        e = jnp.maximum(g - 1, 0)
        return e // n_items, e % n_items

    def q_map(g, qb, kb, *_):
        h, it = score_entry(g)
        return qb[it], h

    def k_map(g, qb, kb, *_):
        h, it = score_entry(g)
        return kb[it], k_col0 + h

    def v_map(g, qb, kb, *_):
        h, it = finish_entry(g)
        return kb[it], v_col0 + h

    def o_map(g, qb, kb, *_):
        h, it = finish_entry(g)
        return qb[it], h

    grid_spec = pltpu.PrefetchScalarGridSpec(
        num_scalar_prefetch=6,
        grid=(n_entries + 1,),
        in_specs=[
            pl.BlockSpec((tq, v_dim), q_map),
            pl.BlockSpec((chunk, v_dim), k_map),
            pl.BlockSpec((chunk, v_dim), v_map),
            pl.BlockSpec((1, 2, tq, v_dim), lambda g, *_: (score_entry(g)[0], 0, 0, 0)),
            pl.BlockSpec((1, 2, chunk, v_dim), lambda g, *_: (score_entry(g)[0], 0, 0, 0)),
            pl.BlockSpec((1, tq, tq), lambda g, *_: (score_entry(g)[0], 0, 0)),
            pl.BlockSpec((2, 1, v_dim), lambda g, *_: (0, 0, 0)),
            pl.BlockSpec((8, head_dim), lambda g, *_: (0, 0)),
            _row_spec(v_dim),
        ],
        out_specs=pl.BlockSpec((tq, v_dim), o_map),
        scratch_shapes=[
            pltpu.VMEM((2, tq, chunk), F32),
            pltpu.VMEM((2, tq, chunk), F32),
            pltpu.VMEM((2, tq, 1), F32),
            pltpu.VMEM((2, tq, 2 * v_dim), F32),
        ],
    )
    return pl.pallas_call(
        functools.partial(_attn_kernel, head_dim=head_dim, n_items=n_items),
        grid_spec=grid_spec,
        out_shape=jax.ShapeDtypeStruct((t, n_heads * v_dim), BF16),
        compiler_params=_params(("arbitrary",)),
        name="diff_attn",
    )(*items, z, z, z, qc, kc, bd, ed0, lam_pack_l, g_subln_l)


def _outproj_kernel(attn_ref, cb_ref, cc_ref, cx_ref, ga_ref, gc_ref,
                    ccp_ref, cxp_ref, ccn_ref, cxn_ref,
                    x_ref, convw_ref, wout_ref, gpost_ref, gatem_ref, gpre_ref, scalef_ref, shiftf_ref,
                    *rest, geom, moe):
    if moe:
        wr_ref, xo_ref, h_ref, tope_ref, topw_ref = rest
    else:
        xo_ref, h_ref = rest
    tm = x_ref.shape[0]
    row0 = pl.program_id(0) * tm
    seq = jnp.where(row0 < geom.tp, geom.sp, geom.ss)
    rel0 = jnp.where(row0 < geom.tp, row0, row0 - geom.tp)
    not_start = (rel0 % seq != 0).astype(F32)
    not_end = ((rel0 + tm) % seq != 0).astype(F32)

    u = cc_ref[...].astype(F32) * cx_ref[...].astype(F32)
    halo = ccp_ref.shape[0]
    u_prev = (ccp_ref[halo - 1:halo, :].astype(F32) * cxp_ref[halo - 1:halo, :].astype(F32)) * not_start
    u_next = (ccn_ref[0:1, :].astype(F32) * cxn_ref[0:1, :].astype(F32)) * not_end
    row = lax.broadcasted_iota(jnp.int32, u.shape, 0)
    u_m1 = jnp.where(row == 0, u_prev, pltpu.roll(u, 1, 0))
    u_p1 = jnp.where(row == tm - 1, u_next, pltpu.roll(u, tm - 1, 0))
    cw = convw_ref[...]
    conv = u_m1 * cw[0:1] + u * cw[1:2] + u_p1 * cw[2:3]
    short = cb_ref[...].astype(F32) * conv
    merged = (jax.nn.sigmoid(ga_ref[...].astype(F32)) * attn_ref[...].astype(F32)
              + jax.nn.sigmoid(gc_ref[...].astype(F32)) * short)
    o = jnp.dot(merged.astype(BF16), wout_ref[...], preferred_element_type=F32)
    x_new = x_ref[...] + gatem_ref[0] * (_rms(o) * gpost_ref[...])
    xo_ref[...] = x_new
    h = _rms(x_new) * gpre_ref[...] * (1.0 + scalef_ref[0]) + shiftf_ref[0]
    h_ref[...] = h.astype(h_ref.dtype)

    if moe:
        n_e = wr_ref.shape[1]
        wr = wr_ref[...]
        h_hi = h.astype(BF16)
        h_lo = (h - h_hi.astype(F32)).astype(BF16)
        w_hi = wr.astype(BF16)
        w_lo = (wr - w_hi.astype(F32)).astype(BF16)
        logits = (jnp.dot(h_hi, w_hi, preferred_element_type=F32) + jnp.dot(h_lo, w_hi, preferred_element_type=F32)
                  + jnp.dot(h_hi, w_lo, preferred_element_type=F32))
        lane = lax.broadcasted_iota(jnp.int32, logits.shape, 1).astype(F32)
        m1 = jnp.max(logits, axis=-1, keepdims=True)
        i1 = jnp.min(jnp.where(logits == m1, lane, float(n_e)), axis=-1, keepdims=True)
        rest_l = jnp.where(lane == i1, -jnp.inf, logits)
        m2 = jnp.max(rest_l, axis=-1, keepdims=True)
        i2 = jnp.min(jnp.where(rest_l == m2, lane, float(n_e)), axis=-1, keepdims=True)
        e = jnp.exp(m2 - m1)
        w1 = 1.0 / (1.0 + e)
        w2 = e / (1.0 + e)
        tope_ref[...] = jnp.where(lane == 0.0, i1, jnp.where(lane == 1.0, i2, 0.0)).astype(jnp.int32)
        topw_ref[...] = jnp.where(lane == 0.0, w1, jnp.where(lane == 1.0, w2, 0.0))


def _outproj(geom, attn, z, x, conv_w_l, w_out_l, g_post, g_pre, mod_l, w_router_l):
    t, d = x.shape
    moe = w_router_l is not None
    tm = _tile(geom.row_gcd, OUTPROJ_ROWS, ROW_ALIGN)
    halo = ROW_ALIGN
    nh = tm // halo
    last_halo = t // halo - 1
    zcol = lambda c: pl.BlockSpec((tm, d), lambda i: (i, c))
    in_specs = [
        pl.BlockSpec((tm, d), lambda i: (i, 0)),
        zcol(3), zcol(4), zcol(5), zcol(6), zcol(7),
        pl.BlockSpec((halo, d), lambda i: (jnp.maximum(i * nh - 1, 0), 4)),
        pl.BlockSpec((halo, d), lambda i: (jnp.maximum(i * nh - 1, 0), 5)),
        pl.BlockSpec((halo, d), lambda i: (jnp.minimum((i + 1) * nh, last_halo), 4)),
        pl.BlockSpec((halo, d), lambda i: (jnp.minimum((i + 1) * nh, last_halo), 5)),
        pl.BlockSpec((tm, d), lambda i: (i, 0)),
        pl.BlockSpec(conv_w_l.shape, lambda i: (0, 0)),
        pl.BlockSpec((d, d), lambda i: (0, 0)),
        _row_spec(d),
        geom.mod_spec(2, tm),
        _row_spec(d),
        geom.mod_spec(4, tm), geom.mod_spec(3, tm),
    ]
    args = [attn, z, z, z, z, z, z, z, z, z, x, conv_w_l, w_out_l, g_post, mod_l, g_pre, mod_l, mod_l]
    out_specs = [pl.BlockSpec((tm, d), lambda i: (i, 0)), pl.BlockSpec((tm, d), lambda i: (i, 0))]
    out_shape = [jax.ShapeDtypeStruct((t, d), F32), jax.ShapeDtypeStruct((t, d), F32 if moe else BF16)]
    if moe:
        n_e = w_router_l.shape[1]
        in_specs.append(pl.BlockSpec((d, n_e), lambda i: (0, 0)))
        args.append(w_router_l)
        out_specs += [pl.BlockSpec((tm, n_e), lambda i: (i, 0))] * 2
        out_shape += [jax.ShapeDtypeStruct((t, n_e), jnp.int32), jax.ShapeDtypeStruct((t, n_e), F32)]
    return pl.pallas_call(
        functools.partial(_outproj_kernel, geom=geom, moe=moe),
        grid=(t // tm,),
        in_specs=in_specs,
        out_specs=out_specs,
        out_shape=out_shape,
        compiler_params=_params(("arbitrary",)),
        name="out_proj_moe" if moe else "out_proj",
    )(*args)


def _swiglu_partial(h, wg, wu, wd):
    g = jnp.dot(h, wg, preferred_element_type=F32)
    u = jnp.dot(h, wu, preferred_element_type=F32)
    a = (g * jax.nn.sigmoid(g)) * u
    return jnp.dot(a.astype(BF16), wd, preferred_element_type=F32)


def _ffn_kernel(h_ref, x_ref, wg_ref, wu_ref, wd_ref, gpost_ref, gate_ref, xo_ref, acc_scr):
    f = pl.program_id(1)
    part = _swiglu_partial(h_ref[...], wg_ref[...], wu_ref[...], wd_ref[...])

    @pl.when(f == 0)
    def _():
        acc_scr[...] = part

    @pl.when(f > 0)
    def _():
        acc_scr[...] += part

    @pl.when(f == pl.num_programs(1) - 1)
    def _():
        xo_ref[...] = x_ref[...] + gate_ref[0] * (_rms(acc_scr[...]) * gpost_ref[...])


def _ffn(geom, h, x, wg, wu, wd, g_post, mod_l):
    t, d = x.shape
    ff = wg.shape[1]
    tm = _tile(geom.row_gcd, FFN_ROWS, ROW_ALIGN)
    tf = _tile(ff, FFN_COLS, LANES)
    return pl.pallas_call(
        _ffn_kernel,
        grid=(t // tm, ff // tf),
        in_specs=[
            pl.BlockSpec((tm, d), lambda i, f: (i, 0)),
            pl.BlockSpec((tm, d), lambda i, f: (i, 0)),
            pl.BlockSpec((d, tf), lambda i, f: (0, f)),
            pl.BlockSpec((d, tf), lambda i, f: (0, f)),
            pl.BlockSpec((tf, d), lambda i, f: (f, 0)),
            _row_spec(d),
            geom.mod_spec(5, tm),
        ],
        out_specs=pl.BlockSpec((tm, d), lambda i, f: (i, 0)),
        out_shape=jax.ShapeDtypeStruct((t, d), F32),
        scratch_shapes=[pltpu.VMEM((tm, d), F32)],
        compiler_params=_params(("arbitrary", "arbitrary")),
        name="dense_ffn",
    )(h, x, wg, wu, wd, g_post, mod_l)


def _expert_kernel(be_ref, idx0_ref, idxn_ref, h_ref, wg_ref, wu_ref, wd_ref, y_ref,
                   xf_scr, xb_scr, sem):
    b = pl.program_id(0)
    f = pl.program_id(1)
    n_blk = pl.num_programs(0)
    n_f = pl.num_programs(1)
    blk, d = xb_scr.shape
    slot = b % 2

    def row_copy(idx_ref, r, s, tile, sub):
        return pltpu.make_async_copy(h_ref.at[pl.ds(idx_ref[r], 1)], xf_scr.at[s, tile, pl.ds(sub, 1)], sem.at[s])

    def block_copy(s):
        return pltpu.make_async_copy(xf_scr.at[s], xf_scr.at[s], sem.at[s])

    @pl.when(jnp.logical_and(b == 0, f == 0))
    def _():
        def issue(r, carry):
            row_copy(idx0_ref, r, 0, r // SUBLANES, r % SUBLANES).start()
            return carry

        lax.fori_loop(0, blk, issue, 0, unroll=DMA_ISSUE_UNROLL)

    @pl.when(f == 0)
    def _():
        block_copy(slot).wait()
        xb_scr[...] = xf_scr[slot].reshape(blk, d).astype(BF16)

    def prefetch_slice():
        per_step = blk // EXPERT_F_STEPS
        for u in range(per_step):
            row_copy(idxn_ref, f * per_step + u, 1 - slot,
                     f * (per_step // SUBLANES) + u // SUBLANES, u % SUBLANES).start()

    part = _swiglu_partial(xb_scr[...], wg_ref[0], wu_ref[0], wd_ref[0])

    @pl.when(f == 0)
    def _():
        prefetch_slice()
        y_ref[...] = part

    @pl.when(f > 0)
    def _():
        prefetch_slice()
        y_ref[...] += part

    @pl.when(jnp.logical_and(b == n_blk - 1, f == n_f - 1))
    def _():
        block_copy(1 - slot).wait()


def _experts(h, row_tok, block_e, wg, wu, wd, blk):
    n_pad = row_tok.shape[0]
    d = h.shape[1]
    ff = wg.shape[2]
    tf = ff // EXPERT_F_STEPS
    n_blk = n_pad // blk
    assert ff % EXPERT_F_STEPS == 0 and tf % LANES == 0 and blk % SMEM_INDEX_ALIGN == 0
    assert blk % (EXPERT_F_STEPS * SUBLANES) == 0 and h.dtype == F32
    grid_spec = pltpu.PrefetchScalarGridSpec(
        num_scalar_prefetch=1,
        grid=(n_blk, EXPERT_F_STEPS),
        in_specs=[
            pl.BlockSpec((blk,), lambda b, f, be: (0,), memory_space=pltpu.SMEM),
            pl.BlockSpec((blk,), lambda b, f, be: (jnp.minimum(b + 1, n_blk - 1),), memory_space=pltpu.SMEM),
            pl.BlockSpec(memory_space=pl.ANY),
            pl.BlockSpec((1, d, tf), lambda b, f, be: (be[b], 0, f)),
            pl.BlockSpec((1, d, tf), lambda b, f, be: (be[b], 0, f)),
            pl.BlockSpec((1, tf, d), lambda b, f, be: (be[b], f, 0)),
        ],
        out_specs=pl.BlockSpec((blk, d), lambda b, f, be: (b, 0)),
        scratch_shapes=[
            pltpu.VMEM((2, blk // SUBLANES, SUBLANES, d), F32),
            pltpu.VMEM((blk, d), BF16),
            pltpu.SemaphoreType.DMA((2,)),
        ],
    )
    return pl.pallas_call(
        _expert_kernel,
        grid_spec=grid_spec,
        out_shape=jax.ShapeDtypeStruct((n_pad, d), F32),
        compiler_params=_params(("arbitrary", "arbitrary")),
        name="expert_ffn",
    )(block_e, row_tok, row_tok, h, wg, wu, wd)


def _combine_kernel(dest_ref, y_ref, w_ref, x_ref, gpost_ref, gate_ref, xo_ref, buf, sem):
    tc, d = x_ref.shape

    def issue(tile, carry):
        for sub in range(SUBLANES):
            for k in range(TOP_K):
                src_row = dest_ref[TOP_K * (tile * SUBLANES + sub) + k]
                pltpu.make_async_copy(y_ref.at[pl.ds(src_row, 1)], buf.at[k, tile, pl.ds(sub, 1)], sem).start()
        return carry

    lax.fori_loop(0, tc // SUBLANES, issue, 0)
    pltpu.make_async_copy(buf, buf, sem).wait()
    w = w_ref[...]
    o = w[:, 0:1] * buf[0].reshape(tc, d)
    for k in range(1, TOP_K):
        o = o + w[:, k:k + 1] * buf[k].reshape(tc, d)
    xo_ref[...] = x_ref[...] + gate_ref[0] * (_rms(o) * gpost_ref[...])


def _combine(geom, y_rows, dest, top_w, x, g_post, mod_l):
    t, d = x.shape
    n_e = top_w.shape[1]
    tc = _tile(geom.row_gcd, COMBINE_ROWS, SMEM_INDEX_ALIGN // TOP_K)
    return pl.pallas_call(
        _combine_kernel,
        grid=(t // tc,),
        in_specs=[
            pl.BlockSpec((TOP_K * tc,), lambda i: (i,), memory_space=pltpu.SMEM),
            pl.BlockSpec(memory_space=pl.ANY),
            pl.BlockSpec((tc, n_e), lambda i: (i, 0)),
            pl.BlockSpec((tc, d), lambda i: (i, 0)),
            _row_spec(d),
            geom.mod_spec(5, tc),
        ],
        out_specs=pl.BlockSpec((tc, d), lambda i: (i, 0)),
        out_shape=jax.ShapeDtypeStruct((t, d), F32),
        scratch_shapes=[pltpu.VMEM((TOP_K, tc // SUBLANES, SUBLANES, d), F32), pltpu.SemaphoreType.DMA(())],
        compiler_params=_params(("arbitrary",)),
        name="moe_combine",
    )(dest, y_rows, top_w, x, g_post, mod_l)


def _route(top_e, n_experts, blk):
    t = top_e.shape[0]
    n_assign = t * TOP_K
    n_blocks = -(-(n_assign + n_experts * (blk - 1)) // blk)
    n_pad = n_blocks * blk
    flat_e = top_e[:, :TOP_K].reshape(-1)
    onehot = (flat_e[:, None] == jnp.arange(n_experts, dtype=jnp.int32)[None, :]).astype(jnp.int32)
    rank = jnp.sum((jnp.cumsum(onehot, axis=0) - onehot) * onehot, axis=1)
    counts = jnp.sum(onehot, axis=0)
    padded = (counts + blk - 1) // blk * blk
    end_pad = jnp.cumsum(padded)
    start_pad = end_pad - padded
    dest = (start_pad[flat_e] + rank).astype(jnp.int32)
    flat_tok = jnp.arange(n_assign, dtype=jnp.int32) // TOP_K
    row_tok = jnp.zeros((n_pad,), jnp.int32).at[dest].set(flat_tok)
    block_e = jnp.minimum(
        jnp.searchsorted(end_pad, jnp.arange(n_blocks, dtype=jnp.int32) * blk, side='right'),
        n_experts - 1).astype(jnp.int32)
    return dest, row_tok, block_e


def kernel(x_prompt, x_sample, c_prompt, c_sample, w_ada, b_ada, g_mix_pre, g_mix_post, g_ffn_pre, g_ffn_post,
           w_in, lam_q1, lam_k1, lam_q2, lam_k2, g_subln, conv_w, w_out, w_ffn_gate, w_ffn_up, w_ffn_down,
           w_router, w_exp_gate, w_exp_up, w_exp_down):
    bp, sp, d = x_prompt.shape
    bs, ss, _ = x_sample.shape
    depth = w_in.shape[0]
    head_dim = lam_q1.shape[1]
    v_dim = g_subln.shape[1]
    n_heads = d // v_dim
    n_experts = w_router.shape[2]
    n_in = w_in.shape[2]
    assert v_dim == 2 * head_dim and n_in == 8 * d
    geom = _Geom(bp, sp, bs, ss, d)

    x = jnp.concatenate([x_prompt.reshape(bp * sp, d), x_sample.reshape(bs * ss, d)], axis=0)
    c_all = jnp.concatenate([c_prompt, c_sample, jnp.zeros((geom.nb_pad - geom.nb, d), F32)], axis=0)
    mod = _ada(c_all, w_ada, b_ada)

    tq = _tile(geom.row_gcd, ATTN_Q_ROWS, ROW_ALIGN)
    chunk = _tile(geom.row_gcd, ATTN_KV_CHUNK, tq)
    attn_consts = _attn_consts(n_heads, head_dim, tq, chunk, max(sp, ss))
    attn_items = _attn_items(geom, tq, chunk)
    col_scale = jnp.asarray(np.where(np.arange(n_in) < n_heads * v_dim, head_dim ** -0.5 * LOG2E, 1.0)
                            .astype(np.float32))[None, :]
    lam_init = np.asarray([0.8 - 0.6 * math.exp(-0.3 * l) for l in range(depth)], np.float32)
    lam_pack = jnp.stack(
        [lam_q1, lam_k1, lam_q2, lam_k2, jnp.broadcast_to(jnp.asarray(lam_init)[:, None], lam_q1.shape)]
        + [jnp.zeros_like(lam_q1)] * 3, axis=1)

    per_layer = dict(mod=mod, g_mix_pre=g_mix_pre[:, None], g_mix_post=g_mix_post[:, None],
                     g_ffn_pre=g_ffn_pre[:, None], g_ffn_post=g_ffn_post[:, None], g_subln=g_subln[:, None],
                     w_in=w_in, lam=lam_pack, conv_w=conv_w, w_out=w_out)

    def mixer(x, p, w_router_l):
        z = _inproj(geom, x, p["g_mix_pre"], p["mod"], p["w_in"].astype(BF16), col_scale)
        attn = _attention(geom, z, attn_consts, attn_items, p["lam"], p["g_subln"], n_heads, head_dim, tq, chunk)
        return _outproj(geom, attn, z, x, p["conv_w"], p["w_out"].astype(BF16), p["g_mix_post"], p["g_ffn_pre"],
                        p["mod"], w_router_l)

    def dense_layer(x, p, w):
        x, h = mixer(x, p, None)
        return _ffn(geom, h, x, w["gate"].astype(BF16), w["up"].astype(BF16), w["down"].astype(BF16),
                    p["g_ffn_post"], p["mod"])

    def expert_layer(x, p, w):
        x, h, top_e, top_w = mixer(x, p, w["router"])
        dest, row_tok, block_e = _route(top_e, n_experts, MOE_BLOCK)
        y_rows = _experts(h, row_tok, block_e, w["gate"].astype(BF16), w["up"].astype(BF16),
                          w["down"].astype(BF16), MOE_BLOCK)
        return _combine(geom, y_rows, dest, top_w, x, p["g_ffn_post"], p["mod"])

    n_pairs = depth // 2
    even = jax.tree.map(lambda a: a[0:2 * n_pairs:2], per_layer)
    odd = jax.tree.map(lambda a: a[1:2 * n_pairs:2], per_layer)
    dense_w = dict(gate=w_ffn_gate, up=w_ffn_up, down=w_ffn_down)
    exp_w = dict(router=w_router, gate=w_exp_gate, up=w_exp_up, down=w_exp_down)

    def pair(x, xs):
        p_even, p_odd, dw, ew = xs
        x = dense_layer(x, p_even, dw)
        x = expert_layer(x, p_odd, ew)
        return x, None

    x, _ = lax.scan(pair, x, (even, odd, jax.tree.map(lambda a: a[:n_pairs], dense_w), exp_w))
    if depth % 2 == 1:
        x = dense_layer(x, jax.tree.map(lambda a: a[depth - 1], per_layer),
                        jax.tree.map(lambda a: a[n_pairs], dense_w))

    y_prompt = x[:geom.tp].reshape(bp, sp, d)
    y_sample = x[geom.tp:].reshape(bs, ss, d)
    return (y_prompt, y_sample)
```

```python
import functools
import math

import numpy as np
import jax
import jax.numpy as jnp
from jax import lax
from jax.experimental import pallas as pl
from jax.experimental.pallas import tpu as pltpu

F32 = jnp.float32
BF16 = jnp.bfloat16

ALIBI_MAX = 8.0
NORM_EPS = 1e-6
TOP_K = 2
N_MOD = 6
LOG2E = math.log2(math.e)

VMEM_LIMIT_BYTES = 56 * 1024 * 1024
ROW_ALIGN = 16
SUBLANES = 8
LANES = 128
BF16_EXACT_INT = 256
SLOPE_PARTS = 3

INPROJ_ROWS, INPROJ_COLS = 1024, 2048
ATTN_Q_ROWS, ATTN_KV_CHUNK = 512, 2048
OUTPROJ_ROWS = 256
FFN_ROWS, FFN_COLS = 512, 1408
MOE_BLOCK = 1024
EXPERT_F_STEPS = 4
COMBINE_ROWS = 512
DMA_ISSUE_UNROLL = 8
DMA_PRIORITIES = 2
SMEM_INDEX_ALIGN = 1024


def _tile(n, pref, align=1):
    t = min(n, pref)
    while t > 0:
        if n % t == 0 and t % align == 0:
            return t
        t -= 1
    raise ValueError(f"no tile for {n} (pref {pref}, align {align})")


def _params(sem):
    return pltpu.CompilerParams(dimension_semantics=sem, vmem_limit_bytes=VMEM_LIMIT_BYTES)


def _rms(x):
    return x * lax.rsqrt(jnp.mean(x * x, axis=-1, keepdims=True) + NORM_EPS)


def _row_spec(d, fn=None):
    return pl.BlockSpec((1, d), fn if fn is not None else (lambda *_: (0, 0)))


def _ada_kernel(c_ref, w_ref, b_ref, o_ref):
    c = c_ref[...]
    cond = c * jax.nn.sigmoid(c)
    o_ref[0] = jnp.dot(cond, w_ref[0], preferred_element_type=F32,
                       precision=lax.Precision.HIGHEST) + b_ref[0]


def _ada(c_all, w_ada, b_ada):
    depth, d, _ = w_ada.shape
    bp = c_all.shape[0]
    b3 = b_ada.reshape(depth * N_MOD, 1, d)
    out = pl.pallas_call(
        _ada_kernel,
        grid=(depth, N_MOD),
        in_specs=[
            pl.BlockSpec((bp, d), lambda l, k: (0, 0)),
            pl.BlockSpec((1, d, d), lambda l, k: (l, 0, k)),
            pl.BlockSpec((1, 1, d), lambda l, k: (l * N_MOD + k, 0, 0)),
        ],
        out_specs=pl.BlockSpec((1, bp, d), lambda l, k: (l * N_MOD + k, 0, 0)),
        out_shape=jax.ShapeDtypeStruct((depth * N_MOD, bp, d), F32),
        compiler_params=_params(("arbitrary", "arbitrary")),
        name="ada_mod",
    )(c_all, w_ada, b3)
    return out.reshape(depth, N_MOD * bp, 1, d)


class _Geom:
    def __init__(self, bp, sp, bs, ss, d):
        self.bp, self.sp, self.bs, self.ss, self.d = bp, sp, bs, ss, d
        self.tp = bp * sp
        self.t = self.tp + bs * ss
        self.nb = bp + bs
        self.nb_pad = -(-self.nb // 8) * 8
        self.row_gcd = math.gcd(sp, ss)

    def batch_of_tile(self, i, rows):
        npt = self.tp // rows
        return jnp.where(i < npt, i // (self.sp // rows), self.bp + (i - npt) // (self.ss // rows))

    def mod_spec(self, k, rows):
        return pl.BlockSpec((1, 1, self.d), lambda i, *_: (k * self.nb_pad + self.batch_of_tile(i, rows), 0, 0))


def _inproj_kernel(x_ref, g_ref, scale_ref, shift_ref, w_ref, cs_ref, z_ref, h_scr):
    @pl.when(pl.program_id(1) == 0)
    def _():
        h = _rms(x_ref[...]) * g_ref[...] * (1.0 + scale_ref[0]) + shift_ref[0]
        h_scr[...] = h.astype(BF16)

    z = (jnp.dot(h_scr[...], w_ref[...], preferred_element_type=F32) * cs_ref[...]).astype(BF16)
    for s in range(z_ref.shape[0]):
        z_ref[s] = z[:, s * LANES:(s + 1) * LANES]


def _inproj(geom, x, g_pre, mod_l, w_in_l, col_scale):
    t, d = x.shape
    n = w_in_l.shape[1]
    tm = _tile(geom.row_gcd, INPROJ_ROWS, ROW_ALIGN)
    tn = _tile(n, INPROJ_COLS, LANES)
    return pl.pallas_call(
        _inproj_kernel,
        grid=(t // tm, n // tn),
        in_specs=[
            pl.BlockSpec((tm, d), lambda i, j: (i, 0)),
            _row_spec(d),
            geom.mod_spec(1, tm),
            geom.mod_spec(0, tm),
            pl.BlockSpec((d, tn), lambda i, j: (0, j)),
            pl.BlockSpec((1, tn), lambda i, j: (0, j)),
        ],
        out_specs=pl.BlockSpec((tn // LANES, tm, LANES), lambda i, j: (j, i, 0)),
        out_shape=jax.ShapeDtypeStruct((n // LANES, t, LANES), BF16),
        scratch_shapes=[pltpu.VMEM((tm, d), BF16)],
        compiler_params=_params(("arbitrary", "arbitrary")),
        name="in_proj",
    )(x, g_pre, mod_l, mod_l, w_in_l, col_scale)


def _attn_scores(s_ref, it, q_ref, k_ref, qc_ref, kc_ref, bd_ref, ed0_ref, d0_ref, dc_ref, head_dim):
    _, tq, v_dim = q_ref.shape
    n_sub = k_ref.shape[1] // tq
    q = q_ref[0]
    lane = lax.broadcasted_iota(jnp.int32, (tq, v_dim), 1)
    half = (lane < head_dim, lane >= head_dim)
    d0 = d0_ref[it]
    d0_v = jnp.full((1, v_dim), d0, jnp.int32).astype(F32)
    zero = jnp.zeros_like(q)
    for m in range(2):
        q_aug = jnp.where(half[m], q, zero) + qc_ref[0, m] + (d0_v * ed0_ref[m]).astype(BF16)
        for c in range(n_sub):
            delta = d0 - c * tq
            sgn = (delta > 0).astype(jnp.int32) - (delta < 0).astype(jnp.int32)
            sgn_v = jnp.full((1, v_dim), sgn, jnp.int32).astype(F32).astype(BF16)
            k_aug = jnp.where(half[m], k_ref[0, c * tq:(c + 1) * tq, :], sgn_v * kc_ref[0, m, c * tq:(c + 1) * tq, :])
            s_ref[m, :, c * tq:(c + 1) * tq] = lax.dot_general(
                q_aug, k_aug, (((1,), (1,)), ((), ())), preferred_element_type=F32)
    dc = dc_ref[it]
    flag = jnp.full((1, 1), (dc >= 0).astype(jnp.int32), jnp.int32).astype(F32)
    off = pl.multiple_of(jnp.maximum(dc, 0) * tq, tq)
    bias = flag * bd_ref[0]
    for m in range(2):
        s_ref[m, :, pl.ds(off, tq)] = s_ref[m, :, pl.ds(off, tq)] - bias


def _attn_softmax_pv(s_ref, is_first, v_ref, m_scr, acc_scr):
    _, chunk, v_dim = v_ref.shape
    ones_col = jnp.where(lax.broadcasted_iota(jnp.int32, (chunk, v_dim), 1) == 0, 1.0, 0.0).astype(BF16)
    v_aug = jnp.concatenate([v_ref[0], ones_col], axis=1)
    for m in range(2):
        s = s_ref[m]
        m_old = jnp.where(is_first, -jnp.inf, m_scr[m])
        acc_old = jnp.where(is_first, 0.0, acc_scr[m])
        m_new = jnp.maximum(m_old, jnp.max(s, axis=-1, keepdims=True))
        p = jnp.exp2(s - m_new).astype(BF16)
        alpha = jnp.exp2(m_old - m_new)
        acc_scr[m] = alpha * acc_old + jnp.dot(p, v_aug, preferred_element_type=F32)
        m_scr[m] = m_new


def _attn_kernel(qblk_ref, kvblk_ref, first_ref, last_ref, d0_ref, dc_ref,
                 q_ref, k_ref, v_ref, qc_ref, kc_ref, bd_ref, ed0_ref, lam_ref, gsub_ref, o_ref,
                 s_even, s_odd, m_scr, acc_scr, *, head_dim, n_items):
    g = pl.program_id(0)
    n_entries = pl.num_programs(0) - 1
    it_a = jnp.minimum(g, n_entries - 1) % n_items
    it_b = jnp.maximum(g - 1, 0) % n_items
    v_dim = q_ref.shape[2]

    @pl.when(g == 0)
    def _():
        s_odd[...] = jnp.zeros(s_odd.shape, F32)
        m_scr[...] = jnp.zeros(m_scr.shape, F32)
        acc_scr[...] = jnp.zeros(acc_scr.shape, F32)

    is_first = jnp.full((1, 1), first_ref[it_b], jnp.int32) > 0

    def step(s_write, s_read):
        _attn_scores(s_write, it_a, q_ref, k_ref, qc_ref, kc_ref, bd_ref, ed0_ref, d0_ref, dc_ref, head_dim)
        _attn_softmax_pv(s_read, is_first, v_ref, m_scr, acc_scr)

    @pl.when(g % 2 == 0)
    def _():
        step(s_even, s_odd)

    @pl.when(g % 2 == 1)
    def _():
        step(s_odd, s_even)

    @pl.when(last_ref[it_b] == 1)
    def _():
        lp = lam_ref[...]
        lam_init = lp[4:5, :1]
        lam = (jnp.exp(jnp.sum(lp[0:1] * lp[1:2], axis=-1, keepdims=True))
               - jnp.exp(jnp.sum(lp[2:3] * lp[3:4], axis=-1, keepdims=True)) + lam_init)
        a0 = acc_scr[0]
        a1 = acc_scr[1]
        o = a0[:, :v_dim] / a0[:, v_dim:v_dim + 1] - lam * (a1[:, :v_dim] / a1[:, v_dim:v_dim + 1])
        o_ref[0] = (_rms(o) * gsub_ref[...] * (1.0 - lam_init)).astype(o_ref.dtype)


def _attn_items(geom, tq, chunk):
    qblk, kvblk, first, last, d0s, dcs = [], [], [], [], [], []
    for off, nb, s in ((0, geom.bp, geom.sp), (geom.tp, geom.bs, geom.ss)):
        for b in range(nb):
            for qt in range(s // tq):
                for c in range(s // chunk):
                    qblk.append((off + b * s + qt * tq) // tq)
                    kvblk.append((off + b * s + c * chunk) // chunk)
                    first.append(int(c == 0))
                    last.append(int(c == s // chunk - 1))
                    d0 = qt * tq - c * chunk
                    d0s.append(d0)
                    dcs.append(d0 // tq if 0 <= d0 < chunk else -1)
    return [jnp.asarray(np.asarray(a, np.int32)) for a in (qblk, kvblk, first, last, d0s, dcs)]


def _bf16_parts(x, n):
    parts, rest = [], np.asarray(x, np.float64)
    for _ in range(n):
        p = rest.astype(BF16).astype(np.float64)
        parts.append(p)
        rest = rest - p
    return parts


def _attn_consts(n_heads, head_dim, tq, chunk, max_d0):
    v_dim = 2 * head_dim
    slopes = 2.0 ** (-ALIBI_MAX * np.arange(1, n_heads + 1, dtype=np.float64) / n_heads) * LOG2E
    sl = _bf16_parts(slopes, SLOPE_PARTS)
    r = np.arange(tq)
    j = np.arange(chunk)
    i_parts = [None, (r // BF16_EXACT_INT) * BF16_EXACT_INT, r % BF16_EXACT_INT]
    j_parts = [(j // BF16_EXACT_INT) * BF16_EXACT_INT, j % BF16_EXACT_INT]
    assert chunk <= BF16_EXACT_INT ** 2 and max_d0 % BF16_EXACT_INT == 0 and max_d0 <= BF16_EXACT_INT ** 2
    n_cols = SLOPE_PARTS * (len(i_parts) + len(j_parts))
    assert n_cols <= head_dim
    qc = np.zeros((n_heads, 2, tq, v_dim), np.float64)
    kc = np.zeros((n_heads, 2, chunk, v_dim), np.float64)
    ed0 = np.zeros((2, 1, v_dim), np.float32)
    for m in range(2):
        base = head_dim if m == 0 else 0
        col = base
        for a in range(SLOPE_PARTS):
            for jp in j_parts:
                qc[:, m, :, col] = sl[a][:, None]
                kc[:, m, :, col] = jp[None, :]
                col += 1
            for ip in i_parts:
                if ip is None:
                    ed0[m, 0, col] = 1.0
                else:
                    qc[:, m, :, col] = ip[None, :]
                kc[:, m, :, col] = -sl[a][:, None]
                col += 1
    bd = slopes[:, None, None] * np.abs(r[:, None] - r[None, :])[None]
    for arr in (qc, kc):
        assert np.array_equal(arr.astype(BF16).astype(np.float64), arr)
    return (jnp.asarray(qc.astype(BF16)), jnp.asarray(kc.astype(BF16)),
            jnp.asarray(bd.astype(np.float32)), jnp.asarray(ed0))


def _attention(geom, qkv, consts, items, lam_pack_l, g_subln_l, n_heads, head_dim, tq, chunk):
    t = qkv.shape[1]
    v_dim = 2 * head_dim
    qc, kc, bd, ed0 = consts
    n_items = items[0].shape[0]
    k_head0 = n_heads
    v_head0 = 2 * n_heads
    n_entries = n_heads * n_items

    def score_entry(g):
        e = jnp.minimum(g, n_entries - 1)
        return e // n_items, e % n_items

    def finish_entry(g):
        e = jnp.maximum(g - 1, 0)
        return e // n_items, e % n_items

    def q_map(g, qb, kb, *_):
        h, it = score_entry(g)
        return h, qb[it], 0

    def k_map(g, qb, kb, *_):
        h, it = score_entry(g)
        return k_head0 + h, kb[it], 0

    def v_map(g, qb, kb, *_):
        h, it = finish_entry(g)
        return v_head0 + h, kb[it], 0

    def o_map(g, qb, kb, *_):
        h, it = finish_entry(g)
        return h, qb[it], 0

    grid_spec = pltpu.PrefetchScalarGridSpec(
        num_scalar_prefetch=6,
        grid=(n_entries + 1,),
        in_specs=[
            pl.BlockSpec((1, tq, v_dim), q_map),
            pl.BlockSpec((1, chunk, v_dim), k_map),
            pl.BlockSpec((1, chunk, v_dim), v_map),
            pl.BlockSpec((1, 2, tq, v_dim), lambda g, *_: (score_entry(g)[0], 0, 0, 0)),
            pl.BlockSpec((1, 2, chunk, v_dim), lambda g, *_: (score_entry(g)[0], 0, 0, 0)),
            pl.BlockSpec((1, tq, tq), lambda g, *_: (score_entry(g)[0], 0, 0)),
            pl.BlockSpec((2, 1, v_dim), lambda g, *_: (0, 0, 0)),
            pl.BlockSpec((8, head_dim), lambda g, *_: (0, 0)),
            _row_spec(v_dim),
        ],
        out_specs=pl.BlockSpec((1, tq, v_dim), o_map),
        scratch_shapes=[
            pltpu.VMEM((2, tq, chunk), F32),
            pltpu.VMEM((2, tq, chunk), F32),
            pltpu.VMEM((2, tq, 1), F32),
            pltpu.VMEM((2, tq, 2 * v_dim), F32),
        ],
    )
    return pl.pallas_call(
        functools.partial(_attn_kernel, head_dim=head_dim, n_items=n_items),
        grid_spec=grid_spec,
        out_shape=jax.ShapeDtypeStruct((n_heads, t, v_dim), BF16),
        compiler_params=_params(("arbitrary",)),
        name="diff_attn",
    )(*items, qkv, qkv, qkv, qc, kc, bd, ed0, lam_pack_l, g_subln_l)


def _outproj_kernel(attn_ref, cb_ref, cc_ref, cx_ref, ga_ref, gc_ref,
                    ccp_ref, cxp_ref, ccn_ref, cxn_ref,
                    x_ref, convw_ref, wout_ref, gpost_ref, gatem_ref, gpre_ref, scalef_ref, shiftf_ref,
                    *rest, geom, moe):
    if moe:
        wr_ref, xo_ref, h_ref, tope_ref, topw_ref = rest
    else:
        xo_ref, h_ref = rest
    tm = x_ref.shape[0]
    row0 = pl.program_id(0) * tm
    seq = jnp.where(row0 < geom.tp, geom.sp, geom.ss)
    rel0 = jnp.where(row0 < geom.tp, row0, row0 - geom.tp)
    not_start = (rel0 % seq != 0).astype(F32)
    not_end = ((rel0 + tm) % seq != 0).astype(F32)

    def rows(ref, lo=0, hi=None):
        hi = ref.shape[1] if hi is None else hi
        return jnp.concatenate([ref[s, lo:hi, :] for s in range(ref.shape[0])], axis=1).astype(F32)

    u = rows(cc_ref) * rows(cx_ref)
    halo = ccp_ref.shape[1]
    u_prev = (rows(ccp_ref, halo - 1, halo) * rows(cxp_ref, halo - 1, halo)) * not_start
    u_next = (rows(ccn_ref, 0, 1) * rows(cxn_ref, 0, 1)) * not_end
    row = lax.broadcasted_iota(jnp.int32, u.shape, 0)
    u_m1 = jnp.where(row == 0, u_prev, pltpu.roll(u, 1, 0))
    u_p1 = jnp.where(row == tm - 1, u_next, pltpu.roll(u, tm - 1, 0))
    cw = convw_ref[...]
    conv = u_m1 * cw[0:1] + u * cw[1:2] + u_p1 * cw[2:3]
    short = rows(cb_ref) * conv
    merged = jax.nn.sigmoid(rows(ga_ref)) * rows(attn_ref) + jax.nn.sigmoid(rows(gc_ref)) * short
    o = jnp.dot(merged.astype(BF16), wout_ref[...], preferred_element_type=F32)
    x_new = x_ref[...] + gatem_ref[0] * (_rms(o) * gpost_ref[...])
    xo_ref[...] = x_new
    h = _rms(x_new) * gpre_ref[...] * (1.0 + scalef_ref[0]) + shiftf_ref[0]
    h_ref[...] = h.astype(h_ref.dtype)

    if moe:
        n_e = wr_ref.shape[1]
        wr = wr_ref[...]
        h_hi = h.astype(BF16)
        h_lo = (h - h_hi.astype(F32)).astype(BF16)
        w_hi = wr.astype(BF16)
        w_lo = (wr - w_hi.astype(F32)).astype(BF16)
        logits = (jnp.dot(h_hi, w_hi, preferred_element_type=F32) + jnp.dot(h_lo, w_hi, preferred_element_type=F32)
                  + jnp.dot(h_hi, w_lo, preferred_element_type=F32))
        lane = lax.broadcasted_iota(jnp.int32, logits.shape, 1).astype(F32)
        m1 = jnp.max(logits, axis=-1, keepdims=True)
        i1 = jnp.min(jnp.where(logits == m1, lane, float(n_e)), axis=-1, keepdims=True)
        rest_l = jnp.where(lane == i1, -jnp.inf, logits)
        m2 = jnp.max(rest_l, axis=-1, keepdims=True)
        i2 = jnp.min(jnp.where(rest_l == m2, lane, float(n_e)), axis=-1, keepdims=True)
        e = jnp.exp(m2 - m1)
        w1 = 1.0 / (1.0 + e)
        w2 = e / (1.0 + e)
        tope_ref[...] = jnp.where(lane == 0.0, i1, jnp.where(lane == 1.0, i2, 0.0)).astype(jnp.int32)
        topw_ref[...] = jnp.where(lane == 0.0, w1, jnp.where(lane == 1.0, w2, 0.0))


def _outproj(geom, attn, z, x, conv_w_l, w_out_l, g_post, g_pre, mod_l, w_router_l):
    t, d = x.shape
    moe = w_router_l is not None
    tm = _tile(geom.row_gcd, OUTPROJ_ROWS, ROW_ALIGN)
    halo = ROW_ALIGN
    nh = tm // halo
    last_halo = t // halo - 1
    n_heads, _, v_dim = attn.shape
    spt = d // LANES
    zcol = lambda c: pl.BlockSpec((spt, tm, LANES), lambda i: (c, i, 0))
    zhalo = lambda c, fn: pl.BlockSpec((spt, halo, LANES), lambda i: (c, fn(i), 0))
    prev_blk = lambda i: jnp.maximum(i * nh - 1, 0)
    next_blk = lambda i: jnp.minimum((i + 1) * nh, last_halo)
    in_specs = [
        pl.BlockSpec((n_heads, tm, v_dim), lambda i: (0, i, 0)),
        zcol(3), zcol(4), zcol(5), zcol(6), zcol(7),
        zhalo(4, prev_blk), zhalo(5, prev_blk), zhalo(4, next_blk), zhalo(5, next_blk),
        pl.BlockSpec((tm, d), lambda i: (i, 0)),
        pl.BlockSpec(conv_w_l.shape, lambda i: (0, 0)),
        pl.BlockSpec((d, d), lambda i: (0, 0)),
        _row_spec(d),
        geom.mod_spec(2, tm),
        _row_spec(d),
        geom.mod_spec(4, tm), geom.mod_spec(3, tm),
    ]
    args = [attn, z, z, z, z, z, z, z, z, z, x, conv_w_l, w_out_l, g_post, mod_l, g_pre, mod_l, mod_l]
    out_specs = [pl.BlockSpec((tm, d), lambda i: (i, 0)), pl.BlockSpec((tm, d), lambda i: (i, 0))]
    out_shape = [jax.ShapeDtypeStruct((t, d), F32), jax.ShapeDtypeStruct((t, d), F32 if moe else BF16)]
    if moe:
        n_e = w_router_l.shape[1]
        in_specs.append(pl.BlockSpec((d, n_e), lambda i: (0, 0)))
        args.append(w_router_l)
        out_specs += [pl.BlockSpec((tm, n_e), lambda i: (i, 0))] * 2
        out_shape += [jax.ShapeDtypeStruct((t, n_e), jnp.int32), jax.ShapeDtypeStruct((t, n_e), F32)]
    return pl.pallas_call(
        functools.partial(_outproj_kernel, geom=geom, moe=moe),
        grid=(t // tm,),
        in_specs=in_specs,
        out_specs=out_specs,
        out_shape=out_shape,
        compiler_params=_params(("arbitrary",)),
        name="out_proj_moe" if moe else "out_proj",
    )(*args)


def _swiglu_partial(h, wg, wu, wd):
    g = jnp.dot(h, wg, preferred_element_type=F32)
    u = jnp.dot(h, wu, preferred_element_type=F32)
    a = (g * jax.nn.sigmoid(g)) * u
    return jnp.dot(a.astype(BF16), wd, preferred_element_type=F32)


def _ffn_kernel(h_ref, x_ref, wg_ref, wu_ref, wd_ref, gpost_ref, gate_ref, xo_ref, acc_scr):
    f = pl.program_id(1)
    part = _swiglu_partial(h_ref[...], wg_ref[...], wu_ref[...], wd_ref[...])

    @pl.when(f == 0)
    def _():
        acc_scr[...] = part

    @pl.when(f > 0)
    def _():
        acc_scr[...] += part

    @pl.when(f == pl.num_programs(1) - 1)
    def _():
        xo_ref[...] = x_ref[...] + gate_ref[0] * (_rms(acc_scr[...]) * gpost_ref[...])


def _ffn(geom, h, x, wg, wu, wd, g_post, mod_l):
    t, d = x.shape
    ff = wg.shape[1]
    tm = _tile(geom.row_gcd, FFN_ROWS, ROW_ALIGN)
    tf = _tile(ff, FFN_COLS, LANES)
    return pl.pallas_call(
        _ffn_kernel,
        grid=(t // tm, ff // tf),
        in_specs=[
            pl.BlockSpec((tm, d), lambda i, f: (i, 0)),
            pl.BlockSpec((tm, d), lambda i, f: (i, 0)),
            pl.BlockSpec((d, tf), lambda i, f: (0, f)),
            pl.BlockSpec((d, tf), lambda i, f: (0, f)),
            pl.BlockSpec((tf, d), lambda i, f: (f, 0)),
            _row_spec(d),
            geom.mod_spec(5, tm),
        ],
        out_specs=pl.BlockSpec((tm, d), lambda i, f: (i, 0)),
        out_shape=jax.ShapeDtypeStruct((t, d), F32),
        scratch_shapes=[pltpu.VMEM((tm, d), F32)],
        compiler_params=_params(("arbitrary", "arbitrary")),
        name="dense_ffn",
    )(h, x, wg, wu, wd, g_post, mod_l)


def _expert_kernel(be_ref, idx0_ref, idxn_ref, h_ref, wg_ref, wu_ref, wd_ref, y_ref,
                   xf_scr, xb_scr, sem):
    b = pl.program_id(0)
    f = pl.program_id(1)
    n_blk = pl.num_programs(0)
    n_f = pl.num_programs(1)
    blk, d = xb_scr.shape
    slot = b % 2

    def row_copy(idx_ref, r, s, tile, sub):
        return pltpu.make_async_copy(h_ref.at[pl.ds(idx_ref[r], 1)], xf_scr.at[s, tile, pl.ds(sub, 1)], sem.at[s])

    def block_copy(s):
        return pltpu.make_async_copy(xf_scr.at[s], xf_scr.at[s], sem.at[s])

    @pl.when(jnp.logical_and(b == 0, f == 0))
    def _():
        def issue(r, carry):
            row_copy(idx0_ref, r, 0, r // SUBLANES, r % SUBLANES).start()
            return carry

        lax.fori_loop(0, blk, issue, 0, unroll=DMA_ISSUE_UNROLL)

    @pl.when(f == 0)
    def _():
        block_copy(slot).wait()
        xb_scr[...] = xf_scr[slot].reshape(blk, d).astype(BF16)

    def prefetch_slice():
        per_step = blk // EXPERT_F_STEPS
        for u in range(per_step):
            row_copy(idxn_ref, f * per_step + u, 1 - slot,
                     f * (per_step // SUBLANES) + u // SUBLANES, u % SUBLANES).start(priority=u % DMA_PRIORITIES)

    part = _swiglu_partial(xb_scr[...], wg_ref[0], wu_ref[0], wd_ref[0])

    @pl.when(f == 0)
    def _():
        prefetch_slice()
        y_ref[...] = part

    @pl.when(f > 0)
    def _():
        prefetch_slice()
        y_ref[...] += part

    @pl.when(jnp.logical_and(b == n_blk - 1, f == n_f - 1))
    def _():
        block_copy(1 - slot).wait()


def _experts(h, row_tok, block_e, wg, wu, wd, blk):
    n_pad = row_tok.shape[0]
    d = h.shape[1]
    ff = wg.shape[2]
    tf = ff // EXPERT_F_STEPS
    n_blk = n_pad // blk
    assert ff % EXPERT_F_STEPS == 0 and tf % LANES == 0 and blk % SMEM_INDEX_ALIGN == 0
    assert blk % (EXPERT_F_STEPS * SUBLANES) == 0 and h.dtype == F32
    grid_spec = pltpu.PrefetchScalarGridSpec(
        num_scalar_prefetch=1,
        grid=(n_blk, EXPERT_F_STEPS),
        in_specs=[
            pl.BlockSpec((blk,), lambda b, f, be: (0,), memory_space=pltpu.SMEM),
            pl.BlockSpec((blk,), lambda b, f, be: (jnp.minimum(b + 1, n_blk - 1),), memory_space=pltpu.SMEM),
            pl.BlockSpec(memory_space=pl.ANY),
            pl.BlockSpec((1, d, tf), lambda b, f, be: (be[b], 0, f)),
            pl.BlockSpec((1, d, tf), lambda b, f, be: (be[b], 0, f)),
            pl.BlockSpec((1, tf, d), lambda b, f, be: (be[b], f, 0)),
        ],
        out_specs=pl.BlockSpec((blk, d), lambda b, f, be: (b, 0)),
        scratch_shapes=[
            pltpu.VMEM((2, blk // SUBLANES, SUBLANES, d), F32),
            pltpu.VMEM((blk, d), BF16),
            pltpu.SemaphoreType.DMA((2,)),
        ],
    )
    return pl.pallas_call(
        _expert_kernel,
        grid_spec=grid_spec,
        out_shape=jax.ShapeDtypeStruct((n_pad, d), F32),
        compiler_params=_params(("arbitrary", "arbitrary")),
        name="expert_ffn",
    )(block_e, row_tok, row_tok, h, wg, wu, wd)


def _combine_kernel(dest_ref, y_ref, w_ref, x_ref, gpost_ref, gate_ref, xo_ref, buf, sem):
    tc, d = x_ref.shape

    def issue(tile, carry):
        for sub in range(SUBLANES):
            for k in range(TOP_K):
                src_row = dest_ref[TOP_K * (tile * SUBLANES + sub) + k]
                pltpu.make_async_copy(y_ref.at[pl.ds(src_row, 1)], buf.at[k, tile, pl.ds(sub, 1)],
                                      sem).start(priority=k % DMA_PRIORITIES)
        return carry

    lax.fori_loop(0, tc // SUBLANES, issue, 0)
    pltpu.make_async_copy(buf, buf, sem).wait()
    w = w_ref[...]
    o = w[:, 0:1] * buf[0].reshape(tc, d)
    for k in range(1, TOP_K):
        o = o + w[:, k:k + 1] * buf[k].reshape(tc, d)
    xo_ref[...] = x_ref[...] + gate_ref[0] * (_rms(o) * gpost_ref[...])


def _combine(geom, y_rows, dest, top_w, x, g_post, mod_l):
    t, d = x.shape
    n_e = top_w.shape[1]
    tc = _tile(geom.row_gcd, COMBINE_ROWS, SMEM_INDEX_ALIGN // TOP_K)
    return pl.pallas_call(
        _combine_kernel,
        grid=(t // tc,),
        in_specs=[
            pl.BlockSpec((TOP_K * tc,), lambda i: (i,), memory_space=pltpu.SMEM),
            pl.BlockSpec(memory_space=pl.ANY),
            pl.BlockSpec((tc, n_e), lambda i: (i, 0)),
            pl.BlockSpec((tc, d), lambda i: (i, 0)),
            _row_spec(d),
            geom.mod_spec(5, tc),
        ],
        out_specs=pl.BlockSpec((tc, d), lambda i: (i, 0)),
        out_shape=jax.ShapeDtypeStruct((t, d), F32),
        scratch_shapes=[pltpu.VMEM((TOP_K, tc // SUBLANES, SUBLANES, d), F32), pltpu.SemaphoreType.DMA(())],
        compiler_params=_params(("arbitrary",)),
        name="moe_combine",
    )(dest, y_rows, top_w, x, g_post, mod_l)


def _route(top_e, n_experts, blk):
    t = top_e.shape[0]
    n_assign = t * TOP_K
    n_blocks = -(-(n_assign + n_experts * (blk - 1)) // blk)
    n_pad = n_blocks * blk
    flat_e = top_e[:, :TOP_K].reshape(-1)
    onehot = (flat_e[:, None] == jnp.arange(n_experts, dtype=jnp.int32)[None, :]).astype(jnp.int32)
    rank = jnp.sum((jnp.cumsum(onehot, axis=0) - onehot) * onehot, axis=1)
    counts = jnp.sum(onehot, axis=0)
    padded = (counts + blk - 1) // blk * blk
    end_pad = jnp.cumsum(padded)
    start_pad = end_pad - padded
    dest = (start_pad[flat_e] + rank).astype(jnp.int32)
    flat_tok = jnp.arange(n_assign, dtype=jnp.int32) // TOP_K
    row_tok = jnp.zeros((n_pad,), jnp.int32).at[dest].set(flat_tok)
    block_e = jnp.minimum(
        jnp.searchsorted(end_pad, jnp.arange(n_blocks, dtype=jnp.int32) * blk, side='right'),
        n_experts - 1).astype(jnp.int32)
    return dest, row_tok, block_e


def kernel(x_prompt, x_sample, c_prompt, c_sample, w_ada, b_ada, g_mix_pre, g_mix_post, g_ffn_pre, g_ffn_post,
           w_in, lam_q1, lam_k1, lam_q2, lam_k2, g_subln, conv_w, w_out, w_ffn_gate, w_ffn_up, w_ffn_down,
           w_router, w_exp_gate, w_exp_up, w_exp_down):
    bp, sp, d = x_prompt.shape
    bs, ss, _ = x_sample.shape
    depth = w_in.shape[0]
    head_dim = lam_q1.shape[1]
    v_dim = g_subln.shape[1]
    n_heads = d // v_dim
    n_experts = w_router.shape[2]
    n_in = w_in.shape[2]
    assert v_dim == 2 * head_dim and v_dim == LANES and n_in == 8 * d
    geom = _Geom(bp, sp, bs, ss, d)

    x = jnp.concatenate([x_prompt.reshape(bp * sp, d), x_sample.reshape(bs * ss, d)], axis=0)
    c_all = jnp.concatenate([c_prompt, c_sample, jnp.zeros((geom.nb_pad - geom.nb, d), F32)], axis=0)
    mod = _ada(c_all, w_ada, b_ada)

    tq = _tile(geom.row_gcd, ATTN_Q_ROWS, ROW_ALIGN)
    chunk = _tile(geom.row_gcd, ATTN_KV_CHUNK, tq)
    attn_consts = _attn_consts(n_heads, head_dim, tq, chunk, max(sp, ss))
    attn_items = _attn_items(geom, tq, chunk)
    col_scale = jnp.asarray(np.where(np.arange(n_in) < n_heads * v_dim, head_dim ** -0.5 * LOG2E, 1.0)
                            .astype(np.float32))[None, :]
    lam_init = np.asarray([0.8 - 0.6 * math.exp(-0.3 * l) for l in range(depth)], np.float32)
    lam_pack = jnp.stack(
        [lam_q1, lam_k1, lam_q2, lam_k2, jnp.broadcast_to(jnp.asarray(lam_init)[:, None], lam_q1.shape)]
        + [jnp.zeros_like(lam_q1)] * 3, axis=1)

    per_layer = dict(mod=mod, g_mix_pre=g_mix_pre[:, None], g_mix_post=g_mix_post[:, None],
                     g_ffn_pre=g_ffn_pre[:, None], g_ffn_post=g_ffn_post[:, None], g_subln=g_subln[:, None],
                     w_in=w_in, lam=lam_pack, conv_w=conv_w, w_out=w_out)

    def mixer(x, p, w_router_l):
        z = _inproj(geom, x, p["g_mix_pre"], p["mod"], p["w_in"].astype(BF16), col_scale)
        attn = _attention(geom, z, attn_consts, attn_items, p["lam"], p["g_subln"], n_heads, head_dim, tq, chunk)
        return _outproj(geom, attn, z, x, p["conv_w"], p["w_out"].astype(BF16), p["g_mix_post"], p["g_ffn_pre"],
                        p["mod"], w_router_l)

    def dense_layer(x, p, w):
        x, h = mixer(x, p, None)
        return _ffn(geom, h, x, w["gate"].astype(BF16), w["up"].astype(BF16), w["down"].astype(BF16),
                    p["g_ffn_post"], p["mod"])

    def expert_layer(x, p, w):
        x, h, top_e, top_w = mixer(x, p, w["router"])
        dest, row_tok, block_e = _route(top_e, n_experts, MOE_BLOCK)
        y_rows = _experts(h, row_tok, block_e, w["gate"].astype(BF16), w["up"].astype(BF16),
                          w["down"].astype(BF16), MOE_BLOCK)
        return _combine(geom, y_rows, dest, top_w, x, p["g_ffn_post"], p["mod"])

    n_pairs = depth // 2
    even = jax.tree.map(lambda a: a[0:2 * n_pairs:2], per_layer)
    odd = jax.tree.map(lambda a: a[1:2 * n_pairs:2], per_layer)
    dense_w = dict(gate=w_ffn_gate, up=w_ffn_up, down=w_ffn_down)
    exp_w = dict(router=w_router, gate=w_exp_gate, up=w_exp_up, down=w_exp_down)

    def pair(x, xs):
        p_even, p_odd, dw, ew = xs
        x = dense_layer(x, p_even, dw)
        x = expert_layer(x, p_odd, ew)
        return x, None

    x, _ = lax.scan(pair, x, (even, odd, jax.tree.map(lambda a: a[:n_pairs], dense_w), exp_w))
    if depth % 2 == 1:
        x = dense_layer(x, jax.tree.map(lambda a: a[depth - 1], per_layer),
                        jax.tree.map(lambda a: a[n_pairs], dense_w))

    y_prompt = x[:geom.tp].reshape(bp, sp, d)
    y_sample = x[geom.tp:].reshape(bs, ss, d)
    return (y_prompt, y_sample)
```

```python
import functools
import math

import numpy as np
import jax
import jax.numpy as jnp
from jax import lax
from jax.experimental import pallas as pl
from jax.experimental.pallas import tpu as pltpu

F32 = jnp.float32
BF16 = jnp.bfloat16

ALIBI_MAX = 8.0
NORM_EPS = 1e-6
TOP_K = 2
N_MOD = 6
LOG2E = math.log2(math.e)

VMEM_LIMIT_BYTES = 56 * 1024 * 1024
ROW_ALIGN = 16
SUBLANES = 8
LANES = 128
BF16_EXACT_INT = 256
SLOPE_PARTS = 3
SKIP_LOG2_GAP = 160.0
SKIP_NORM_SLACK = 1.01

INPROJ_ROWS, INPROJ_COLS = 1024, 2048
ATTN_Q_ROWS, ATTN_KV_CHUNK = 512, 2048
OUTPROJ_ROWS = 256
FFN_ROWS, FFN_COLS = 512, 1408
MOE_BLOCK = 1024
EXPERT_F_STEPS = 4
COMBINE_ROWS = 512
DMA_ISSUE_UNROLL = 8
DMA_PRIORITIES = 2
SMEM_INDEX_ALIGN = 1024


def _tile(n, pref, align=1):
    t = min(n, pref)
    while t > 0:
        if n % t == 0 and t % align == 0:
            return t
        t -= 1
    raise ValueError(f"no tile for {n} (pref {pref}, align {align})")


def _params(sem):
    return pltpu.CompilerParams(dimension_semantics=sem, vmem_limit_bytes=VMEM_LIMIT_BYTES)


def _rms(x):
    return x * lax.rsqrt(jnp.mean(x * x, axis=-1, keepdims=True) + NORM_EPS)


def _row_spec(d, fn=None):
    return pl.BlockSpec((1, d), fn if fn is not None else (lambda *_: (0, 0)))


def _ada_kernel(c_ref, w_ref, b_ref, o_ref):
    c = c_ref[...]
    cond = c * jax.nn.sigmoid(c)
    o_ref[0] = jnp.dot(cond, w_ref[0], preferred_element_type=F32,
                       precision=lax.Precision.HIGHEST) + b_ref[0]


def _ada(c_all, w_ada, b_ada):
    depth, d, _ = w_ada.shape
    bp = c_all.shape[0]
    b3 = b_ada.reshape(depth * N_MOD, 1, d)
    out = pl.pallas_call(
        _ada_kernel,
        grid=(depth, N_MOD),
        in_specs=[
            pl.BlockSpec((bp, d), lambda l, k: (0, 0)),
            pl.BlockSpec((1, d, d), lambda l, k: (l, 0, k)),
            pl.BlockSpec((1, 1, d), lambda l, k: (l * N_MOD + k, 0, 0)),
        ],
        out_specs=pl.BlockSpec((1, bp, d), lambda l, k: (l * N_MOD + k, 0, 0)),
        out_shape=jax.ShapeDtypeStruct((depth * N_MOD, bp, d), F32),
        compiler_params=_params(("arbitrary", "arbitrary")),
        name="ada_mod",
    )(c_all, w_ada, b3)
    return out.reshape(depth, N_MOD * bp, 1, d)


class _Geom:
    def __init__(self, bp, sp, bs, ss, d):
        self.bp, self.sp, self.bs, self.ss, self.d = bp, sp, bs, ss, d
        self.tp = bp * sp
        self.t = self.tp + bs * ss
        self.nb = bp + bs
        self.nb_pad = -(-self.nb // 8) * 8
        self.row_gcd = math.gcd(sp, ss)

    def batch_of_tile(self, i, rows):
        npt = self.tp // rows
        return jnp.where(i < npt, i // (self.sp // rows), self.bp + (i - npt) // (self.ss // rows))

    def mod_spec(self, k, rows):
        return pl.BlockSpec((1, 1, self.d), lambda i, *_: (k * self.nb_pad + self.batch_of_tile(i, rows), 0, 0))


def _inproj_kernel(x_ref, g_ref, scale_ref, shift_ref, w_ref, cs_ref, z_ref, h_scr):
    @pl.when(pl.program_id(1) == 0)
    def _():
        h = _rms(x_ref[...]) * g_ref[...] * (1.0 + scale_ref[0]) + shift_ref[0]
        h_scr[...] = h.astype(BF16)

    z = (jnp.dot(h_scr[...], w_ref[...], preferred_element_type=F32) * cs_ref[...]).astype(BF16)
    for s in range(z_ref.shape[0]):
        z_ref[s] = z[:, s * LANES:(s + 1) * LANES]


def _inproj(geom, x, g_pre, mod_l, w_in_l, col_scale):
    t, d = x.shape
    n = w_in_l.shape[1]
    tm = _tile(geom.row_gcd, INPROJ_ROWS, ROW_ALIGN)
    tn = _tile(n, INPROJ_COLS, LANES)
    return pl.pallas_call(
        _inproj_kernel,
        grid=(t // tm, n // tn),
        in_specs=[
            pl.BlockSpec((tm, d), lambda i, j: (i, 0)),
            _row_spec(d),
            geom.mod_spec(1, tm),
            geom.mod_spec(0, tm),
            pl.BlockSpec((d, tn), lambda i, j: (0, j)),
            pl.BlockSpec((1, tn), lambda i, j: (0, j)),
        ],
        out_specs=pl.BlockSpec((tn // LANES, tm, LANES), lambda i, j: (j, i, 0)),
        out_shape=jax.ShapeDtypeStruct((n // LANES, t, LANES), BF16),
        scratch_shapes=[pltpu.VMEM((tm, d), BF16)],
        compiler_params=_params(("arbitrary", "arbitrary")),
        name="in_proj",
    )(x, g_pre, mod_l, mod_l, w_in_l, col_scale)


def _attn_scores(s_ref, it, q_ref, k_ref, qc_ref, kc_ref, bd_ref, ed0_ref, d0_ref, dc_ref, head_dim):
    _, tq, v_dim = q_ref.shape
    n_sub = k_ref.shape[1] // tq
    q = q_ref[0]
    lane = lax.broadcasted_iota(jnp.int32, (tq, v_dim), 1)
    half = (lane < head_dim, lane >= head_dim)
    d0 = d0_ref[it]
    d0_v = jnp.full((1, v_dim), d0, jnp.int32).astype(F32)
    zero = jnp.zeros_like(q)
    for m in range(2):
        q_aug = jnp.where(half[m], q, zero) + qc_ref[0, m] + (d0_v * ed0_ref[m]).astype(BF16)
        for c in range(n_sub):
            delta = d0 - c * tq
            sgn = (delta > 0).astype(jnp.int32) - (delta < 0).astype(jnp.int32)
            sgn_v = jnp.full((1, v_dim), sgn, jnp.int32).astype(F32).astype(BF16)
            k_aug = jnp.where(half[m], k_ref[0, c * tq:(c + 1) * tq, :], sgn_v * kc_ref[0, m, c * tq:(c + 1) * tq, :])
            s_ref[m, :, c * tq:(c + 1) * tq] = lax.dot_general(
                q_aug, k_aug, (((1,), (1,)), ((), ())), preferred_element_type=F32)
    dc = dc_ref[it]
    flag = jnp.full((1, 1), (dc >= 0).astype(jnp.int32), jnp.int32).astype(F32)
    off = pl.multiple_of(jnp.maximum(dc, 0) * tq, tq)
    bias = flag * bd_ref[0]
    for m in range(2):
        s_ref[m, :, pl.ds(off, tq)] = s_ref[m, :, pl.ds(off, tq)] - bias


def _attn_softmax_pv(s_ref, is_first, v_ref, m_scr, acc_scr):
    _, chunk, v_dim = v_ref.shape
    ones_col = jnp.where(lax.broadcasted_iota(jnp.int32, (chunk, v_dim), 1) == 0, 1.0, 0.0).astype(BF16)
    v_aug = jnp.concatenate([v_ref[0], ones_col], axis=1)
    for m in range(2):
        s = s_ref[m]
        m_old = jnp.where(is_first, -jnp.inf, m_scr[m])
        acc_old = jnp.where(is_first, 0.0, acc_scr[m])
        m_new = jnp.maximum(m_old, jnp.max(s, axis=-1, keepdims=True))
        p = jnp.exp2(s - m_new).astype(BF16)
        alpha = jnp.exp2(m_old - m_new)
        acc_scr[m] = alpha * acc_old + jnp.dot(p, v_aug, preferred_element_type=F32)
        m_scr[m] = m_new


def _attn_kernel(head_ref, qblk_ref, kvblk_ref, first_ref, last_ref, d0_ref, dc_ref, nk_ref,
                 q_ref, k_ref, v_ref, qc_ref, kc_ref, bd_ref, ed0_ref, lam_ref, gsub_ref, o_ref,
                 s_even, s_odd, m_scr, acc_scr, *, head_dim):
    g = pl.program_id(0)
    n_kept = nk_ref[0]
    e_a = jnp.minimum(g, n_kept - 1)
    e_b = jnp.minimum(jnp.maximum(g - 1, 0), n_kept - 1)
    active = g <= n_kept
    v_dim = q_ref.shape[2]

    @pl.when(g == 0)
    def _():
        s_odd[...] = jnp.zeros(s_odd.shape, F32)
        m_scr[...] = jnp.zeros(m_scr.shape, F32)
        acc_scr[...] = jnp.zeros(acc_scr.shape, F32)

    is_first = jnp.full((1, 1), first_ref[e_b], jnp.int32) > 0

    def step(s_write, s_read):
        _attn_scores(s_write, e_a, q_ref, k_ref, qc_ref, kc_ref, bd_ref, ed0_ref, d0_ref, dc_ref, head_dim)
        _attn_softmax_pv(s_read, is_first, v_ref, m_scr, acc_scr)

    @pl.when(jnp.logical_and(active, g % 2 == 0))
    def _():
        step(s_even, s_odd)

    @pl.when(jnp.logical_and(active, g % 2 == 1))
    def _():
        step(s_odd, s_even)

    @pl.when(jnp.logical_and(active, last_ref[e_b] == 1))
    def _():
        lp = lam_ref[...]
        lam_init = lp[4:5, :1]
        lam = (jnp.exp(jnp.sum(lp[0:1] * lp[1:2], axis=-1, keepdims=True))
               - jnp.exp(jnp.sum(lp[2:3] * lp[3:4], axis=-1, keepdims=True)) + lam_init)
        a0 = acc_scr[0]
        a1 = acc_scr[1]
        o = a0[:, :v_dim] / a0[:, v_dim:v_dim + 1] - lam * (a1[:, :v_dim] / a1[:, v_dim:v_dim + 1])
        o_ref[0] = (_rms(o) * gsub_ref[...] * (1.0 - lam_init)).astype(o_ref.dtype)


def _attn_items(geom, tq, chunk):
    qblk, kvblk, d0s, dcs, dist = [], [], [], [], []
    for off, nb, s in ((0, geom.bp, geom.sp), (geom.tp, geom.bs, geom.ss)):
        for b in range(nb):
            for qt in range(s // tq):
                for c in range(s // chunk):
                    qblk.append((off + b * s + qt * tq) // tq)
                    kvblk.append((off + b * s + c * chunk) // chunk)
                    d0 = qt * tq - c * chunk
                    d0s.append(d0)
                    dcs.append(d0 // tq if 0 <= d0 < chunk else -1)
                    dist.append(0 if 0 <= d0 < chunk else (d0 - chunk + 1 if d0 > 0 else -d0 - tq + 1))
    names = ("qblk", "kvblk", "d0", "dc", "min_dist")
    return {n: np.asarray(a, np.int32) for n, a in zip(names, (qblk, kvblk, d0s, dcs, dist))}


def _attn_stats_kernel(q_ref, k_ref, o_ref, *, head_dim, tq):
    v_dim = q_ref.shape[2]
    r = lax.broadcasted_iota(jnp.int32, (v_dim, v_dim), 0)
    c = lax.broadcasted_iota(jnp.int32, (v_dim, v_dim), 1)
    sel = jnp.where(jnp.logical_or(jnp.logical_and(c == 0, r < head_dim), jnp.logical_and(c == 1, r >= head_dim)),
                    1.0, 0.0).astype(BF16)
    pad = jnp.zeros((o_ref.shape[2] - 3, v_dim), F32)
    for t in range(q_ref.shape[1] // tq):
        q = q_ref[0, t * tq:(t + 1) * tq, :].astype(F32)
        k = k_ref[0, t * tq:(t + 1) * tq, :].astype(F32)
        out_rows = []
        for x, reduce_rows in ((q * q, jnp.max), (k * k, jnp.max), (q * k, jnp.min)):
            per_row = jnp.dot(x.astype(BF16), sel, preferred_element_type=F32)
            out_rows.append(reduce_rows(per_row, axis=0, keepdims=True))
        o_ref[0, t] = jnp.concatenate(out_rows + [pad], axis=0)


def _attn_stats(qkv, n_heads, head_dim, tq, rows_per_step):
    t, v_dim = qkv.shape[1], qkv.shape[2]
    tiles = rows_per_step // tq
    return pl.pallas_call(
        functools.partial(_attn_stats_kernel, head_dim=head_dim, tq=tq),
        grid=(n_heads, t // rows_per_step),
        in_specs=[
            pl.BlockSpec((1, rows_per_step, v_dim), lambda h, i: (h, i, 0)),
            pl.BlockSpec((1, rows_per_step, v_dim), lambda h, i: (n_heads + h, i, 0)),
        ],
        out_specs=pl.BlockSpec((1, tiles, SUBLANES, LANES), lambda h, i: (h, i, 0, 0)),
        out_shape=jax.ShapeDtypeStruct((n_heads, t // tq, SUBLANES, LANES), F32),
        compiler_params=_params(("arbitrary", "arbitrary")),
        name="attn_stats",
    )(qkv, qkv)


def _attn_entries(items, stats, slopes_log2, n_sub):
    n_heads = stats.shape[0]
    n_items = items["qblk"].shape[0]
    q_norm = jnp.sqrt(stats[:, :, 0, 0:2])
    k_norm = jnp.sqrt(stats[:, :, 1, 0:2])
    self_min = stats[:, :, 2, 0:2]
    kv_tiles = items["kvblk"][:, None] * n_sub + np.arange(n_sub)[None, :]
    k_norm_item = jnp.max(k_norm[:, kv_tiles, :], axis=2)
    upper = (q_norm[:, items["qblk"], :] * k_norm_item * SKIP_NORM_SLACK
             - slopes_log2[:, None, None] * items["min_dist"][None, :, None].astype(np.float32))
    gap = upper - self_min[:, items["qblk"], :]
    drop = jnp.logical_and(jnp.all(gap < -SKIP_LOG2_GAP, axis=-1), (items["min_dist"] > 0)[None, :])
    keep = jnp.logical_not(drop).reshape(-1)
    n_all = n_heads * n_items
    order = jnp.argsort(jnp.logical_not(keep), stable=True).astype(jnp.int32)
    n_kept = jnp.sum(keep.astype(jnp.int32))
    pos = jnp.arange(n_all, dtype=jnp.int32)
    src = order[jnp.minimum(pos, n_kept - 1)]
    head = src // n_items
    it = src % n_items
    qblk = jnp.asarray(items["qblk"])[it]
    tile_id = head * (int(items["qblk"].max()) + 1) + qblk
    first = jnp.logical_or(pos == 0, tile_id != jnp.roll(tile_id, 1))
    last = jnp.logical_or(pos >= n_kept - 1, tile_id != jnp.roll(tile_id, -1))
    i32 = lambda a: a.astype(jnp.int32)
    return (i32(head), qblk, jnp.asarray(items["kvblk"])[it], i32(first), i32(last),
            jnp.asarray(items["d0"])[it], jnp.asarray(items["dc"])[it], n_kept.reshape(1))


def _bf16_parts(x, n):
    parts, rest = [], np.asarray(x, np.float64)
    for _ in range(n):
        p = rest.astype(BF16).astype(np.float64)
        parts.append(p)
        rest = rest - p
    return parts


def _attn_consts(n_heads, head_dim, tq, chunk, max_d0):
    v_dim = 2 * head_dim
    slopes = 2.0 ** (-ALIBI_MAX * np.arange(1, n_heads + 1, dtype=np.float64) / n_heads) * LOG2E
    sl = _bf16_parts(slopes, SLOPE_PARTS)
    r = np.arange(tq)
    j = np.arange(chunk)
    i_parts = [None, (r // BF16_EXACT_INT) * BF16_EXACT_INT, r % BF16_EXACT_INT]
    j_parts = [(j // BF16_EXACT_INT) * BF16_EXACT_INT, j % BF16_EXACT_INT]
    assert chunk <= BF16_EXACT_INT ** 2 and max_d0 % BF16_EXACT_INT == 0 and max_d0 <= BF16_EXACT_INT ** 2
    n_cols = SLOPE_PARTS * (len(i_parts) + len(j_parts))
    assert n_cols <= head_dim
    qc = np.zeros((n_heads, 2, tq, v_dim), np.float64)
    kc = np.zeros((n_heads, 2, chunk, v_dim), np.float64)
    ed0 = np.zeros((2, 1, v_dim), np.float32)
    for m in range(2):
        base = head_dim if m == 0 else 0
        col = base
        for a in range(SLOPE_PARTS):
            for jp in j_parts:
                qc[:, m, :, col] = sl[a][:, None]
                kc[:, m, :, col] = jp[None, :]
                col += 1
            for ip in i_parts:
                if ip is None:
                    ed0[m, 0, col] = 1.0
                else:
                    qc[:, m, :, col] = ip[None, :]
                kc[:, m, :, col] = -sl[a][:, None]
                col += 1
    bd = slopes[:, None, None] * np.abs(r[:, None] - r[None, :])[None]
    for arr in (qc, kc):
        assert np.array_equal(arr.astype(BF16).astype(np.float64), arr)
    return (jnp.asarray(qc.astype(BF16)), jnp.asarray(kc.astype(BF16)),
            jnp.asarray(bd.astype(np.float32)), jnp.asarray(ed0), jnp.asarray(slopes.astype(np.float32)))


def _attention(geom, qkv, consts, items, lam_pack_l, g_subln_l, n_heads, head_dim, tq, chunk):
    t = qkv.shape[1]
    v_dim = 2 * head_dim
    qc, kc, bd, ed0, slopes_log2 = consts
    n_sub = chunk // tq
    entries = _attn_entries(items, _attn_stats(qkv, n_heads, head_dim, tq, chunk), slopes_log2, n_sub)
    n_entries = entries[0].shape[0]
    k_head0 = n_heads
    v_head0 = 2 * n_heads

    def score_entry(g, nk):
        return jnp.minimum(g, nk[0] - 1)

    def finish_entry(g, nk):
        return jnp.minimum(jnp.maximum(g - 1, 0), nk[0] - 1)

    def q_map(g, hd, qb, kb, fi, la, d0, dc, nk):
        e = score_entry(g, nk)
        return hd[e], qb[e], 0

    def k_map(g, hd, qb, kb, fi, la, d0, dc, nk):
        e = score_entry(g, nk)
        return k_head0 + hd[e], kb[e], 0

    def v_map(g, hd, qb, kb, fi, la, d0, dc, nk):
        e = finish_entry(g, nk)
        return v_head0 + hd[e], kb[e], 0

    def o_map(g, hd, qb, kb, fi, la, d0, dc, nk):
        e = finish_entry(g, nk)
        return hd[e], qb[e], 0

    def head_map(n_trailing):
        def fn(g, hd, qb, kb, fi, la, d0, dc, nk):
            return (hd[score_entry(g, nk)],) + (0,) * n_trailing
        return fn

    grid_spec = pltpu.PrefetchScalarGridSpec(
        num_scalar_prefetch=8,
        grid=(n_entries + 1,),
        in_specs=[
            pl.BlockSpec((1, tq, v_dim), q_map),
            pl.BlockSpec((1, chunk, v_dim), k_map),
            pl.BlockSpec((1, chunk, v_dim), v_map),
            pl.BlockSpec((1, 2, tq, v_dim), head_map(3)),
            pl.BlockSpec((1, 2, chunk, v_dim), head_map(3)),
            pl.BlockSpec((1, tq, tq), head_map(2)),
            pl.BlockSpec((2, 1, v_dim), lambda g, *_: (0, 0, 0)),
            pl.BlockSpec((8, head_dim), lambda g, *_: (0, 0)),
            _row_spec(v_dim),
        ],
        out_specs=pl.BlockSpec((1, tq, v_dim), o_map),
        scratch_shapes=[
            pltpu.VMEM((2, tq, chunk), F32),
            pltpu.VMEM((2, tq, chunk), F32),
            pltpu.VMEM((2, tq, 1), F32),
            pltpu.VMEM((2, tq, 2 * v_dim), F32),
        ],
    )
    return pl.pallas_call(
        functools.partial(_attn_kernel, head_dim=head_dim),
        grid_spec=grid_spec,
        out_shape=jax.ShapeDtypeStruct((n_heads, t, v_dim), BF16),
        compiler_params=_params(("arbitrary",)),
        name="diff_attn",
    )(*entries, qkv, qkv, qkv, qc, kc, bd, ed0, lam_pack_l, g_subln_l)


def _outproj_kernel(attn_ref, cb_ref, cc_ref, cx_ref, ga_ref, gc_ref,
                    ccp_ref, cxp_ref, ccn_ref, cxn_ref,
                    x_ref, convw_ref, wout_ref, gpost_ref, gatem_ref, gpre_ref, scalef_ref, shiftf_ref,
                    *rest, geom, moe):
    if moe:
        wr_ref, xo_ref, h_ref, tope_ref, topw_ref = rest
    else:
        xo_ref, h_ref = rest
    tm = x_ref.shape[0]
    row0 = pl.program_id(0) * tm
    seq = jnp.where(row0 < geom.tp, geom.sp, geom.ss)
    rel0 = jnp.where(row0 < geom.tp, row0, row0 - geom.tp)
    not_start = (rel0 % seq != 0).astype(F32)
    not_end = ((rel0 + tm) % seq != 0).astype(F32)

    def rows(ref, lo=0, hi=None):
        hi = ref.shape[1] if hi is None else hi
        return jnp.concatenate([ref[s, lo:hi, :] for s in range(ref.shape[0])], axis=1).astype(F32)

    u = rows(cc_ref) * rows(cx_ref)
    halo = ccp_ref.shape[1]
    u_prev = (rows(ccp_ref, halo - 1, halo) * rows(cxp_ref, halo - 1, halo)) * not_start
    u_next = (rows(ccn_ref, 0, 1) * rows(cxn_ref, 0, 1)) * not_end
    row = lax.broadcasted_iota(jnp.int32, u.shape, 0)
    u_m1 = jnp.where(row == 0, u_prev, pltpu.roll(u, 1, 0))
    u_p1 = jnp.where(row == tm - 1, u_next, pltpu.roll(u, tm - 1, 0))
    cw = convw_ref[...]
    conv = u_m1 * cw[0:1] + u * cw[1:2] + u_p1 * cw[2:3]
    short = rows(cb_ref) * conv
    merged = jax.nn.sigmoid(rows(ga_ref)) * rows(attn_ref) + jax.nn.sigmoid(rows(gc_ref)) * short
    o = jnp.dot(merged.astype(BF16), wout_ref[...], preferred_element_type=F32)
    x_new = x_ref[...] + gatem_ref[0] * (_rms(o) * gpost_ref[...])
    xo_ref[...] = x_new
    h = _rms(x_new) * gpre_ref[...] * (1.0 + scalef_ref[0]) + shiftf_ref[0]
    h_ref[...] = h.astype(h_ref.dtype)

    if moe:
        n_e = wr_ref.shape[1]
        wr = wr_ref[...]
        h_hi = h.astype(BF16)
        h_lo = (h - h_hi.astype(F32)).astype(BF16)
        w_hi = wr.astype(BF16)
        w_lo = (wr - w_hi.astype(F32)).astype(BF16)
        logits = (jnp.dot(h_hi, w_hi, preferred_element_type=F32) + jnp.dot(h_lo, w_hi, preferred_element_type=F32)
                  + jnp.dot(h_hi, w_lo, preferred_element_type=F32))
        lane = lax.broadcasted_iota(jnp.int32, logits.shape, 1).astype(F32)
        m1 = jnp.max(logits, axis=-1, keepdims=True)
        i1 = jnp.min(jnp.where(logits == m1, lane, float(n_e)), axis=-1, keepdims=True)
        rest_l = jnp.where(lane == i1, -jnp.inf, logits)
        m2 = jnp.max(rest_l, axis=-1, keepdims=True)
        i2 = jnp.min(jnp.where(rest_l == m2, lane, float(n_e)), axis=-1, keepdims=True)
        e = jnp.exp(m2 - m1)
        w1 = 1.0 / (1.0 + e)
        w2 = e / (1.0 + e)
        tope_ref[...] = jnp.where(lane == 0.0, i1, jnp.where(lane == 1.0, i2, 0.0)).astype(jnp.int32)
        topw_ref[...] = jnp.where(lane == 0.0, w1, jnp.where(lane == 1.0, w2, 0.0))


def _outproj(geom, attn, z, x, conv_w_l, w_out_l, g_post, g_pre, mod_l, w_router_l):
    t, d = x.shape
    moe = w_router_l is not None
    tm = _tile(geom.row_gcd, OUTPROJ_ROWS, ROW_ALIGN)
    halo = ROW_ALIGN
    nh = tm // halo
    last_halo = t // halo - 1
    n_heads, _, v_dim = attn.shape
    spt = d // LANES
    zcol = lambda c: pl.BlockSpec((spt, tm, LANES), lambda i: (c, i, 0))
    zhalo = lambda c, fn: pl.BlockSpec((spt, halo, LANES), lambda i: (c, fn(i), 0))
    prev_blk = lambda i: jnp.maximum(i * nh - 1, 0)
    next_blk = lambda i: jnp.minimum((i + 1) * nh, last_halo)
    in_specs = [
        pl.BlockSpec((n_heads, tm, v_dim), lambda i: (0, i, 0)),
        zcol(3), zcol(4), zcol(5), zcol(6), zcol(7),
        zhalo(4, prev_blk), zhalo(5, prev_blk), zhalo(4, next_blk), zhalo(5, next_blk),
        pl.BlockSpec((tm, d), lambda i: (i, 0)),
        pl.BlockSpec(conv_w_l.shape, lambda i: (0, 0)),
        pl.BlockSpec((d, d), lambda i: (0, 0)),
        _row_spec(d),
        geom.mod_spec(2, tm),
        _row_spec(d),
        geom.mod_spec(4, tm), geom.mod_spec(3, tm),
    ]
    args = [attn, z, z, z, z, z, z, z, z, z, x, conv_w_l, w_out_l, g_post, mod_l, g_pre, mod_l, mod_l]
    out_specs = [pl.BlockSpec((tm, d), lambda i: (i, 0)), pl.BlockSpec((tm, d), lambda i: (i, 0))]
    out_shape = [jax.ShapeDtypeStruct((t, d), F32), jax.ShapeDtypeStruct((t, d), F32 if moe else BF16)]
    if moe:
        n_e = w_router_l.shape[1]
        in_specs.append(pl.BlockSpec((d, n_e), lambda i: (0, 0)))
        args.append(w_router_l)
        out_specs += [pl.BlockSpec((tm, n_e), lambda i: (i, 0))] * 2
        out_shape += [jax.ShapeDtypeStruct((t, n_e), jnp.int32), jax.ShapeDtypeStruct((t, n_e), F32)]
    return pl.pallas_call(
        functools.partial(_outproj_kernel, geom=geom, moe=moe),
        grid=(t // tm,),
        in_specs=in_specs,
        out_specs=out_specs,
        out_shape=out_shape,
        compiler_params=_params(("arbitrary",)),
        name="out_proj_moe" if moe else "out_proj",
    )(*args)


def _swiglu_partial(h, wg, wu, wd):
    g = jnp.dot(h, wg, preferred_element_type=F32)
    u = jnp.dot(h, wu, preferred_element_type=F32)
    a = (g * jax.nn.sigmoid(g)) * u
    return jnp.dot(a.astype(BF16), wd, preferred_element_type=F32)


def _ffn_kernel(h_ref, x_ref, wg_ref, wu_ref, wd_ref, gpost_ref, gate_ref, xo_ref, acc_scr):
    f = pl.program_id(1)
    part = _swiglu_partial(h_ref[...], wg_ref[...], wu_ref[...], wd_ref[...])

    @pl.when(f == 0)
    def _():
        acc_scr[...] = part

    @pl.when(f > 0)
    def _():
        acc_scr[...] += part

    @pl.when(f == pl.num_programs(1) - 1)
    def _():
        xo_ref[...] = x_ref[...] + gate_ref[0] * (_rms(acc_scr[...]) * gpost_ref[...])


def _ffn(geom, h, x, wg, wu, wd, g_post, mod_l):
    t, d = x.shape
    ff = wg.shape[1]
    tm = _tile(geom.row_gcd, FFN_ROWS, ROW_ALIGN)
    tf = _tile(ff, FFN_COLS, LANES)
    return pl.pallas_call(
        _ffn_kernel,
        grid=(t // tm, ff // tf),
        in_specs=[
            pl.BlockSpec((tm, d), lambda i, f: (i, 0)),
            pl.BlockSpec((tm, d), lambda i, f: (i, 0)),
            pl.BlockSpec((d, tf), lambda i, f: (0, f)),
            pl.BlockSpec((d, tf), lambda i, f: (0, f)),
            pl.BlockSpec((tf, d), lambda i, f: (f, 0)),
            _row_spec(d),
            geom.mod_spec(5, tm),
        ],
        out_specs=pl.BlockSpec((tm, d), lambda i, f: (i, 0)),
        out_shape=jax.ShapeDtypeStruct((t, d), F32),
        scratch_shapes=[pltpu.VMEM((tm, d), F32)],
        compiler_params=_params(("arbitrary", "arbitrary")),
        name="dense_ffn",
    )(h, x, wg, wu, wd, g_post, mod_l)


def _expert_kernel(be_ref, idx0_ref, idxn_ref, h_ref, wg_ref, wu_ref, wd_ref, y_ref,
                   xf_scr, xb_scr, sem):
    b = pl.program_id(0)
    f = pl.program_id(1)
    n_blk = pl.num_programs(0)
    n_f = pl.num_programs(1)
    blk, d = xb_scr.shape
    slot = b % 2

    def row_copy(idx_ref, r, s, tile, sub):
        return pltpu.make_async_copy(h_ref.at[pl.ds(idx_ref[r], 1)], xf_scr.at[s, tile, pl.ds(sub, 1)], sem.at[s])

    def block_copy(s):
        return pltpu.make_async_copy(xf_scr.at[s], xf_scr.at[s], sem.at[s])

    @pl.when(jnp.logical_and(b == 0, f == 0))
    def _():
        def issue(r, carry):
            row_copy(idx0_ref, r, 0, r // SUBLANES, r % SUBLANES).start()
            return carry

        lax.fori_loop(0, blk, issue, 0, unroll=DMA_ISSUE_UNROLL)

    @pl.when(f == 0)
    def _():
        block_copy(slot).wait()
        xb_scr[...] = xf_scr[slot].reshape(blk, d).astype(BF16)

    def prefetch_slice():
        per_step = blk // EXPERT_F_STEPS
        for u in range(per_step):
            row_copy(idxn_ref, f * per_step + u, 1 - slot,
                     f * (per_step // SUBLANES) + u // SUBLANES, u % SUBLANES).start(priority=u % DMA_PRIORITIES)

    part = _swiglu_partial(xb_scr[...], wg_ref[0], wu_ref[0], wd_ref[0])

    @pl.when(f == 0)
    def _():
        prefetch_slice()
        y_ref[...] = part

    @pl.when(f > 0)
    def _():
        prefetch_slice()
        y_ref[...] += part

    @pl.when(jnp.logical_and(b == n_blk - 1, f == n_f - 1))
    def _():
        block_copy(1 - slot).wait()


def _experts(h, row_tok, block_e, wg, wu, wd, blk):
    n_pad = row_tok.shape[0]
    d = h.shape[1]
    ff = wg.shape[2]
    tf = ff // EXPERT_F_STEPS
    n_blk = n_pad // blk
    assert ff % EXPERT_F_STEPS == 0 and tf % LANES == 0 and blk % SMEM_INDEX_ALIGN == 0
    assert blk % (EXPERT_F_STEPS * SUBLANES) == 0 and h.dtype == F32
    grid_spec = pltpu.PrefetchScalarGridSpec(
        num_scalar_prefetch=1,
        grid=(n_blk, EXPERT_F_STEPS),
        in_specs=[
            pl.BlockSpec((blk,), lambda b, f, be: (0,), memory_space=pltpu.SMEM),
            pl.BlockSpec((blk,), lambda b, f, be: (jnp.minimum(b + 1, n_blk - 1),), memory_space=pltpu.SMEM),
            pl.BlockSpec(memory_space=pl.ANY),
            pl.BlockSpec((1, d, tf), lambda b, f, be: (be[b], 0, f)),
            pl.BlockSpec((1, d, tf), lambda b, f, be: (be[b], 0, f)),
            pl.BlockSpec((1, tf, d), lambda b, f, be: (be[b], f, 0)),
        ],
        out_specs=pl.BlockSpec((blk, d), lambda b, f, be: (b, 0)),
        scratch_shapes=[
            pltpu.VMEM((2, blk // SUBLANES, SUBLANES, d), F32),
            pltpu.VMEM((blk, d), BF16),
            pltpu.SemaphoreType.DMA((2,)),
        ],
    )
    return pl.pallas_call(
        _expert_kernel,
        grid_spec=grid_spec,
        out_shape=jax.ShapeDtypeStruct((n_pad, d), F32),
        compiler_params=_params(("arbitrary", "arbitrary")),
        name="expert_ffn",
    )(block_e, row_tok, row_tok, h, wg, wu, wd)


def _combine_kernel(dest_ref, y_ref, w_ref, x_ref, gpost_ref, gate_ref, xo_ref, buf, sem):
    tc, d = x_ref.shape

    def issue(tile, carry):
        for sub in range(SUBLANES):
            for k in range(TOP_K):
                src_row = dest_ref[TOP_K * (tile * SUBLANES + sub) + k]
                pltpu.make_async_copy(y_ref.at[pl.ds(src_row, 1)], buf.at[k, tile, pl.ds(sub, 1)],
                                      sem).start(priority=k % DMA_PRIORITIES)
        return carry

    lax.fori_loop(0, tc // SUBLANES, issue, 0)
    pltpu.make_async_copy(buf, buf, sem).wait()
    w = w_ref[...]
    o = w[:, 0:1] * buf[0].reshape(tc, d)
    for k in range(1, TOP_K):
        o = o + w[:, k:k + 1] * buf[k].reshape(tc, d)
    xo_ref[...] = x_ref[...] + gate_ref[0] * (_rms(o) * gpost_ref[...])


def _combine(geom, y_rows, dest, top_w, x, g_post, mod_l):
    t, d = x.shape
    n_e = top_w.shape[1]
    tc = _tile(geom.row_gcd, COMBINE_ROWS, SMEM_INDEX_ALIGN // TOP_K)
    return pl.pallas_call(
        _combine_kernel,
        grid=(t // tc,),
        in_specs=[
            pl.BlockSpec((TOP_K * tc,), lambda i: (i,), memory_space=pltpu.SMEM),
            pl.BlockSpec(memory_space=pl.ANY),
            pl.BlockSpec((tc, n_e), lambda i: (i, 0)),
            pl.BlockSpec((tc, d), lambda i: (i, 0)),
            _row_spec(d),
            geom.mod_spec(5, tc),
        ],
        out_specs=pl.BlockSpec((tc, d), lambda i: (i, 0)),
        out_shape=jax.ShapeDtypeStruct((t, d), F32),
        scratch_shapes=[pltpu.VMEM((TOP_K, tc // SUBLANES, SUBLANES, d), F32), pltpu.SemaphoreType.DMA(())],
        compiler_params=_params(("arbitrary",)),
        name="moe_combine",
    )(dest, y_rows, top_w, x, g_post, mod_l)


def _route(top_e, n_experts, blk):
    t = top_e.shape[0]
    n_assign = t * TOP_K
    n_blocks = -(-(n_assign + n_experts * (blk - 1)) // blk)
    n_pad = n_blocks * blk
    flat_e = top_e[:, :TOP_K].reshape(-1)
    onehot = (flat_e[:, None] == jnp.arange(n_experts, dtype=jnp.int32)[None, :]).astype(jnp.int32)
    rank = jnp.sum((jnp.cumsum(onehot, axis=0) - onehot) * onehot, axis=1)
    counts = jnp.sum(onehot, axis=0)
    padded = (counts + blk - 1) // blk * blk
    end_pad = jnp.cumsum(padded)
    start_pad = end_pad - padded
    dest = (start_pad[flat_e] + rank).astype(jnp.int32)
    flat_tok = jnp.arange(n_assign, dtype=jnp.int32) // TOP_K
    row_tok = jnp.zeros((n_pad,), jnp.int32).at[dest].set(flat_tok)
    block_e = jnp.minimum(
        jnp.searchsorted(end_pad, jnp.arange(n_blocks, dtype=jnp.int32) * blk, side='right'),
        n_experts - 1).astype(jnp.int32)
    return dest, row_tok, block_e


def kernel(x_prompt, x_sample, c_prompt, c_sample, w_ada, b_ada, g_mix_pre, g_mix_post, g_ffn_pre, g_ffn_post,
           w_in, lam_q1, lam_k1, lam_q2, lam_k2, g_subln, conv_w, w_out, w_ffn_gate, w_ffn_up, w_ffn_down,
           w_router, w_exp_gate, w_exp_up, w_exp_down):
    bp, sp, d = x_prompt.shape
    bs, ss, _ = x_sample.shape
    depth = w_in.shape[0]
    head_dim = lam_q1.shape[1]
    v_dim = g_subln.shape[1]
    n_heads = d // v_dim
    n_experts = w_router.shape[2]
    n_in = w_in.shape[2]
    assert v_dim == 2 * head_dim and v_dim == LANES and n_in == 8 * d
    geom = _Geom(bp, sp, bs, ss, d)

    x = jnp.concatenate([x_prompt.reshape(bp * sp, d), x_sample.reshape(bs * ss, d)], axis=0)
    c_all = jnp.concatenate([c_prompt, c_sample, jnp.zeros((geom.nb_pad - geom.nb, d), F32)], axis=0)
    mod = _ada(c_all, w_ada, b_ada)

    tq = _tile(geom.row_gcd, ATTN_Q_ROWS, ROW_ALIGN)
    chunk = _tile(geom.row_gcd, ATTN_KV_CHUNK, tq)
    attn_consts = _attn_consts(n_heads, head_dim, tq, chunk, max(sp, ss))
    attn_items = _attn_items(geom, tq, chunk)
    col_scale = jnp.asarray(np.where(np.arange(n_in) < n_heads * v_dim, head_dim ** -0.5 * LOG2E, 1.0)
                            .astype(np.float32))[None, :]
    lam_init = np.asarray([0.8 - 0.6 * math.exp(-0.3 * l) for l in range(depth)], np.float32)
    lam_pack = jnp.stack(
        [lam_q1, lam_k1, lam_q2, lam_k2, jnp.broadcast_to(jnp.asarray(lam_init)[:, None], lam_q1.shape)]
        + [jnp.zeros_like(lam_q1)] * 3, axis=1)

    per_layer = dict(mod=mod, g_mix_pre=g_mix_pre[:, None], g_mix_post=g_mix_post[:, None],
                     g_ffn_pre=g_ffn_pre[:, None], g_ffn_post=g_ffn_post[:, None], g_subln=g_subln[:, None],
                     w_in=w_in, lam=lam_pack, conv_w=conv_w, w_out=w_out)

    def mixer(x, p, w_router_l):
        z = _inproj(geom, x, p["g_mix_pre"], p["mod"], p["w_in"].astype(BF16), col_scale)
        attn = _attention(geom, z, attn_consts, attn_items, p["lam"], p["g_subln"], n_heads, head_dim, tq, chunk)
        return _outproj(geom, attn, z, x, p["conv_w"], p["w_out"].astype(BF16), p["g_mix_post"], p["g_ffn_pre"],
                        p["mod"], w_router_l)

    def dense_layer(x, p, w):
        x, h = mixer(x, p, None)
        return _ffn(geom, h, x, w["gate"].astype(BF16), w["up"].astype(BF16), w["down"].astype(BF16),
                    p["g_ffn_post"], p["mod"])

    def expert_layer(x, p, w):
        x, h, top_e, top_w = mixer(x, p, w["router"])
        dest, row_tok, block_e = _route(top_e, n_experts, MOE_BLOCK)
        y_rows = _experts(h, row_tok, block_e, w["gate"].astype(BF16), w["up"].astype(BF16),
                          w["down"].astype(BF16), MOE_BLOCK)
        return _combine(geom, y_rows, dest, top_w, x, p["g_ffn_post"], p["mod"])

    n_pairs = depth // 2
    even = jax.tree.map(lambda a: a[0:2 * n_pairs:2], per_layer)
    odd = jax.tree.map(lambda a: a[1:2 * n_pairs:2], per_layer)
    dense_w = dict(gate=w_ffn_gate, up=w_ffn_up, down=w_ffn_down)
    exp_w = dict(router=w_router, gate=w_exp_gate, up=w_exp_up, down=w_exp_down)

    def pair(x, xs):
        p_even, p_odd, dw, ew = xs
        x = dense_layer(x, p_even, dw)
        x = expert_layer(x, p_odd, ew)
        return x, None

    x, _ = lax.scan(pair, x, (even, odd, jax.tree.map(lambda a: a[:n_pairs], dense_w), exp_w))
    if depth % 2 == 1:
        x = dense_layer(x, jax.tree.map(lambda a: a[depth - 1], per_layer),
                        jax.tree.map(lambda a: a[n_pairs], dense_w))

    y_prompt = x[:geom.tp].reshape(bp, sp, d)
    y_sample = x[geom.tp:].reshape(bs, ss, d)
    return (y_prompt, y_sample)
```

```python
import functools
import math

import numpy as np
import jax
import jax.numpy as jnp
from jax import lax
from jax.experimental import pallas as pl
from jax.experimental.pallas import tpu as pltpu

F32 = jnp.float32
BF16 = jnp.bfloat16

ALIBI_MAX = 8.0
NORM_EPS = 1e-6
TOP_K = 2
N_MOD = 6
LOG2E = math.log2(math.e)

VMEM_LIMIT_BYTES = 56 * 1024 * 1024
ROW_ALIGN = 16
SUBLANES = 8
LANES = 128
BF16_EXACT_INT = 256
SLOPE_PARTS = 3
SKIP_LOG2_GAP = 160.0
SKIP_NORM_SLACK = 1.01

INPROJ_ROWS, INPROJ_COLS = 1024, 2048
ATTN_Q_ROWS, ATTN_KV_CHUNK = 512, 2048
ATTN_STATS_ROWS = 8192
OUTPROJ_ROWS = 256
FFN_ROWS, FFN_COLS = 512, 1408
MOE_BLOCK = 1024
EXPERT_F_STEPS = 4
COMBINE_ROWS = 512
DMA_ISSUE_UNROLL = 8
DMA_PRIORITIES = 2
SMEM_INDEX_ALIGN = 1024


def _tile(n, pref, align=1):
    t = min(n, pref)
    while t > 0:
        if n % t == 0 and t % align == 0:
            return t
        t -= 1
    raise ValueError(f"no tile for {n} (pref {pref}, align {align})")


def _params(sem):
    return pltpu.CompilerParams(dimension_semantics=sem, vmem_limit_bytes=VMEM_LIMIT_BYTES)


def _rms(x):
    return x * lax.rsqrt(jnp.mean(x * x, axis=-1, keepdims=True) + NORM_EPS)


def _row_spec(d, fn=None):
    return pl.BlockSpec((1, d), fn if fn is not None else (lambda *_: (0, 0)))


def _ada_kernel(c_ref, w_ref, b_ref, o_ref):
    c = c_ref[...]
    cond = c * jax.nn.sigmoid(c)
    o_ref[0] = jnp.dot(cond, w_ref[0], preferred_element_type=F32,
                       precision=lax.Precision.HIGHEST) + b_ref[0]


def _ada(c_all, w_ada, b_ada):
    depth, d, _ = w_ada.shape
    bp = c_all.shape[0]
    b3 = b_ada.reshape(depth * N_MOD, 1, d)
    out = pl.pallas_call(
        _ada_kernel,
        grid=(depth, N_MOD),
        in_specs=[
            pl.BlockSpec((bp, d), lambda l, k: (0, 0)),
            pl.BlockSpec((1, d, d), lambda l, k: (l, 0, k)),
            pl.BlockSpec((1, 1, d), lambda l, k: (l * N_MOD + k, 0, 0)),
        ],
        out_specs=pl.BlockSpec((1, bp, d), lambda l, k: (l * N_MOD + k, 0, 0)),
        out_shape=jax.ShapeDtypeStruct((depth * N_MOD, bp, d), F32),
        compiler_params=_params(("arbitrary", "arbitrary")),
        name="ada_mod",
    )(c_all, w_ada, b3)
    return out.reshape(depth, N_MOD * bp, 1, d)


class _Geom:
    def __init__(self, bp, sp, bs, ss, d):
        self.bp, self.sp, self.bs, self.ss, self.d = bp, sp, bs, ss, d
        self.tp = bp * sp
        self.t = self.tp + bs * ss
        self.nb = bp + bs
        self.nb_pad = -(-self.nb // 8) * 8
        self.row_gcd = math.gcd(sp, ss)

    def batch_of_tile(self, i, rows):
        npt = self.tp // rows
        return jnp.where(i < npt, i // (self.sp // rows), self.bp + (i - npt) // (self.ss // rows))

    def mod_spec(self, k, rows):
        return pl.BlockSpec((1, 1, self.d), lambda i, *_: (k * self.nb_pad + self.batch_of_tile(i, rows), 0, 0))


def _inproj_kernel(x_ref, g_ref, scale_ref, shift_ref, w_ref, cs_ref, z_ref, h_scr):
    @pl.when(pl.program_id(1) == 0)
    def _():
        h = _rms(x_ref[...]) * g_ref[...] * (1.0 + scale_ref[0]) + shift_ref[0]
        h_scr[...] = h.astype(BF16)

    z = (jnp.dot(h_scr[...], w_ref[...], preferred_element_type=F32) * cs_ref[...]).astype(BF16)
    for s in range(z_ref.shape[0]):
        z_ref[s] = z[:, s * LANES:(s + 1) * LANES]


def _inproj(geom, x, g_pre, mod_l, w_in_l, col_scale):
    t, d = x.shape
    n = w_in_l.shape[1]
    tm = _tile(geom.row_gcd, INPROJ_ROWS, ROW_ALIGN)
    tn = _tile(n, INPROJ_COLS, LANES)
    return pl.pallas_call(
        _inproj_kernel,
        grid=(t // tm, n // tn),
        in_specs=[
            pl.BlockSpec((tm, d), lambda i, j: (i, 0)),
            _row_spec(d),
            geom.mod_spec(1, tm),
            geom.mod_spec(0, tm),
            pl.BlockSpec((d, tn), lambda i, j: (0, j)),
            pl.BlockSpec((1, tn), lambda i, j: (0, j)),
        ],
        out_specs=pl.BlockSpec((tn // LANES, tm, LANES), lambda i, j: (j, i, 0)),
        out_shape=jax.ShapeDtypeStruct((n // LANES, t, LANES), BF16),
        scratch_shapes=[pltpu.VMEM((tm, d), BF16)],
        compiler_params=_params(("arbitrary", "arbitrary")),
        name="in_proj",
    )(x, g_pre, mod_l, mod_l, w_in_l, col_scale)


def _attn_scores(s_ref, it, q_ref, k_ref, qc_ref, kc_ref, bd_ref, ed0_ref, d0_ref, dc_ref, head_dim):
    _, tq, v_dim = q_ref.shape
    n_sub = k_ref.shape[1] // tq
    q = q_ref[0]
    lane = lax.broadcasted_iota(jnp.int32, (tq, v_dim), 1)
    half = (lane < head_dim, lane >= head_dim)
    d0 = d0_ref[it]
    d0_v = jnp.full((1, v_dim), d0, jnp.int32).astype(F32)
    zero = jnp.zeros_like(q)
    for m in range(2):
        q_aug = jnp.where(half[m], q, zero) + qc_ref[0, m] + (d0_v * ed0_ref[m]).astype(BF16)
        for c in range(n_sub):
            delta = d0 - c * tq
            sgn = (delta > 0).astype(jnp.int32) - (delta < 0).astype(jnp.int32)
            sgn_v = jnp.full((1, v_dim), sgn, jnp.int32).astype(F32).astype(BF16)
            k_aug = jnp.where(half[m], k_ref[0, c * tq:(c + 1) * tq, :], sgn_v * kc_ref[0, m, c * tq:(c + 1) * tq, :])
            s_ref[m, :, c * tq:(c + 1) * tq] = lax.dot_general(
                q_aug, k_aug, (((1,), (1,)), ((), ())), preferred_element_type=F32)
    dc = dc_ref[it]
    flag = jnp.full((1, 1), (dc >= 0).astype(jnp.int32), jnp.int32).astype(F32)
    off = pl.multiple_of(jnp.maximum(dc, 0) * tq, tq)
    bias = flag * bd_ref[0]
    for m in range(2):
        s_ref[m, :, pl.ds(off, tq)] = s_ref[m, :, pl.ds(off, tq)] - bias


def _attn_softmax_pv(s_ref, is_first, v_ref, m_scr, acc_scr):
    _, chunk, v_dim = v_ref.shape
    ones_col = jnp.where(lax.broadcasted_iota(jnp.int32, (chunk, v_dim), 1) == 0, 1.0, 0.0).astype(BF16)
    v_aug = jnp.concatenate([v_ref[0], ones_col], axis=1)
    for m in range(2):
        s = s_ref[m]
        m_old = jnp.where(is_first, -jnp.inf, m_scr[m])
        acc_old = jnp.where(is_first, 0.0, acc_scr[m])
        m_new = jnp.maximum(m_old, jnp.max(s, axis=-1, keepdims=True))
        p = jnp.exp2(s - m_new).astype(BF16)
        alpha = jnp.exp2(m_old - m_new)
        acc_scr[m] = alpha * acc_old + jnp.dot(p, v_aug, preferred_element_type=F32)
        m_scr[m] = m_new


def _attn_kernel(head_ref, qblk_ref, kvblk_ref, first_ref, last_ref, d0_ref, dc_ref, nk_ref,
                 q_ref, k_ref, v_ref, qc_ref, kc_ref, bd_ref, ed0_ref, lam_ref, gsub_ref, o_ref,
                 s_even, s_odd, m_scr, acc_scr, *, head_dim):
    g = pl.program_id(0)
    n_kept = nk_ref[0]
    e_a = jnp.minimum(g, n_kept - 1)
    e_b = jnp.minimum(jnp.maximum(g - 1, 0), n_kept - 1)
    active = g <= n_kept
    v_dim = q_ref.shape[2]

    @pl.when(g == 0)
    def _():
        s_odd[...] = jnp.zeros(s_odd.shape, F32)
        m_scr[...] = jnp.zeros(m_scr.shape, F32)
        acc_scr[...] = jnp.zeros(acc_scr.shape, F32)

    is_first = jnp.full((1, 1), first_ref[e_b], jnp.int32) > 0

    def step(s_write, s_read):
        _attn_scores(s_write, e_a, q_ref, k_ref, qc_ref, kc_ref, bd_ref, ed0_ref, d0_ref, dc_ref, head_dim)
        _attn_softmax_pv(s_read, is_first, v_ref, m_scr, acc_scr)

    @pl.when(jnp.logical_and(active, g % 2 == 0))
    def _():
        step(s_even, s_odd)

    @pl.when(jnp.logical_and(active, g % 2 == 1))
    def _():
        step(s_odd, s_even)

    @pl.when(jnp.logical_and(active, last_ref[e_b] == 1))
    def _():
        lp = lam_ref[...]
        lam_init = lp[4:5, :1]
        lam = (jnp.exp(jnp.sum(lp[0:1] * lp[1:2], axis=-1, keepdims=True))
               - jnp.exp(jnp.sum(lp[2:3] * lp[3:4], axis=-1, keepdims=True)) + lam_init)
        a0 = acc_scr[0]
        a1 = acc_scr[1]
        o = a0[:, :v_dim] / a0[:, v_dim:v_dim + 1] - lam * (a1[:, :v_dim] / a1[:, v_dim:v_dim + 1])
        o_ref[0] = (_rms(o) * gsub_ref[...] * (1.0 - lam_init)).astype(o_ref.dtype)


def _attn_items(geom, tq, chunk):
    qblk, kvblk, d0s, dcs, dist = [], [], [], [], []
    for off, nb, s in ((0, geom.bp, geom.sp), (geom.tp, geom.bs, geom.ss)):
        for b in range(nb):
            for qt in range(s // tq):
                for c in range(s // chunk):
                    qblk.append((off + b * s + qt * tq) // tq)
                    kvblk.append((off + b * s + c * chunk) // chunk)
                    d0 = qt * tq - c * chunk
                    d0s.append(d0)
                    dcs.append(d0 // tq if 0 <= d0 < chunk else -1)
                    dist.append(0 if 0 <= d0 < chunk else (d0 - chunk + 1 if d0 > 0 else -d0 - tq + 1))
    names = ("qblk", "kvblk", "d0", "dc", "min_dist")
    return {n: np.asarray(a, np.int32) for n, a in zip(names, (qblk, kvblk, d0s, dcs, dist))}


def _attn_stats_kernel(q_ref, k_ref, o_ref, *, head_dim, tq):
    v_dim = q_ref.shape[2]
    r = lax.broadcasted_iota(jnp.int32, (v_dim, v_dim), 0)
    c = lax.broadcasted_iota(jnp.int32, (v_dim, v_dim), 1)
    sel = jnp.where(jnp.logical_or(jnp.logical_and(c == 0, r < head_dim), jnp.logical_and(c == 1, r >= head_dim)),
                    1.0, 0.0).astype(BF16)
    pad = jnp.zeros((o_ref.shape[2] - 3, v_dim), F32)
    for t in range(q_ref.shape[1] // tq):
        q = q_ref[0, t * tq:(t + 1) * tq, :].astype(F32)
        k = k_ref[0, t * tq:(t + 1) * tq, :].astype(F32)
        out_rows = []
        for x, reduce_rows in ((q * q, jnp.max), (k * k, jnp.max), (q * k, jnp.min)):
            per_row = jnp.dot(x.astype(BF16), sel, preferred_element_type=F32)
            out_rows.append(reduce_rows(per_row, axis=0, keepdims=True))
        o_ref[0, t] = jnp.concatenate(out_rows + [pad], axis=0)


def _attn_stats(qkv, n_heads, head_dim, tq, rows_per_step):
    t, v_dim = qkv.shape[1], qkv.shape[2]
    tiles = rows_per_step // tq
    return pl.pallas_call(
        functools.partial(_attn_stats_kernel, head_dim=head_dim, tq=tq),
        grid=(n_heads, t // rows_per_step),
        in_specs=[
            pl.BlockSpec((1, rows_per_step, v_dim), lambda h, i: (h, i, 0)),
            pl.BlockSpec((1, rows_per_step, v_dim), lambda h, i: (n_heads + h, i, 0)),
        ],
        out_specs=pl.BlockSpec((1, tiles, SUBLANES, LANES), lambda h, i: (h, i, 0, 0)),
        out_shape=jax.ShapeDtypeStruct((n_heads, t // tq, SUBLANES, LANES), F32),
        compiler_params=_params(("arbitrary", "arbitrary")),
        name="attn_stats",
    )(qkv, qkv)


def _attn_entries(items, stats, slopes_log2, n_sub):
    n_heads = stats.shape[0]
    n_items = items["qblk"].shape[0]
    q_norm = jnp.sqrt(stats[:, :, 0, 0:2])
    k_norm = jnp.sqrt(stats[:, :, 1, 0:2])
    self_min = stats[:, :, 2, 0:2]
    kv_tiles = items["kvblk"][:, None] * n_sub + np.arange(n_sub)[None, :]
    k_norm_item = jnp.max(k_norm[:, kv_tiles, :], axis=2)
    upper = (q_norm[:, items["qblk"], :] * k_norm_item * SKIP_NORM_SLACK
             - slopes_log2[:, None, None] * items["min_dist"][None, :, None].astype(np.float32))
    gap = upper - self_min[:, items["qblk"], :]
    drop = jnp.logical_and(jnp.all(gap < -SKIP_LOG2_GAP, axis=-1), (items["min_dist"] > 0)[None, :])
    keep = jnp.logical_not(drop).reshape(-1)
    n_all = n_heads * n_items
    order = jnp.argsort(jnp.logical_not(keep), stable=True).astype(jnp.int32)
    n_kept = jnp.sum(keep.astype(jnp.int32))
    pos = jnp.arange(n_all, dtype=jnp.int32)
    src = order[jnp.minimum(pos, n_kept - 1)]
    head = src // n_items
    it = src % n_items
    qblk = jnp.asarray(items["qblk"])[it]
    tile_id = head * (int(items["qblk"].max()) + 1) + qblk
    first = jnp.logical_or(pos == 0, tile_id != jnp.roll(tile_id, 1))
    last = jnp.logical_or(pos >= n_kept - 1, tile_id != jnp.roll(tile_id, -1))
    i32 = lambda a: a.astype(jnp.int32)
    return (i32(head), qblk, jnp.asarray(items["kvblk"])[it], i32(first), i32(last),
            jnp.asarray(items["d0"])[it], jnp.asarray(items["dc"])[it], n_kept.reshape(1))


def _bf16_parts(x, n):
    parts, rest = [], np.asarray(x, np.float64)
    for _ in range(n):
        p = rest.astype(BF16).astype(np.float64)
        parts.append(p)
        rest = rest - p
    return parts


def _attn_consts(n_heads, head_dim, tq, chunk, max_d0):
    v_dim = 2 * head_dim
    slopes = 2.0 ** (-ALIBI_MAX * np.arange(1, n_heads + 1, dtype=np.float64) / n_heads) * LOG2E
    sl = _bf16_parts(slopes, SLOPE_PARTS)
    r = np.arange(tq)
    j = np.arange(chunk)
    i_parts = [None, (r // BF16_EXACT_INT) * BF16_EXACT_INT, r % BF16_EXACT_INT]
    j_parts = [(j // BF16_EXACT_INT) * BF16_EXACT_INT, j % BF16_EXACT_INT]
    assert chunk <= BF16_EXACT_INT ** 2 and max_d0 % BF16_EXACT_INT == 0 and max_d0 <= BF16_EXACT_INT ** 2
    n_cols = SLOPE_PARTS * (len(i_parts) + len(j_parts))
    assert n_cols <= head_dim
    qc = np.zeros((n_heads, 2, tq, v_dim), np.float64)
    kc = np.zeros((n_heads, 2, chunk, v_dim), np.float64)
    ed0 = np.zeros((2, 1, v_dim), np.float32)
    for m in range(2):
        base = head_dim if m == 0 else 0
        col = base
        for a in range(SLOPE_PARTS):
            for jp in j_parts:
                qc[:, m, :, col] = sl[a][:, None]
                kc[:, m, :, col] = jp[None, :]
                col += 1
            for ip in i_parts:
                if ip is None:
                    ed0[m, 0, col] = 1.0
                else:
                    qc[:, m, :, col] = ip[None, :]
                kc[:, m, :, col] = -sl[a][:, None]
                col += 1
    bd = slopes[:, None, None] * np.abs(r[:, None] - r[None, :])[None]
    for arr in (qc, kc):
        assert np.array_equal(arr.astype(BF16).astype(np.float64), arr)
    return (jnp.asarray(qc.astype(BF16)), jnp.asarray(kc.astype(BF16)),
            jnp.asarray(bd.astype(np.float32)), jnp.asarray(ed0), jnp.asarray(slopes.astype(np.float32)))


def _attention(geom, qkv, consts, items, lam_pack_l, g_subln_l, n_heads, head_dim, tq, chunk):
    t = qkv.shape[1]
    v_dim = 2 * head_dim
    qc, kc, bd, ed0, slopes_log2 = consts
    n_sub = chunk // tq
    stats = _attn_stats(qkv, n_heads, head_dim, tq, _tile(t, ATTN_STATS_ROWS, tq))
    entries = _attn_entries(items, stats, slopes_log2, n_sub)
    n_entries = entries[0].shape[0]
    k_head0 = n_heads
    v_head0 = 2 * n_heads

    def score_entry(g, nk):
        return jnp.minimum(g, nk[0] - 1)

    def finish_entry(g, nk):
        return jnp.minimum(jnp.maximum(g - 1, 0), nk[0] - 1)

    def q_map(g, hd, qb, kb, fi, la, d0, dc, nk):
        e = score_entry(g, nk)
        return hd[e], qb[e], 0

    def k_map(g, hd, qb, kb, fi, la, d0, dc, nk):
        e = score_entry(g, nk)
        return k_head0 + hd[e], kb[e], 0

    def v_map(g, hd, qb, kb, fi, la, d0, dc, nk):
        e = finish_entry(g, nk)
        return v_head0 + hd[e], kb[e], 0

    def o_map(g, hd, qb, kb, fi, la, d0, dc, nk):
        e = finish_entry(g, nk)
        return hd[e], qb[e], 0

    def head_map(n_trailing):
        def fn(g, hd, qb, kb, fi, la, d0, dc, nk):
            return (hd[score_entry(g, nk)],) + (0,) * n_trailing
        return fn

    grid_spec = pltpu.PrefetchScalarGridSpec(
        num_scalar_prefetch=8,
        grid=(n_entries + 1,),
        in_specs=[
            pl.BlockSpec((1, tq, v_dim), q_map),
            pl.BlockSpec((1, chunk, v_dim), k_map),
            pl.BlockSpec((1, chunk, v_dim), v_map),
            pl.BlockSpec((1, 2, tq, v_dim), head_map(3)),
            pl.BlockSpec((1, 2, chunk, v_dim), head_map(3)),
            pl.BlockSpec((1, tq, tq), head_map(2)),
            pl.BlockSpec((2, 1, v_dim), lambda g, *_: (0, 0, 0)),
            pl.BlockSpec((8, head_dim), lambda g, *_: (0, 0)),
            _row_spec(v_dim),
        ],
        out_specs=pl.BlockSpec((1, tq, v_dim), o_map),
        scratch_shapes=[
            pltpu.VMEM((2, tq, chunk), F32),
            pltpu.VMEM((2, tq, chunk), F32),
            pltpu.VMEM((2, tq, 1), F32),
            pltpu.VMEM((2, tq, 2 * v_dim), F32),
        ],
    )
    return pl.pallas_call(
        functools.partial(_attn_kernel, head_dim=head_dim),
        grid_spec=grid_spec,
        out_shape=jax.ShapeDtypeStruct((n_heads, t, v_dim), BF16),
        compiler_params=_params(("arbitrary",)),
        name="diff_attn",
    )(*entries, qkv, qkv, qkv, qc, kc, bd, ed0, lam_pack_l, g_subln_l)


def _outproj_kernel(attn_ref, cb_ref, cc_ref, cx_ref, ga_ref, gc_ref,
                    ccp_ref, cxp_ref, ccn_ref, cxn_ref,
                    x_ref, convw_ref, wout_ref, gpost_ref, gatem_ref, gpre_ref, scalef_ref, shiftf_ref,
                    *rest, geom, moe):
    if moe:
        wr_ref, xo_ref, h_ref, tope_ref, topw_ref = rest
    else:
        xo_ref, h_ref = rest
    tm = x_ref.shape[0]
    row0 = pl.program_id(0) * tm
    seq = jnp.where(row0 < geom.tp, geom.sp, geom.ss)
    rel0 = jnp.where(row0 < geom.tp, row0, row0 - geom.tp)
    not_start = (rel0 % seq != 0).astype(F32)
    not_end = ((rel0 + tm) % seq != 0).astype(F32)

    def rows(ref, lo=0, hi=None):
        hi = ref.shape[1] if hi is None else hi
        return jnp.concatenate([ref[s, lo:hi, :] for s in range(ref.shape[0])], axis=1).astype(F32)

    u = rows(cc_ref) * rows(cx_ref)
    halo = ccp_ref.shape[1]
    u_prev = (rows(ccp_ref, halo - 1, halo) * rows(cxp_ref, halo - 1, halo)) * not_start
    u_next = (rows(ccn_ref, 0, 1) * rows(cxn_ref, 0, 1)) * not_end
    row = lax.broadcasted_iota(jnp.int32, u.shape, 0)
    u_m1 = jnp.where(row == 0, u_prev, pltpu.roll(u, 1, 0))
    u_p1 = jnp.where(row == tm - 1, u_next, pltpu.roll(u, tm - 1, 0))
    cw = convw_ref[...]
    conv = u_m1 * cw[0:1] + u * cw[1:2] + u_p1 * cw[2:3]
    short = rows(cb_ref) * conv
    merged = jax.nn.sigmoid(rows(ga_ref)) * rows(attn_ref) + jax.nn.sigmoid(rows(gc_ref)) * short
    o = jnp.dot(merged.astype(BF16), wout_ref[...], preferred_element_type=F32)
    x_new = x_ref[...] + gatem_ref[0] * (_rms(o) * gpost_ref[...])
    xo_ref[...] = x_new
    h = _rms(x_new) * gpre_ref[...] * (1.0 + scalef_ref[0]) + shiftf_ref[0]
    h_ref[...] = h.astype(h_ref.dtype)

    if moe:
        n_e = wr_ref.shape[1]
        wr = wr_ref[...]
        h_hi = h.astype(BF16)
        h_lo = (h - h_hi.astype(F32)).astype(BF16)
        w_hi = wr.astype(BF16)
        w_lo = (wr - w_hi.astype(F32)).astype(BF16)
        logits = (jnp.dot(h_hi, w_hi, preferred_element_type=F32) + jnp.dot(h_lo, w_hi, preferred_element_type=F32)
                  + jnp.dot(h_hi, w_lo, preferred_element_type=F32))
        lane = lax.broadcasted_iota(jnp.int32, logits.shape, 1).astype(F32)
        m1 = jnp.max(logits, axis=-1, keepdims=True)
        i1 = jnp.min(jnp.where(logits == m1, lane, float(n_e)), axis=-1, keepdims=True)
        rest_l = jnp.where(lane == i1, -jnp.inf, logits)
        m2 = jnp.max(rest_l, axis=-1, keepdims=True)
        i2 = jnp.min(jnp.where(rest_l == m2, lane, float(n_e)), axis=-1, keepdims=True)
        e = jnp.exp(m2 - m1)
        w1 = 1.0 / (1.0 + e)
        w2 = e / (1.0 + e)
        tope_ref[...] = jnp.where(lane == 0.0, i1, jnp.where(lane == 1.0, i2, 0.0)).astype(jnp.int32)
        topw_ref[...] = jnp.where(lane == 0.0, w1, jnp.where(lane == 1.0, w2, 0.0))


def _outproj(geom, attn, z, x, conv_w_l, w_out_l, g_post, g_pre, mod_l, w_router_l):
    t, d = x.shape
    moe = w_router_l is not None
    tm = _tile(geom.row_gcd, OUTPROJ_ROWS, ROW_ALIGN)
    halo = ROW_ALIGN
    nh = tm // halo
    last_halo = t // halo - 1
    n_heads, _, v_dim = attn.shape
    spt = d // LANES
    zcol = lambda c: pl.BlockSpec((spt, tm, LANES), lambda i: (c, i, 0))
    zhalo = lambda c, fn: pl.BlockSpec((spt, halo, LANES), lambda i: (c, fn(i), 0))
    prev_blk = lambda i: jnp.maximum(i * nh - 1, 0)
    next_blk = lambda i: jnp.minimum((i + 1) * nh, last_halo)
    in_specs = [
        pl.BlockSpec((n_heads, tm, v_dim), lambda i: (0, i, 0)),
        zcol(3), zcol(4), zcol(5), zcol(6), zcol(7),
        zhalo(4, prev_blk), zhalo(5, prev_blk), zhalo(4, next_blk), zhalo(5, next_blk),
        pl.BlockSpec((tm, d), lambda i: (i, 0)),
        pl.BlockSpec(conv_w_l.shape, lambda i: (0, 0)),
        pl.BlockSpec((d, d), lambda i: (0, 0)),
        _row_spec(d),
        geom.mod_spec(2, tm),
        _row_spec(d),
        geom.mod_spec(4, tm), geom.mod_spec(3, tm),
    ]
    args = [attn, z, z, z, z, z, z, z, z, z, x, conv_w_l, w_out_l, g_post, mod_l, g_pre, mod_l, mod_l]
    out_specs = [pl.BlockSpec((tm, d), lambda i: (i, 0)), pl.BlockSpec((tm, d), lambda i: (i, 0))]
    out_shape = [jax.ShapeDtypeStruct((t, d), F32), jax.ShapeDtypeStruct((t, d), F32 if moe else BF16)]
    if moe:
        n_e = w_router_l.shape[1]
        in_specs.append(pl.BlockSpec((d, n_e), lambda i: (0, 0)))
        args.append(w_router_l)
        out_specs += [pl.BlockSpec((tm, n_e), lambda i: (i, 0))] * 2
        out_shape += [jax.ShapeDtypeStruct((t, n_e), jnp.int32), jax.ShapeDtypeStruct((t, n_e), F32)]
    return pl.pallas_call(
        functools.partial(_outproj_kernel, geom=geom, moe=moe),
        grid=(t // tm,),
        in_specs=in_specs,
        out_specs=out_specs,
        out_shape=out_shape,
        compiler_params=_params(("arbitrary",)),
        name="out_proj_moe" if moe else "out_proj",
    )(*args)


def _swiglu_partial(h, wg, wu, wd):
    g = jnp.dot(h, wg, preferred_element_type=F32)
    u = jnp.dot(h, wu, preferred_element_type=F32)
    a = (g * jax.nn.sigmoid(g)) * u
    return jnp.dot(a.astype(BF16), wd, preferred_element_type=F32)


def _ffn_kernel(h_ref, x_ref, wg_ref, wu_ref, wd_ref, gpost_ref, gate_ref, xo_ref, acc_scr):
    f = pl.program_id(1)
    part = _swiglu_partial(h_ref[...], wg_ref[...], wu_ref[...], wd_ref[...])

    @pl.when(f == 0)
    def _():
        acc_scr[...] = part

    @pl.when(f > 0)
    def _():
        acc_scr[...] += part

    @pl.when(f == pl.num_programs(1) - 1)
    def _():
        xo_ref[...] = x_ref[...] + gate_ref[0] * (_rms(acc_scr[...]) * gpost_ref[...])


def _ffn(geom, h, x, wg, wu, wd, g_post, mod_l):
    t, d = x.shape
    ff = wg.shape[1]
    tm = _tile(geom.row_gcd, FFN_ROWS, ROW_ALIGN)
    tf = _tile(ff, FFN_COLS, LANES)
    return pl.pallas_call(
        _ffn_kernel,
        grid=(t // tm, ff // tf),
        in_specs=[
            pl.BlockSpec((tm, d), lambda i, f: (i, 0)),
            pl.BlockSpec((tm, d), lambda i, f: (i, 0)),
            pl.BlockSpec((d, tf), lambda i, f: (0, f)),
            pl.BlockSpec((d, tf), lambda i, f: (0, f)),
            pl.BlockSpec((tf, d), lambda i, f: (f, 0)),
            _row_spec(d),
            geom.mod_spec(5, tm),
        ],
        out_specs=pl.BlockSpec((tm, d), lambda i, f: (i, 0)),
        out_shape=jax.ShapeDtypeStruct((t, d), F32),
        scratch_shapes=[pltpu.VMEM((tm, d), F32)],
        compiler_params=_params(("arbitrary", "arbitrary")),
        name="dense_ffn",
    )(h, x, wg, wu, wd, g_post, mod_l)


def _expert_kernel(be_ref, idx0_ref, idxn_ref, h_ref, wg_ref, wu_ref, wd_ref, y_ref,
                   xf_scr, xb_scr, sem):
    b = pl.program_id(0)
    f = pl.program_id(1)
    n_blk = pl.num_programs(0)
    n_f = pl.num_programs(1)
    blk, d = xb_scr.shape
    slot = b % 2

    def row_copy(idx_ref, r, s, tile, sub):
        return pltpu.make_async_copy(h_ref.at[pl.ds(idx_ref[r], 1)], xf_scr.at[s, tile, pl.ds(sub, 1)], sem.at[s])

    def block_copy(s):
        return pltpu.make_async_copy(xf_scr.at[s], xf_scr.at[s], sem.at[s])

    @pl.when(jnp.logical_and(b == 0, f == 0))
    def _():
        def issue(r, carry):
            row_copy(idx0_ref, r, 0, r // SUBLANES, r % SUBLANES).start()
            return carry

        lax.fori_loop(0, blk, issue, 0, unroll=DMA_ISSUE_UNROLL)

    @pl.when(f == 0)
    def _():
        block_copy(slot).wait()
        xb_scr[...] = xf_scr[slot].reshape(blk, d).astype(BF16)

    def prefetch_slice():
        per_step = blk // EXPERT_F_STEPS
        for u in range(per_step):
            row_copy(idxn_ref, f * per_step + u, 1 - slot,
                     f * (per_step // SUBLANES) + u // SUBLANES, u % SUBLANES).start(priority=u % DMA_PRIORITIES)

    part = _swiglu_partial(xb_scr[...], wg_ref[0], wu_ref[0], wd_ref[0])

    @pl.when(f == 0)
    def _():
        prefetch_slice()
        y_ref[...] = part

    @pl.when(f > 0)
    def _():
        prefetch_slice()
        y_ref[...] += part

    @pl.when(jnp.logical_and(b == n_blk - 1, f == n_f - 1))
    def _():
        block_copy(1 - slot).wait()


def _experts(h, row_tok, block_e, wg, wu, wd, blk):
    n_pad = row_tok.shape[0]
    d = h.shape[1]
    ff = wg.shape[2]
    tf = ff // EXPERT_F_STEPS
    n_blk = n_pad // blk
    assert ff % EXPERT_F_STEPS == 0 and tf % LANES == 0 and blk % SMEM_INDEX_ALIGN == 0
    assert blk % (EXPERT_F_STEPS * SUBLANES) == 0 and h.dtype == F32
    grid_spec = pltpu.PrefetchScalarGridSpec(
        num_scalar_prefetch=1,
        grid=(n_blk, EXPERT_F_STEPS),
        in_specs=[
            pl.BlockSpec((blk,), lambda b, f, be: (0,), memory_space=pltpu.SMEM),
            pl.BlockSpec((blk,), lambda b, f, be: (jnp.minimum(b + 1, n_blk - 1),), memory_space=pltpu.SMEM),
            pl.BlockSpec(memory_space=pl.ANY),
            pl.BlockSpec((1, d, tf), lambda b, f, be: (be[b], 0, f)),
            pl.BlockSpec((1, d, tf), lambda b, f, be: (be[b], 0, f)),
            pl.BlockSpec((1, tf, d), lambda b, f, be: (be[b], f, 0)),
        ],
        out_specs=pl.BlockSpec((blk, d), lambda b, f, be: (b, 0)),
        scratch_shapes=[
            pltpu.VMEM((2, blk // SUBLANES, SUBLANES, d), F32),
            pltpu.VMEM((blk, d), BF16),
            pltpu.SemaphoreType.DMA((2,)),
        ],
    )
    return pl.pallas_call(
        _expert_kernel,
        grid_spec=grid_spec,
        out_shape=jax.ShapeDtypeStruct((n_pad, d), F32),
        compiler_params=_params(("arbitrary", "arbitrary")),
        name="expert_ffn",
    )(block_e, row_tok, row_tok, h, wg, wu, wd)


def _combine_kernel(dest_ref, y_ref, w_ref, x_ref, gpost_ref, gate_ref, xo_ref, buf, sem):
    tc, d = x_ref.shape

    def issue(tile, carry):
        for sub in range(SUBLANES):
            for k in range(TOP_K):
                src_row = dest_ref[TOP_K * (tile * SUBLANES + sub) + k]
                pltpu.make_async_copy(y_ref.at[pl.ds(src_row, 1)], buf.at[k, tile, pl.ds(sub, 1)],
                                      sem).start(priority=k % DMA_PRIORITIES)
        return carry

    lax.fori_loop(0, tc // SUBLANES, issue, 0)
    pltpu.make_async_copy(buf, buf, sem).wait()
    w = w_ref[...]
    o = w[:, 0:1] * buf[0].reshape(tc, d)
    for k in range(1, TOP_K):
        o = o + w[:, k:k + 1] * buf[k].reshape(tc, d)
    xo_ref[...] = x_ref[...] + gate_ref[0] * (_rms(o) * gpost_ref[...])


def _combine(geom, y_rows, dest, top_w, x, g_post, mod_l):
    t, d = x.shape
    n_e = top_w.shape[1]
    tc = _tile(geom.row_gcd, COMBINE_ROWS, SMEM_INDEX_ALIGN // TOP_K)
    return pl.pallas_call(
        _combine_kernel,
        grid=(t // tc,),
        in_specs=[
            pl.BlockSpec((TOP_K * tc,), lambda i: (i,), memory_space=pltpu.SMEM),
            pl.BlockSpec(memory_space=pl.ANY),
            pl.BlockSpec((tc, n_e), lambda i: (i, 0)),
            pl.BlockSpec((tc, d), lambda i: (i, 0)),
            _row_spec(d),
            geom.mod_spec(5, tc),
        ],
        out_specs=pl.BlockSpec((tc, d), lambda i: (i, 0)),
        out_shape=jax.ShapeDtypeStruct((t, d), F32),
        scratch_shapes=[pltpu.VMEM((TOP_K, tc // SUBLANES, SUBLANES, d), F32), pltpu.SemaphoreType.DMA(())],
        compiler_params=_params(("arbitrary",)),
        name="moe_combine",
    )(dest, y_rows, top_w, x, g_post, mod_l)


def _route(top_e, n_experts, blk):
    t = top_e.shape[0]
    n_assign = t * TOP_K
    n_blocks = -(-(n_assign + n_experts * (blk - 1)) // blk)
    n_pad = n_blocks * blk
    flat_e = top_e[:, :TOP_K].reshape(-1)
    onehot = (flat_e[:, None] == jnp.arange(n_experts, dtype=jnp.int32)[None, :]).astype(jnp.int32)
    rank = jnp.sum((jnp.cumsum(onehot, axis=0) - onehot) * onehot, axis=1)
    counts = jnp.sum(onehot, axis=0)
    padded = (counts + blk - 1) // blk * blk
    end_pad = jnp.cumsum(padded)
    start_pad = end_pad - padded
    dest = (start_pad[flat_e] + rank).astype(jnp.int32)
    flat_tok = jnp.arange(n_assign, dtype=jnp.int32) // TOP_K
    row_tok = jnp.zeros((n_pad,), jnp.int32).at[dest].set(flat_tok)
    block_e = jnp.minimum(
        jnp.searchsorted(end_pad, jnp.arange(n_blocks, dtype=jnp.int32) * blk, side='right'),
        n_experts - 1).astype(jnp.int32)
    return dest, row_tok, block_e


def kernel(x_prompt, x_sample, c_prompt, c_sample, w_ada, b_ada, g_mix_pre, g_mix_post, g_ffn_pre, g_ffn_post,
           w_in, lam_q1, lam_k1, lam_q2, lam_k2, g_subln, conv_w, w_out, w_ffn_gate, w_ffn_up, w_ffn_down,
           w_router, w_exp_gate, w_exp_up, w_exp_down):
    bp, sp, d = x_prompt.shape
    bs, ss, _ = x_sample.shape
    depth = w_in.shape[0]
    head_dim = lam_q1.shape[1]
    v_dim = g_subln.shape[1]
    n_heads = d // v_dim
    n_experts = w_router.shape[2]
    n_in = w_in.shape[2]
    assert v_dim == 2 * head_dim and v_dim == LANES and n_in == 8 * d
    geom = _Geom(bp, sp, bs, ss, d)

    x = jnp.concatenate([x_prompt.reshape(bp * sp, d), x_sample.reshape(bs * ss, d)], axis=0)
    c_all = jnp.concatenate([c_prompt, c_sample, jnp.zeros((geom.nb_pad - geom.nb, d), F32)], axis=0)
    mod = _ada(c_all, w_ada, b_ada)

    tq = _tile(geom.row_gcd, ATTN_Q_ROWS, ROW_ALIGN)
    chunk = _tile(geom.row_gcd, ATTN_KV_CHUNK, tq)
    attn_consts = _attn_consts(n_heads, head_dim, tq, chunk, max(sp, ss))
    attn_items = _attn_items(geom, tq, chunk)
    col_scale = jnp.asarray(np.where(np.arange(n_in) < n_heads * v_dim, head_dim ** -0.5 * LOG2E, 1.0)
                            .astype(np.float32))[None, :]
    lam_init = np.asarray([0.8 - 0.6 * math.exp(-0.3 * l) for l in range(depth)], np.float32)
    lam_pack = jnp.stack(
        [lam_q1, lam_k1, lam_q2, lam_k2, jnp.broadcast_to(jnp.asarray(lam_init)[:, None], lam_q1.shape)]
        + [jnp.zeros_like(lam_q1)] * 3, axis=1)

    per_layer = dict(mod=mod, g_mix_pre=g_mix_pre[:, None], g_mix_post=g_mix_post[:, None],
                     g_ffn_pre=g_ffn_pre[:, None], g_ffn_post=g_ffn_post[:, None], g_subln=g_subln[:, None],
                     w_in=w_in, lam=lam_pack, conv_w=conv_w, w_out=w_out)

    def mixer(x, p, w_router_l):
        z = _inproj(geom, x, p["g_mix_pre"], p["mod"], p["w_in"].astype(BF16), col_scale)
        attn = _attention(geom, z, attn_consts, attn_items, p["lam"], p["g_subln"], n_heads, head_dim, tq, chunk)
        return _outproj(geom, attn, z, x, p["conv_w"], p["w_out"].astype(BF16), p["g_mix_post"], p["g_ffn_pre"],
                        p["mod"], w_router_l)

    def dense_layer(x, p, w):
        x, h = mixer(x, p, None)
        return _ffn(geom, h, x, w["gate"].astype(BF16), w["up"].astype(BF16), w["down"].astype(BF16),
                    p["g_ffn_post"], p["mod"])

    def expert_layer(x, p, w):
        x, h, top_e, top_w = mixer(x, p, w["router"])
        dest, row_tok, block_e = _route(top_e, n_experts, MOE_BLOCK)
        y_rows = _experts(h, row_tok, block_e, w["gate"].astype(BF16), w["up"].astype(BF16),
                          w["down"].astype(BF16), MOE_BLOCK)
        return _combine(geom, y_rows, dest, top_w, x, p["g_ffn_post"], p["mod"])

    n_pairs = depth // 2
    even = jax.tree.map(lambda a: a[0:2 * n_pairs:2], per_layer)
    odd = jax.tree.map(lambda a: a[1:2 * n_pairs:2], per_layer)
    dense_w = dict(gate=w_ffn_gate, up=w_ffn_up, down=w_ffn_down)
    exp_w = dict(router=w_router, gate=w_exp_gate, up=w_exp_up, down=w_exp_down)

    def pair(x, xs):
        p_even, p_odd, dw, ew = xs
        x = dense_layer(x, p_even, dw)
        x = expert_layer(x, p_odd, ew)
        return x, None

    x, _ = lax.scan(pair, x, (even, odd, jax.tree.map(lambda a: a[:n_pairs], dense_w), exp_w))
    if depth % 2 == 1:
        x = dense_layer(x, jax.tree.map(lambda a: a[depth - 1], per_layer),
                        jax.tree.map(lambda a: a[n_pairs], dense_w))

    y_prompt = x[:geom.tp].reshape(bp, sp, d)
    y_sample = x[geom.tp:].reshape(bs, ss, d)
    return (y_prompt, y_sample)
```

```python
import functools
import math

import numpy as np
import jax
import jax.numpy as jnp
from jax import lax
from jax.experimental import pallas as pl
from jax.experimental.pallas import tpu as pltpu

F32 = jnp.float32
BF16 = jnp.bfloat16

ALIBI_MAX = 8.0
NORM_EPS = 1e-6
TOP_K = 2
N_MOD = 6
LOG2E = math.log2(math.e)

VMEM_LIMIT_BYTES = 56 * 1024 * 1024
ROW_ALIGN = 16
SUBLANES = 8
LANES = 128
BF16_EXACT_INT = 256
SLOPE_PARTS = 3
SKIP_LOG2_GAP = 160.0
SKIP_NORM_SLACK = 1.01
SKIP_SELF_SLACK = 2.0 ** -7

INPROJ_ROWS, INPROJ_COLS = 1024, 2048
ATTN_Q_ROWS, ATTN_KV_CHUNK = 512, 2048
ATTN_STATS_ROWS = 8192
OUTPROJ_ROWS = 256
FFN_ROWS, FFN_COLS = 512, 1408
MOE_BLOCK = 1024
EXPERT_F_STEPS = 4
COMBINE_ROWS = 512
DMA_ISSUE_UNROLL = 8
DMA_PRIORITIES = 2
SMEM_INDEX_ALIGN = 1024


def _tile(n, pref, align=1):
    t = min(n, pref)
    while t > 0:
        if n % t == 0 and t % align == 0:
            return t
        t -= 1
    raise ValueError(f"no tile for {n} (pref {pref}, align {align})")


def _params(sem):
    return pltpu.CompilerParams(dimension_semantics=sem, vmem_limit_bytes=VMEM_LIMIT_BYTES)


def _rms(x):
    return x * lax.rsqrt(jnp.mean(x * x, axis=-1, keepdims=True) + NORM_EPS)


def _row_spec(d, fn=None):
    return pl.BlockSpec((1, d), fn if fn is not None else (lambda *_: (0, 0)))


def _ada_kernel(c_ref, w_ref, b_ref, o_ref):
    c = c_ref[...]
    cond = c * jax.nn.sigmoid(c)
    o_ref[0] = jnp.dot(cond, w_ref[0], preferred_element_type=F32,
                       precision=lax.Precision.HIGHEST) + b_ref[0]


def _ada(c_all, w_ada, b_ada):
    depth, d, _ = w_ada.shape
    bp = c_all.shape[0]
    b3 = b_ada.reshape(depth * N_MOD, 1, d)
    out = pl.pallas_call(
        _ada_kernel,
        grid=(depth, N_MOD),
        in_specs=[
            pl.BlockSpec((bp, d), lambda l, k: (0, 0)),
            pl.BlockSpec((1, d, d), lambda l, k: (l, 0, k)),
            pl.BlockSpec((1, 1, d), lambda l, k: (l * N_MOD + k, 0, 0)),
        ],
        out_specs=pl.BlockSpec((1, bp, d), lambda l, k: (l * N_MOD + k, 0, 0)),
        out_shape=jax.ShapeDtypeStruct((depth * N_MOD, bp, d), F32),
        compiler_params=_params(("arbitrary", "arbitrary")),
        name="ada_mod",
    )(c_all, w_ada, b3)
    return out.reshape(depth, N_MOD * bp, 1, d)


class _Geom:
    def __init__(self, bp, sp, bs, ss, d):
        self.bp, self.sp, self.bs, self.ss, self.d = bp, sp, bs, ss, d
        self.tp = bp * sp
        self.t = self.tp + bs * ss
        self.nb = bp + bs
        self.nb_pad = -(-self.nb // 8) * 8
        self.row_gcd = math.gcd(sp, ss)

    def batch_of_tile(self, i, rows):
        npt = self.tp // rows
        return jnp.where(i < npt, i // (self.sp // rows), self.bp + (i - npt) // (self.ss // rows))

    def mod_spec(self, k, rows):
        return pl.BlockSpec((1, 1, self.d), lambda i, *_: (k * self.nb_pad + self.batch_of_tile(i, rows), 0, 0))


def _inproj_kernel(x_ref, g_ref, scale_ref, shift_ref, w_ref, cs_ref, z_ref, h_scr):
    @pl.when(pl.program_id(1) == 0)
    def _():
        h = _rms(x_ref[...]) * g_ref[...] * (1.0 + scale_ref[0]) + shift_ref[0]
        h_scr[...] = h.astype(BF16)

    z = (jnp.dot(h_scr[...], w_ref[...], preferred_element_type=F32) * cs_ref[...]).astype(BF16)
    for s in range(z_ref.shape[0]):
        z_ref[s] = z[:, s * LANES:(s + 1) * LANES]


def _inproj(geom, x, g_pre, mod_l, w_in_l, col_scale):
    t, d = x.shape
    n = w_in_l.shape[1]
    tm = _tile(geom.row_gcd, INPROJ_ROWS, ROW_ALIGN)
    tn = _tile(n, INPROJ_COLS, LANES)
    return pl.pallas_call(
        _inproj_kernel,
        grid=(t // tm, n // tn),
        in_specs=[
            pl.BlockSpec((tm, d), lambda i, j: (i, 0)),
            _row_spec(d),
            geom.mod_spec(1, tm),
            geom.mod_spec(0, tm),
            pl.BlockSpec((d, tn), lambda i, j: (0, j)),
            pl.BlockSpec((1, tn), lambda i, j: (0, j)),
        ],
        out_specs=pl.BlockSpec((tn // LANES, tm, LANES), lambda i, j: (j, i, 0)),
        out_shape=jax.ShapeDtypeStruct((n // LANES, t, LANES), BF16),
        scratch_shapes=[pltpu.VMEM((tm, d), BF16)],
        compiler_params=_params(("arbitrary", "arbitrary")),
        name="in_proj",
    )(x, g_pre, mod_l, mod_l, w_in_l, col_scale)


def _attn_scores(s_ref, it, q_ref, k_ref, qc_ref, kc_ref, bd_ref, ed0_ref, d0_ref, dc_ref, head_dim):
    _, tq, v_dim = q_ref.shape
    n_sub = k_ref.shape[1] // tq
    q = q_ref[0]
    lane = lax.broadcasted_iota(jnp.int32, (tq, v_dim), 1)
    half = (lane < head_dim, lane >= head_dim)
    d0 = d0_ref[it]
    d0_v = jnp.full((1, v_dim), d0, jnp.int32).astype(F32)
    zero = jnp.zeros_like(q)
    for m in range(2):
        q_aug = jnp.where(half[m], q, zero) + qc_ref[0, m] + (d0_v * ed0_ref[m]).astype(BF16)
        for c in range(n_sub):
            delta = d0 - c * tq
            sgn = (delta > 0).astype(jnp.int32) - (delta < 0).astype(jnp.int32)
            sgn_v = jnp.full((1, v_dim), sgn, jnp.int32).astype(F32).astype(BF16)
            k_aug = jnp.where(half[m], k_ref[0, c * tq:(c + 1) * tq, :], sgn_v * kc_ref[0, m, c * tq:(c + 1) * tq, :])
            s_ref[m, :, c * tq:(c + 1) * tq] = lax.dot_general(
                q_aug, k_aug, (((1,), (1,)), ((), ())), preferred_element_type=F32)
    dc = dc_ref[it]
    flag = jnp.full((1, 1), (dc >= 0).astype(jnp.int32), jnp.int32).astype(F32)
    off = pl.multiple_of(jnp.maximum(dc, 0) * tq, tq)
    bias = flag * bd_ref[0]
    for m in range(2):
        s_ref[m, :, pl.ds(off, tq)] = s_ref[m, :, pl.ds(off, tq)] - bias


def _attn_softmax_pv(s_ref, is_first, v_ref, m_scr, acc_scr):
    _, chunk, v_dim = v_ref.shape
    ones_col = jnp.where(lax.broadcasted_iota(jnp.int32, (chunk, v_dim), 1) == 0, 1.0, 0.0).astype(BF16)
    v_aug = jnp.concatenate([v_ref[0], ones_col], axis=1)
    for m in range(2):
        s = s_ref[m]
        m_old = jnp.where(is_first, -jnp.inf, m_scr[m])
        acc_old = jnp.where(is_first, 0.0, acc_scr[m])
        m_new = jnp.maximum(m_old, jnp.max(s, axis=-1, keepdims=True))
        p = jnp.exp2(s - m_new).astype(BF16)
        alpha = jnp.exp2(m_old - m_new)
        acc_scr[m] = alpha * acc_old + jnp.dot(p, v_aug, preferred_element_type=F32)
        m_scr[m] = m_new


def _attn_kernel(head_ref, qblk_ref, kvblk_ref, first_ref, last_ref, d0_ref, dc_ref, nk_ref,
                 q_ref, k_ref, v_ref, qc_ref, kc_ref, bd_ref, ed0_ref, lam_ref, gsub_ref, o_ref,
                 s_even, s_odd, m_scr, acc_scr, *, head_dim):
    g = pl.program_id(0)
    n_kept = nk_ref[0]
    e_a = jnp.minimum(g, n_kept - 1)
    e_b = jnp.minimum(jnp.maximum(g - 1, 0), n_kept - 1)
    active = g <= n_kept
    v_dim = q_ref.shape[2]

    @pl.when(g == 0)
    def _():
        s_odd[...] = jnp.zeros(s_odd.shape, F32)
        m_scr[...] = jnp.zeros(m_scr.shape, F32)
        acc_scr[...] = jnp.zeros(acc_scr.shape, F32)

    is_first = jnp.full((1, 1), first_ref[e_b], jnp.int32) > 0

    def step(s_write, s_read):
        _attn_scores(s_write, e_a, q_ref, k_ref, qc_ref, kc_ref, bd_ref, ed0_ref, d0_ref, dc_ref, head_dim)
        _attn_softmax_pv(s_read, is_first, v_ref, m_scr, acc_scr)

    @pl.when(jnp.logical_and(active, g % 2 == 0))
    def _():
        step(s_even, s_odd)

    @pl.when(jnp.logical_and(active, g % 2 == 1))
    def _():
        step(s_odd, s_even)

    @pl.when(jnp.logical_and(active, last_ref[e_b] == 1))
    def _():
        lp = lam_ref[...]
        lam_init = lp[4:5, :1]
        lam = (jnp.exp(jnp.sum(lp[0:1] * lp[1:2], axis=-1, keepdims=True))
               - jnp.exp(jnp.sum(lp[2:3] * lp[3:4], axis=-1, keepdims=True)) + lam_init)
        a0 = acc_scr[0]
        a1 = acc_scr[1]
        o = a0[:, :v_dim] / a0[:, v_dim:v_dim + 1] - lam * (a1[:, :v_dim] / a1[:, v_dim:v_dim + 1])
        o_ref[0] = (_rms(o) * gsub_ref[...] * (1.0 - lam_init)).astype(o_ref.dtype)


def _attn_items(geom, tq, chunk):
    qblk, kvblk, d0s, dcs, dist = [], [], [], [], []
    for off, nb, s in ((0, geom.bp, geom.sp), (geom.tp, geom.bs, geom.ss)):
        for b in range(nb):
            for qt in range(s // tq):
                for c in range(s // chunk):
                    qblk.append((off + b * s + qt * tq) // tq)
                    kvblk.append((off + b * s + c * chunk) // chunk)
                    d0 = qt * tq - c * chunk
                    d0s.append(d0)
                    dcs.append(d0 // tq if 0 <= d0 < chunk else -1)
                    dist.append(0 if 0 <= d0 < chunk else (d0 - chunk + 1 if d0 > 0 else -d0 - tq + 1))
    names = ("qblk", "kvblk", "d0", "dc", "min_dist")
    return {n: np.asarray(a, np.int32) for n, a in zip(names, (qblk, kvblk, d0s, dcs, dist))}


def _attn_stats_kernel(q_ref, k_ref, o_ref, *, head_dim, tq):
    v_dim = q_ref.shape[2]
    r = lax.broadcasted_iota(jnp.int32, (v_dim, v_dim), 0)
    c = lax.broadcasted_iota(jnp.int32, (v_dim, v_dim), 1)
    sel = jnp.where(jnp.logical_or(jnp.logical_and(c == 0, r < head_dim), jnp.logical_and(c == 1, r >= head_dim)),
                    1.0, 0.0).astype(BF16)
    pad = jnp.zeros((o_ref.shape[2] - 3, v_dim), F32)
    for t in range(q_ref.shape[1] // tq):
        q = q_ref[0, t * tq:(t + 1) * tq, :].astype(F32)
        k = k_ref[0, t * tq:(t + 1) * tq, :].astype(F32)
        out_rows = []
        for x, reduce_rows in ((q * q, jnp.max), (k * k, jnp.max), (q * k, jnp.min)):
            per_row = jnp.dot(x.astype(BF16), sel, preferred_element_type=F32)
            out_rows.append(reduce_rows(per_row, axis=0, keepdims=True))
        o_ref[0, t] = jnp.concatenate(out_rows + [pad], axis=0)


def _attn_stats(qkv, n_heads, head_dim, tq, rows_per_step):
    t, v_dim = qkv.shape[1], qkv.shape[2]
    tiles = rows_per_step // tq
    return pl.pallas_call(
        functools.partial(_attn_stats_kernel, head_dim=head_dim, tq=tq),
        grid=(n_heads, t // rows_per_step),
        in_specs=[
            pl.BlockSpec((1, rows_per_step, v_dim), lambda h, i: (h, i, 0)),
            pl.BlockSpec((1, rows_per_step, v_dim), lambda h, i: (n_heads + h, i, 0)),
        ],
        out_specs=pl.BlockSpec((1, tiles, SUBLANES, LANES), lambda h, i: (h, i, 0, 0)),
        out_shape=jax.ShapeDtypeStruct((n_heads, t // tq, SUBLANES, LANES), F32),
        compiler_params=_params(("arbitrary", "arbitrary")),
        name="attn_stats",
    )(qkv, qkv)


def _attn_entries(items, stats, slopes_log2, n_sub):
    n_heads = stats.shape[0]
    n_items = items["qblk"].shape[0]
    q_norm = jnp.sqrt(stats[:, :, 0, 0:2])
    k_norm = jnp.sqrt(stats[:, :, 1, 0:2])
    self_min = stats[:, :, 2, 0:2]
    kv_tiles = items["kvblk"][:, None] * n_sub + np.arange(n_sub)[None, :]
    k_norm_item = jnp.max(k_norm[:, kv_tiles, :], axis=2)
    upper = (q_norm[:, items["qblk"], :] * k_norm_item * SKIP_NORM_SLACK
             - slopes_log2[:, None, None] * items["min_dist"][None, :, None].astype(np.float32))
    lower = self_min - SKIP_SELF_SLACK * q_norm * k_norm
    gap = upper - lower[:, items["qblk"], :]
    drop = jnp.logical_and(jnp.all(gap < -SKIP_LOG2_GAP, axis=-1), (items["min_dist"] > 0)[None, :])
    keep = jnp.logical_not(drop).reshape(-1)
    n_all = n_heads * n_items
    order = jnp.argsort(jnp.logical_not(keep), stable=True).astype(jnp.int32)
    n_kept = jnp.sum(keep.astype(jnp.int32))
    pos = jnp.arange(n_all, dtype=jnp.int32)
    src = order[jnp.minimum(pos, n_kept - 1)]
    head = src // n_items
    it = src % n_items
    qblk = jnp.asarray(items["qblk"])[it]
    tile_id = head * (int(items["qblk"].max()) + 1) + qblk
    first = jnp.logical_or(pos == 0, tile_id != jnp.roll(tile_id, 1))
    last = jnp.logical_or(pos >= n_kept - 1, tile_id != jnp.roll(tile_id, -1))
    i32 = lambda a: a.astype(jnp.int32)
    return (i32(head), qblk, jnp.asarray(items["kvblk"])[it], i32(first), i32(last),
            jnp.asarray(items["d0"])[it], jnp.asarray(items["dc"])[it], n_kept.reshape(1))


def _bf16_parts(x, n):
    parts, rest = [], np.asarray(x, np.float64)
    for _ in range(n):
        p = rest.astype(BF16).astype(np.float64)
        parts.append(p)
        rest = rest - p
    return parts


def _attn_consts(n_heads, head_dim, tq, chunk, max_d0):
    v_dim = 2 * head_dim
    slopes = 2.0 ** (-ALIBI_MAX * np.arange(1, n_heads + 1, dtype=np.float64) / n_heads) * LOG2E
    sl = _bf16_parts(slopes, SLOPE_PARTS)
    r = np.arange(tq)
    j = np.arange(chunk)
    i_parts = [None, (r // BF16_EXACT_INT) * BF16_EXACT_INT, r % BF16_EXACT_INT]
    j_parts = [(j // BF16_EXACT_INT) * BF16_EXACT_INT, j % BF16_EXACT_INT]
    assert chunk <= BF16_EXACT_INT ** 2 and max_d0 % BF16_EXACT_INT == 0 and max_d0 <= BF16_EXACT_INT ** 2
    n_cols = SLOPE_PARTS * (len(i_parts) + len(j_parts))
    assert n_cols <= head_dim
    qc = np.zeros((n_heads, 2, tq, v_dim), np.float64)
    kc = np.zeros((n_heads, 2, chunk, v_dim), np.float64)
    ed0 = np.zeros((2, 1, v_dim), np.float32)
    for m in range(2):
        base = head_dim if m == 0 else 0
        col = base
        for a in range(SLOPE_PARTS):
            for jp in j_parts:
                qc[:, m, :, col] = sl[a][:, None]
                kc[:, m, :, col] = jp[None, :]
                col += 1
            for ip in i_parts:
                if ip is None:
                    ed0[m, 0, col] = 1.0
                else:
                    qc[:, m, :, col] = ip[None, :]
                kc[:, m, :, col] = -sl[a][:, None]
                col += 1
    bd = slopes[:, None, None] * np.abs(r[:, None] - r[None, :])[None]
    for arr in (qc, kc):
        assert np.array_equal(arr.astype(BF16).astype(np.float64), arr)
    return (jnp.asarray(qc.astype(BF16)), jnp.asarray(kc.astype(BF16)),
            jnp.asarray(bd.astype(np.float32)), jnp.asarray(ed0), jnp.asarray(slopes.astype(np.float32)))


def _attention(geom, qkv, consts, items, lam_pack_l, g_subln_l, n_heads, head_dim, tq, chunk):
    t = qkv.shape[1]
    v_dim = 2 * head_dim
    qc, kc, bd, ed0, slopes_log2 = consts
    n_sub = chunk // tq
    stats = _attn_stats(qkv, n_heads, head_dim, tq, _tile(t, ATTN_STATS_ROWS, tq))
    entries = _attn_entries(items, stats, slopes_log2, n_sub)
    n_entries = entries[0].shape[0]
    k_head0 = n_heads
    v_head0 = 2 * n_heads

    def score_entry(g, nk):
        return jnp.minimum(g, nk[0] - 1)

    def finish_entry(g, nk):
        return jnp.minimum(jnp.maximum(g - 1, 0), nk[0] - 1)

    def q_map(g, hd, qb, kb, fi, la, d0, dc, nk):
        e = score_entry(g, nk)
        return hd[e], qb[e], 0

    def k_map(g, hd, qb, kb, fi, la, d0, dc, nk):
        e = score_entry(g, nk)
        return k_head0 + hd[e], kb[e], 0

    def v_map(g, hd, qb, kb, fi, la, d0, dc, nk):
        e = finish_entry(g, nk)
        return v_head0 + hd[e], kb[e], 0

    def o_map(g, hd, qb, kb, fi, la, d0, dc, nk):
        e = finish_entry(g, nk)
        return hd[e], qb[e], 0

    def head_map(n_trailing):
        def fn(g, hd, qb, kb, fi, la, d0, dc, nk):
            return (hd[score_entry(g, nk)],) + (0,) * n_trailing
        return fn

    grid_spec = pltpu.PrefetchScalarGridSpec(
        num_scalar_prefetch=8,
        grid=(n_entries + 1,),
        in_specs=[
            pl.BlockSpec((1, tq, v_dim), q_map),
            pl.BlockSpec((1, chunk, v_dim), k_map),
            pl.BlockSpec((1, chunk, v_dim), v_map),
            pl.BlockSpec((1, 2, tq, v_dim), head_map(3)),
            pl.BlockSpec((1, 2, chunk, v_dim), head_map(3)),
            pl.BlockSpec((1, tq, tq), head_map(2)),
            pl.BlockSpec((2, 1, v_dim), lambda g, *_: (0, 0, 0)),
            pl.BlockSpec((8, head_dim), lambda g, *_: (0, 0)),
            _row_spec(v_dim),
        ],
        out_specs=pl.BlockSpec((1, tq, v_dim), o_map),
        scratch_shapes=[
            pltpu.VMEM((2, tq, chunk), F32),
            pltpu.VMEM((2, tq, chunk), F32),
            pltpu.VMEM((2, tq, 1), F32),
            pltpu.VMEM((2, tq, 2 * v_dim), F32),
        ],
    )
    return pl.pallas_call(
        functools.partial(_attn_kernel, head_dim=head_dim),
        grid_spec=grid_spec,
        out_shape=jax.ShapeDtypeStruct((n_heads, t, v_dim), BF16),
        compiler_params=_params(("arbitrary",)),
        name="diff_attn",
    )(*entries, qkv, qkv, qkv, qc, kc, bd, ed0, lam_pack_l, g_subln_l)


def _outproj_kernel(attn_ref, cb_ref, cc_ref, cx_ref, ga_ref, gc_ref,
                    ccp_ref, cxp_ref, ccn_ref, cxn_ref,
                    x_ref, convw_ref, wout_ref, gpost_ref, gatem_ref, gpre_ref, scalef_ref, shiftf_ref,
                    *rest, geom, moe):
    if moe:
        wr_ref, xo_ref, h_ref, tope_ref, topw_ref = rest
    else:
        xo_ref, h_ref = rest
    tm = x_ref.shape[0]
    row0 = pl.program_id(0) * tm
    seq = jnp.where(row0 < geom.tp, geom.sp, geom.ss)
    rel0 = jnp.where(row0 < geom.tp, row0, row0 - geom.tp)
    not_start = (rel0 % seq != 0).astype(F32)
    not_end = ((rel0 + tm) % seq != 0).astype(F32)

    def rows(ref, lo=0, hi=None):
        hi = ref.shape[1] if hi is None else hi
        return jnp.concatenate([ref[s, lo:hi, :] for s in range(ref.shape[0])], axis=1).astype(F32)

    u = rows(cc_ref) * rows(cx_ref)
    halo = ccp_ref.shape[1]
    u_prev = (rows(ccp_ref, halo - 1, halo) * rows(cxp_ref, halo - 1, halo)) * not_start
    u_next = (rows(ccn_ref, 0, 1) * rows(cxn_ref, 0, 1)) * not_end
    row = lax.broadcasted_iota(jnp.int32, u.shape, 0)
    u_m1 = jnp.where(row == 0, u_prev, pltpu.roll(u, 1, 0))
    u_p1 = jnp.where(row == tm - 1, u_next, pltpu.roll(u, tm - 1, 0))
    cw = convw_ref[...]
    conv = u_m1 * cw[0:1] + u * cw[1:2] + u_p1 * cw[2:3]
    short = rows(cb_ref) * conv
    merged = jax.nn.sigmoid(rows(ga_ref)) * rows(attn_ref) + jax.nn.sigmoid(rows(gc_ref)) * short
    o = jnp.dot(merged.astype(BF16), wout_ref[...], preferred_element_type=F32)
    x_new = x_ref[...] + gatem_ref[0] * (_rms(o) * gpost_ref[...])
    xo_ref[...] = x_new
    h = _rms(x_new) * gpre_ref[...] * (1.0 + scalef_ref[0]) + shiftf_ref[0]
    h_ref[...] = h.astype(h_ref.dtype)

    if moe:
        n_e = wr_ref.shape[1]
        wr = wr_ref[...]
        h_hi = h.astype(BF16)
        h_lo = (h - h_hi.astype(F32)).astype(BF16)
        w_hi = wr.astype(BF16)
        w_lo = (wr - w_hi.astype(F32)).astype(BF16)
        logits = (jnp.dot(h_hi, w_hi, preferred_element_type=F32) + jnp.dot(h_lo, w_hi, preferred_element_type=F32)
                  + jnp.dot(h_hi, w_lo, preferred_element_type=F32))
        lane = lax.broadcasted_iota(jnp.int32, logits.shape, 1).astype(F32)
        m1 = jnp.max(logits, axis=-1, keepdims=True)
        i1 = jnp.min(jnp.where(logits == m1, lane, float(n_e)), axis=-1, keepdims=True)
        rest_l = jnp.where(lane == i1, -jnp.inf, logits)
        m2 = jnp.max(rest_l, axis=-1, keepdims=True)
        i2 = jnp.min(jnp.where(rest_l == m2, lane, float(n_e)), axis=-1, keepdims=True)
        e = jnp.exp(m2 - m1)
        w1 = 1.0 / (1.0 + e)
        w2 = e / (1.0 + e)
        tope_ref[...] = jnp.where(lane == 0.0, i1, jnp.where(lane == 1.0, i2, 0.0)).astype(jnp.int32)
        topw_ref[...] = jnp.where(lane == 0.0, w1, jnp.where(lane == 1.0, w2, 0.0))


def _outproj(geom, attn, z, x, conv_w_l, w_out_l, g_post, g_pre, mod_l, w_router_l):
    t, d = x.shape
    moe = w_router_l is not None
    tm = _tile(geom.row_gcd, OUTPROJ_ROWS, ROW_ALIGN)
    halo = ROW_ALIGN
    nh = tm // halo
    last_halo = t // halo - 1
    n_heads, _, v_dim = attn.shape
    spt = d // LANES
    zcol = lambda c: pl.BlockSpec((spt, tm, LANES), lambda i: (c, i, 0))
    zhalo = lambda c, fn: pl.BlockSpec((spt, halo, LANES), lambda i: (c, fn(i), 0))
    prev_blk = lambda i: jnp.maximum(i * nh - 1, 0)
    next_blk = lambda i: jnp.minimum((i + 1) * nh, last_halo)
    in_specs = [
        pl.BlockSpec((n_heads, tm, v_dim), lambda i: (0, i, 0)),
        zcol(3), zcol(4), zcol(5), zcol(6), zcol(7),
        zhalo(4, prev_blk), zhalo(5, prev_blk), zhalo(4, next_blk), zhalo(5, next_blk),
        pl.BlockSpec((tm, d), lambda i: (i, 0)),
        pl.BlockSpec(conv_w_l.shape, lambda i: (0, 0)),
        pl.BlockSpec((d, d), lambda i: (0, 0)),
        _row_spec(d),
        geom.mod_spec(2, tm),
        _row_spec(d),
        geom.mod_spec(4, tm), geom.mod_spec(3, tm),
    ]
    args = [attn, z, z, z, z, z, z, z, z, z, x, conv_w_l, w_out_l, g_post, mod_l, g_pre, mod_l, mod_l]
    out_specs = [pl.BlockSpec((tm, d), lambda i: (i, 0)), pl.BlockSpec((tm, d), lambda i: (i, 0))]
    out_shape = [jax.ShapeDtypeStruct((t, d), F32), jax.ShapeDtypeStruct((t, d), F32 if moe else BF16)]
    if moe:
        n_e = w_router_l.shape[1]
        in_specs.append(pl.BlockSpec((d, n_e), lambda i: (0, 0)))
        args.append(w_router_l)
        out_specs += [pl.BlockSpec((tm, n_e), lambda i: (i, 0))] * 2
        out_shape += [jax.ShapeDtypeStruct((t, n_e), jnp.int32), jax.ShapeDtypeStruct((t, n_e), F32)]
    return pl.pallas_call(
        functools.partial(_outproj_kernel, geom=geom, moe=moe),
        grid=(t // tm,),
        in_specs=in_specs,
        out_specs=out_specs,
        out_shape=out_shape,
        compiler_params=_params(("arbitrary",)),
        name="out_proj_moe" if moe else "out_proj",
    )(*args)


def _swiglu_partial(h, wg, wu, wd):
    g = jnp.dot(h, wg, preferred_element_type=F32)
    u = jnp.dot(h, wu, preferred_element_type=F32)
    a = (g * jax.nn.sigmoid(g)) * u
    return jnp.dot(a.astype(BF16), wd, preferred_element_type=F32)


def _ffn_kernel(h_ref, x_ref, wg_ref, wu_ref, wd_ref, gpost_ref, gate_ref, xo_ref, acc_scr):
    f = pl.program_id(1)
    part = _swiglu_partial(h_ref[...], wg_ref[...], wu_ref[...], wd_ref[...])

    @pl.when(f == 0)
    def _():
        acc_scr[...] = part

    @pl.when(f > 0)
    def _():
        acc_scr[...] += part

    @pl.when(f == pl.num_programs(1) - 1)
    def _():
        xo_ref[...] = x_ref[...] + gate_ref[0] * (_rms(acc_scr[...]) * gpost_ref[...])


def _ffn(geom, h, x, wg, wu, wd, g_post, mod_l):
    t, d = x.shape
    ff = wg.shape[1]
    tm = _tile(geom.row_gcd, FFN_ROWS, ROW_ALIGN)
    tf = _tile(ff, FFN_COLS, LANES)
    return pl.pallas_call(
        _ffn_kernel,
        grid=(t // tm, ff // tf),
        in_specs=[
            pl.BlockSpec((tm, d), lambda i, f: (i, 0)),
            pl.BlockSpec((tm, d), lambda i, f: (i, 0)),
            pl.BlockSpec((d, tf), lambda i, f: (0, f)),
            pl.BlockSpec((d, tf), lambda i, f: (0, f)),
            pl.BlockSpec((tf, d), lambda i, f: (f, 0)),
            _row_spec(d),
            geom.mod_spec(5, tm),
        ],
        out_specs=pl.BlockSpec((tm, d), lambda i, f: (i, 0)),
        out_shape=jax.ShapeDtypeStruct((t, d), F32),
        scratch_shapes=[pltpu.VMEM((tm, d), F32)],
        compiler_params=_params(("arbitrary", "arbitrary")),
        name="dense_ffn",
    )(h, x, wg, wu, wd, g_post, mod_l)


def _expert_kernel(be_ref, idx0_ref, idxn_ref, h_ref, wg_ref, wu_ref, wd_ref, y_ref,
                   xf_scr, xb_scr, sem):
    b = pl.program_id(0)
    f = pl.program_id(1)
    n_blk = pl.num_programs(0)
    n_f = pl.num_programs(1)
    blk, d = xb_scr.shape
    slot = b % 2

    def row_copy(idx_ref, r, s, tile, sub):
        return pltpu.make_async_copy(h_ref.at[pl.ds(idx_ref[r], 1)], xf_scr.at[s, tile, pl.ds(sub, 1)], sem.at[s])

    def block_copy(s):
        return pltpu.make_async_copy(xf_scr.at[s], xf_scr.at[s], sem.at[s])

    @pl.when(jnp.logical_and(b == 0, f == 0))
    def _():
        def issue(r, carry):
            row_copy(idx0_ref, r, 0, r // SUBLANES, r % SUBLANES).start()
            return carry

        lax.fori_loop(0, blk, issue, 0, unroll=DMA_ISSUE_UNROLL)

    @pl.when(f == 0)
    def _():
        block_copy(slot).wait()
        xb_scr[...] = xf_scr[slot].reshape(blk, d).astype(BF16)

    def prefetch_slice():
        per_step = blk // EXPERT_F_STEPS
        for u in range(per_step):
            row_copy(idxn_ref, f * per_step + u, 1 - slot,
                     f * (per_step // SUBLANES) + u // SUBLANES, u % SUBLANES).start(priority=u % DMA_PRIORITIES)

    part = _swiglu_partial(xb_scr[...], wg_ref[0], wu_ref[0], wd_ref[0])

    @pl.when(f == 0)
    def _():
        prefetch_slice()
        y_ref[...] = part

    @pl.when(f > 0)
    def _():
        prefetch_slice()
        y_ref[...] += part

    @pl.when(jnp.logical_and(b == n_blk - 1, f == n_f - 1))
    def _():
        block_copy(1 - slot).wait()


def _experts(h, row_tok, block_e, wg, wu, wd, blk):
    n_pad = row_tok.shape[0]
    d = h.shape[1]
    ff = wg.shape[2]
    tf = ff // EXPERT_F_STEPS
    n_blk = n_pad // blk
    assert ff % EXPERT_F_STEPS == 0 and tf % LANES == 0 and blk % SMEM_INDEX_ALIGN == 0
    assert blk % (EXPERT_F_STEPS * SUBLANES) == 0 and h.dtype == F32
    grid_spec = pltpu.PrefetchScalarGridSpec(
        num_scalar_prefetch=1,
        grid=(n_blk, EXPERT_F_STEPS),
        in_specs=[
            pl.BlockSpec((blk,), lambda b, f, be: (0,), memory_space=pltpu.SMEM),
            pl.BlockSpec((blk,), lambda b, f, be: (jnp.minimum(b + 1, n_blk - 1),), memory_space=pltpu.SMEM),
            pl.BlockSpec(memory_space=pl.ANY),
            pl.BlockSpec((1, d, tf), lambda b, f, be: (be[b], 0, f)),
            pl.BlockSpec((1, d, tf), lambda b, f, be: (be[b], 0, f)),
            pl.BlockSpec((1, tf, d), lambda b, f, be: (be[b], f, 0)),
        ],
        out_specs=pl.BlockSpec((blk, d), lambda b, f, be: (b, 0)),
        scratch_shapes=[
            pltpu.VMEM((2, blk // SUBLANES, SUBLANES, d), F32),
            pltpu.VMEM((blk, d), BF16),
            pltpu.SemaphoreType.DMA((2,)),
        ],
    )
    return pl.pallas_call(
        _expert_kernel,
        grid_spec=grid_spec,
        out_shape=jax.ShapeDtypeStruct((n_pad, d), F32),
        compiler_params=_params(("arbitrary", "arbitrary")),
        name="expert_ffn",
    )(block_e, row_tok, row_tok, h, wg, wu, wd)


def _combine_kernel(dest_ref, y_ref, w_ref, x_ref, gpost_ref, gate_ref, xo_ref, buf, sem):
    tc, d = x_ref.shape

    def issue(tile, carry):
        for sub in range(SUBLANES):
            for k in range(TOP_K):
                src_row = dest_ref[TOP_K * (tile * SUBLANES + sub) + k]
                pltpu.make_async_copy(y_ref.at[pl.ds(src_row, 1)], buf.at[k, tile, pl.ds(sub, 1)],
                                      sem).start(priority=k % DMA_PRIORITIES)
        return carry

    lax.fori_loop(0, tc // SUBLANES, issue, 0)
    pltpu.make_async_copy(buf, buf, sem).wait()
    w = w_ref[...]
    o = w[:, 0:1] * buf[0].reshape(tc, d)
    for k in range(1, TOP_K):
        o = o + w[:, k:k + 1] * buf[k].reshape(tc, d)
    xo_ref[...] = x_ref[...] + gate_ref[0] * (_rms(o) * gpost_ref[...])


def _combine(geom, y_rows, dest, top_w, x, g_post, mod_l):
    t, d = x.shape
    n_e = top_w.shape[1]
    tc = _tile(geom.row_gcd, COMBINE_ROWS, SMEM_INDEX_ALIGN // TOP_K)
    return pl.pallas_call(
        _combine_kernel,
        grid=(t // tc,),
        in_specs=[
            pl.BlockSpec((TOP_K * tc,), lambda i: (i,), memory_space=pltpu.SMEM),
            pl.BlockSpec(memory_space=pl.ANY),
            pl.BlockSpec((tc, n_e), lambda i: (i, 0)),
            pl.BlockSpec((tc, d), lambda i: (i, 0)),
            _row_spec(d),
            geom.mod_spec(5, tc),
        ],
        out_specs=pl.BlockSpec((tc, d), lambda i: (i, 0)),
        out_shape=jax.ShapeDtypeStruct((t, d), F32),
        scratch_shapes=[pltpu.VMEM((TOP_K, tc // SUBLANES, SUBLANES, d), F32), pltpu.SemaphoreType.DMA(())],
        compiler_params=_params(("arbitrary",)),
        name="moe_combine",
    )(dest, y_rows, top_w, x, g_post, mod_l)


def _route(top_e, n_experts, blk):
    t = top_e.shape[0]
    n_assign = t * TOP_K
    n_blocks = -(-(n_assign + n_experts * (blk - 1)) // blk)
    n_pad = n_blocks * blk
    flat_e = top_e[:, :TOP_K].reshape(-1)
    onehot = (flat_e[:, None] == jnp.arange(n_experts, dtype=jnp.int32)[None, :]).astype(jnp.int32)
    rank = jnp.sum((jnp.cumsum(onehot, axis=0) - onehot) * onehot, axis=1)
    counts = jnp.sum(onehot, axis=0)
    padded = (counts + blk - 1) // blk * blk
    end_pad = jnp.cumsum(padded)
    start_pad = end_pad - padded
    dest = (start_pad[flat_e] + rank).astype(jnp.int32)
    flat_tok = jnp.arange(n_assign, dtype=jnp.int32) // TOP_K
    row_tok = jnp.zeros((n_pad,), jnp.int32).at[dest].set(flat_tok)
    block_e = jnp.minimum(
        jnp.searchsorted(end_pad, jnp.arange(n_blocks, dtype=jnp.int32) * blk, side='right'),
        n_experts - 1).astype(jnp.int32)
    return dest, row_tok, block_e


def kernel(x_prompt, x_sample, c_prompt, c_sample, w_ada, b_ada, g_mix_pre, g_mix_post, g_ffn_pre, g_ffn_post,
           w_in, lam_q1, lam_k1, lam_q2, lam_k2, g_subln, conv_w, w_out, w_ffn_gate, w_ffn_up, w_ffn_down,
           w_router, w_exp_gate, w_exp_up, w_exp_down):
    bp, sp, d = x_prompt.shape
    bs, ss, _ = x_sample.shape
    depth = w_in.shape[0]
    head_dim = lam_q1.shape[1]
    v_dim = g_subln.shape[1]
    n_heads = d // v_dim
    n_experts = w_router.shape[2]
    n_in = w_in.shape[2]
    assert v_dim == 2 * head_dim and v_dim == LANES and n_in == 8 * d
    geom = _Geom(bp, sp, bs, ss, d)

    x = jnp.concatenate([x_prompt.reshape(bp * sp, d), x_sample.reshape(bs * ss, d)], axis=0)
    c_all = jnp.concatenate([c_prompt, c_sample, jnp.zeros((geom.nb_pad - geom.nb, d), F32)], axis=0)
    mod = _ada(c_all, w_ada, b_ada)

    tq = _tile(geom.row_gcd, ATTN_Q_ROWS, ROW_ALIGN)
    chunk = _tile(geom.row_gcd, ATTN_KV_CHUNK, tq)
    attn_consts = _attn_consts(n_heads, head_dim, tq, chunk, max(sp, ss))
    attn_items = _attn_items(geom, tq, chunk)
    col_scale = jnp.asarray(np.where(np.arange(n_in) < n_heads * v_dim, head_dim ** -0.5 * LOG2E, 1.0)
                            .astype(np.float32))[None, :]
    lam_init = np.asarray([0.8 - 0.6 * math.exp(-0.3 * l) for l in range(depth)], np.float32)
    lam_pack = jnp.stack(
        [lam_q1, lam_k1, lam_q2, lam_k2, jnp.broadcast_to(jnp.asarray(lam_init)[:, None], lam_q1.shape)]
        + [jnp.zeros_like(lam_q1)] * 3, axis=1)

    per_layer = dict(mod=mod, g_mix_pre=g_mix_pre[:, None], g_mix_post=g_mix_post[:, None],
                     g_ffn_pre=g_ffn_pre[:, None], g_ffn_post=g_ffn_post[:, None], g_subln=g_subln[:, None],
                     w_in=w_in, lam=lam_pack, conv_w=conv_w, w_out=w_out)

    def mixer(x, p, w_router_l):
        z = _inproj(geom, x, p["g_mix_pre"], p["mod"], p["w_in"].astype(BF16), col_scale)
        attn = _attention(geom, z, attn_consts, attn_items, p["lam"], p["g_subln"], n_heads, head_dim, tq, chunk)
        return _outproj(geom, attn, z, x, p["conv_w"], p["w_out"].astype(BF16), p["g_mix_post"], p["g_ffn_pre"],
                        p["mod"], w_router_l)

    def dense_layer(x, p, w):
        x, h = mixer(x, p, None)
        return _ffn(geom, h, x, w["gate"].astype(BF16), w["up"].astype(BF16), w["down"].astype(BF16),
                    p["g_ffn_post"], p["mod"])

    def expert_layer(x, p, w):
        x, h, top_e, top_w = mixer(x, p, w["router"])
        dest, row_tok, block_e = _route(top_e, n_experts, MOE_BLOCK)
        y_rows = _experts(h, row_tok, block_e, w["gate"].astype(BF16), w["up"].astype(BF16),
                          w["down"].astype(BF16), MOE_BLOCK)
        return _combine(geom, y_rows, dest, top_w, x, p["g_ffn_post"], p["mod"])

    n_pairs = depth // 2
    even = jax.tree.map(lambda a: a[0:2 * n_pairs:2], per_layer)
    odd = jax.tree.map(lambda a: a[1:2 * n_pairs:2], per_layer)
    dense_w = dict(gate=w_ffn_gate, up=w_ffn_up, down=w_ffn_down)
    exp_w = dict(router=w_router, gate=w_exp_gate, up=w_exp_up, down=w_exp_down)

    def pair(x, xs):
        p_even, p_odd, dw, ew = xs
        x = dense_layer(x, p_even, dw)
        x = expert_layer(x, p_odd, ew)
        return x, None

    x, _ = lax.scan(pair, x, (even, odd, jax.tree.map(lambda a: a[:n_pairs], dense_w), exp_w))
    if depth % 2 == 1:
        x = dense_layer(x, jax.tree.map(lambda a: a[depth - 1], per_layer),
                        jax.tree.map(lambda a: a[n_pairs], dense_w))

    y_prompt = x[:geom.tp].reshape(bp, sp, d)
    y_sample = x[geom.tp:].reshape(bs, ss, d)
    return (y_prompt, y_sample)
```

```python
import functools
import math

import numpy as np
import jax
import jax.numpy as jnp
from jax import lax
from jax.experimental import pallas as pl
from jax.experimental.pallas import tpu as pltpu

F32 = jnp.float32
BF16 = jnp.bfloat16

ALIBI_MAX = 8.0
NORM_EPS = 1e-6
TOP_K = 2
N_MOD = 6
LOG2E = math.log2(math.e)

VMEM_LIMIT_BYTES = 56 * 1024 * 1024
ROW_ALIGN = 16
SUBLANES = 8
LANES = 128
BF16_EXACT_INT = 256
SLOPE_PARTS = 3
SKIP_LOG2_GAP = 160.0
SKIP_NORM_SLACK = 1.01
SKIP_SELF_SLACK = 2.0 ** -7

INPROJ_ROWS, INPROJ_COLS = 1024, 2048
ATTN_Q_ROWS, ATTN_KV_CHUNK = 512, 2048
ATTN_STATS_ROWS = 8192
OUTPROJ_ROWS = 256
FFN_ROWS, FFN_COLS = 512, 2816
MOE_BLOCK = 1024
EXPERT_F_STEPS = 4
COMBINE_ROWS = 512
DMA_ISSUE_UNROLL = 8
DMA_PRIORITIES = 2
SMEM_INDEX_ALIGN = 1024


def _tile(n, pref, align=1):
    t = min(n, pref)
    while t > 0:
        if n % t == 0 and t % align == 0:
            return t
        t -= 1
    raise ValueError(f"no tile for {n} (pref {pref}, align {align})")


def _params(sem):
    return pltpu.CompilerParams(dimension_semantics=sem, vmem_limit_bytes=VMEM_LIMIT_BYTES)


def _rms(x):
    return x * lax.rsqrt(jnp.mean(x * x, axis=-1, keepdims=True) + NORM_EPS)


def _row_spec(d, fn=None):
    return pl.BlockSpec((1, d), fn if fn is not None else (lambda *_: (0, 0)))


def _ada_kernel(c_ref, w_ref, b_ref, o_ref):
    c = c_ref[...]
    cond = c * jax.nn.sigmoid(c)
    o_ref[0] = jnp.dot(cond, w_ref[0], preferred_element_type=F32,
                       precision=lax.Precision.HIGHEST) + b_ref[0]


def _ada(c_all, w_ada, b_ada):
    depth, d, _ = w_ada.shape
    bp = c_all.shape[0]
    b3 = b_ada.reshape(depth * N_MOD, 1, d)
    out = pl.pallas_call(
        _ada_kernel,
        grid=(depth, N_MOD),
        in_specs=[
            pl.BlockSpec((bp, d), lambda l, k: (0, 0)),
            pl.BlockSpec((1, d, d), lambda l, k: (l, 0, k)),
            pl.BlockSpec((1, 1, d), lambda l, k: (l * N_MOD + k, 0, 0)),
        ],
        out_specs=pl.BlockSpec((1, bp, d), lambda l, k: (l * N_MOD + k, 0, 0)),
        out_shape=jax.ShapeDtypeStruct((depth * N_MOD, bp, d), F32),
        compiler_params=_params(("arbitrary", "arbitrary")),
        name="ada_mod",
    )(c_all, w_ada, b3)
    return out.reshape(depth, N_MOD * bp, 1, d)


class _Geom:
    def __init__(self, bp, sp, bs, ss, d):
        self.bp, self.sp, self.bs, self.ss, self.d = bp, sp, bs, ss, d
        self.tp = bp * sp
        self.t = self.tp + bs * ss
        self.nb = bp + bs
        self.nb_pad = -(-self.nb // 8) * 8
        self.row_gcd = math.gcd(sp, ss)

    def batch_of_tile(self, i, rows):
        npt = self.tp // rows
        return jnp.where(i < npt, i // (self.sp // rows), self.bp + (i - npt) // (self.ss // rows))

    def mod_spec(self, k, rows):
        return pl.BlockSpec((1, 1, self.d), lambda i, *_: (k * self.nb_pad + self.batch_of_tile(i, rows), 0, 0))


def _inproj_kernel(x_ref, g_ref, scale_ref, shift_ref, w_ref, cs_ref, z_ref, h_scr):
    @pl.when(pl.program_id(1) == 0)
    def _():
        h = _rms(x_ref[...]) * g_ref[...] * (1.0 + scale_ref[0]) + shift_ref[0]
        h_scr[...] = h.astype(BF16)

    z = (jnp.dot(h_scr[...], w_ref[...], preferred_element_type=F32) * cs_ref[...]).astype(BF16)
    for s in range(z_ref.shape[0]):
        z_ref[s] = z[:, s * LANES:(s + 1) * LANES]


def _inproj(geom, x, g_pre, mod_l, w_in_l, col_scale):
    t, d = x.shape
    n = w_in_l.shape[1]
    tm = _tile(geom.row_gcd, INPROJ_ROWS, ROW_ALIGN)
    tn = _tile(n, INPROJ_COLS, LANES)
    return pl.pallas_call(
        _inproj_kernel,
        grid=(t // tm, n // tn),
        in_specs=[
            pl.BlockSpec((tm, d), lambda i, j: (i, 0)),
            _row_spec(d),
            geom.mod_spec(1, tm),
            geom.mod_spec(0, tm),
            pl.BlockSpec((d, tn), lambda i, j: (0, j)),
            pl.BlockSpec((1, tn), lambda i, j: (0, j)),
        ],
        out_specs=pl.BlockSpec((tn // LANES, tm, LANES), lambda i, j: (j, i, 0)),
        out_shape=jax.ShapeDtypeStruct((n // LANES, t, LANES), BF16),
        scratch_shapes=[pltpu.VMEM((tm, d), BF16)],
        compiler_params=_params(("arbitrary", "arbitrary")),
        name="in_proj",
    )(x, g_pre, mod_l, mod_l, w_in_l, col_scale)


def _attn_scores(s_ref, it, q_ref, k_ref, qc_ref, kc_ref, bd_ref, ed0_ref, d0_ref, dc_ref, head_dim):
    _, tq, v_dim = q_ref.shape
    n_sub = k_ref.shape[1] // tq
    q = q_ref[0]
    lane = lax.broadcasted_iota(jnp.int32, (tq, v_dim), 1)
    half = (lane < head_dim, lane >= head_dim)
    d0 = d0_ref[it]
    d0_v = jnp.full((1, v_dim), d0, jnp.int32).astype(F32)
    zero = jnp.zeros_like(q)
    for m in range(2):
        q_aug = jnp.where(half[m], q, zero) + qc_ref[0, m] + (d0_v * ed0_ref[m]).astype(BF16)
        for c in range(n_sub):
            delta = d0 - c * tq
            sgn = (delta > 0).astype(jnp.int32) - (delta < 0).astype(jnp.int32)
            sgn_v = jnp.full((1, v_dim), sgn, jnp.int32).astype(F32).astype(BF16)
            k_aug = jnp.where(half[m], k_ref[0, c * tq:(c + 1) * tq, :], sgn_v * kc_ref[0, m, c * tq:(c + 1) * tq, :])
            s_ref[m, :, c * tq:(c + 1) * tq] = lax.dot_general(
                q_aug, k_aug, (((1,), (1,)), ((), ())), preferred_element_type=F32)
    dc = dc_ref[it]
    flag = jnp.full((1, 1), (dc >= 0).astype(jnp.int32), jnp.int32).astype(F32)
    off = pl.multiple_of(jnp.maximum(dc, 0) * tq, tq)
    bias = flag * bd_ref[0]
    for m in range(2):
        s_ref[m, :, pl.ds(off, tq)] = s_ref[m, :, pl.ds(off, tq)] - bias


def _attn_softmax_pv(s_ref, is_first, v_ref, m_scr, acc_scr):
    _, chunk, v_dim = v_ref.shape
    ones_col = jnp.where(lax.broadcasted_iota(jnp.int32, (chunk, v_dim), 1) == 0, 1.0, 0.0).astype(BF16)
    v_aug = jnp.concatenate([v_ref[0], ones_col], axis=1)
    for m in range(2):
        s = s_ref[m]
        m_old = jnp.where(is_first, -jnp.inf, m_scr[m])
        acc_old = jnp.where(is_first, 0.0, acc_scr[m])
        m_new = jnp.maximum(m_old, jnp.max(s, axis=-1, keepdims=True))
        p = jnp.exp2(s - m_new).astype(BF16)
        alpha = jnp.exp2(m_old - m_new)
        acc_scr[m] = alpha * acc_old + jnp.dot(p, v_aug, preferred_element_type=F32)
        m_scr[m] = m_new


def _attn_kernel(head_ref, qblk_ref, kvblk_ref, first_ref, last_ref, d0_ref, dc_ref, nk_ref,
                 q_ref, k_ref, v_ref, qc_ref, kc_ref, bd_ref, ed0_ref, lam_ref, gsub_ref, o_ref,
                 s_even, s_odd, m_scr, acc_scr, *, head_dim):
    g = pl.program_id(0)
    n_kept = nk_ref[0]
    e_a = jnp.minimum(g, n_kept - 1)
    e_b = jnp.minimum(jnp.maximum(g - 1, 0), n_kept - 1)
    active = g <= n_kept
    v_dim = q_ref.shape[2]

    @pl.when(g == 0)
    def _():
        s_odd[...] = jnp.zeros(s_odd.shape, F32)
        m_scr[...] = jnp.zeros(m_scr.shape, F32)
        acc_scr[...] = jnp.zeros(acc_scr.shape, F32)

    is_first = jnp.full((1, 1), first_ref[e_b], jnp.int32) > 0

    def step(s_write, s_read):
        _attn_scores(s_write, e_a, q_ref, k_ref, qc_ref, kc_ref, bd_ref, ed0_ref, d0_ref, dc_ref, head_dim)
        _attn_softmax_pv(s_read, is_first, v_ref, m_scr, acc_scr)

    @pl.when(jnp.logical_and(active, g % 2 == 0))
    def _():
        step(s_even, s_odd)

    @pl.when(jnp.logical_and(active, g % 2 == 1))
    def _():
        step(s_odd, s_even)

    @pl.when(jnp.logical_and(active, last_ref[e_b] == 1))
    def _():
        lp = lam_ref[...]
        lam_init = lp[4:5, :1]
        lam = (jnp.exp(jnp.sum(lp[0:1] * lp[1:2], axis=-1, keepdims=True))
               - jnp.exp(jnp.sum(lp[2:3] * lp[3:4], axis=-1, keepdims=True)) + lam_init)
        a0 = acc_scr[0]
        a1 = acc_scr[1]
        o = a0[:, :v_dim] / a0[:, v_dim:v_dim + 1] - lam * (a1[:, :v_dim] / a1[:, v_dim:v_dim + 1])
        o_ref[0] = (_rms(o) * gsub_ref[...] * (1.0 - lam_init)).astype(o_ref.dtype)


def _attn_items(geom, tq, chunk):
    qblk, kvblk, d0s, dcs, dist = [], [], [], [], []
    for off, nb, s in ((0, geom.bp, geom.sp), (geom.tp, geom.bs, geom.ss)):
        for b in range(nb):
            for qt in range(s // tq):
                for c in range(s // chunk):
                    qblk.append((off + b * s + qt * tq) // tq)
                    kvblk.append((off + b * s + c * chunk) // chunk)
                    d0 = qt * tq - c * chunk
                    d0s.append(d0)
                    dcs.append(d0 // tq if 0 <= d0 < chunk else -1)
                    dist.append(0 if 0 <= d0 < chunk else (d0 - chunk + 1 if d0 > 0 else -d0 - tq + 1))
    names = ("qblk", "kvblk", "d0", "dc", "min_dist")
    return {n: np.asarray(a, np.int32) for n, a in zip(names, (qblk, kvblk, d0s, dcs, dist))}


def _attn_stats_kernel(q_ref, k_ref, o_ref, *, head_dim, tq):
    v_dim = q_ref.shape[2]
    r = lax.broadcasted_iota(jnp.int32, (v_dim, v_dim), 0)
    c = lax.broadcasted_iota(jnp.int32, (v_dim, v_dim), 1)
    sel = jnp.where(jnp.logical_or(jnp.logical_and(c == 0, r < head_dim), jnp.logical_and(c == 1, r >= head_dim)),
                    1.0, 0.0).astype(BF16)
    pad = jnp.zeros((o_ref.shape[2] - 3, v_dim), F32)
    for t in range(q_ref.shape[1] // tq):
        q = q_ref[0, t * tq:(t + 1) * tq, :].astype(F32)
        k = k_ref[0, t * tq:(t + 1) * tq, :].astype(F32)
        out_rows = []
        for x, reduce_rows in ((q * q, jnp.max), (k * k, jnp.max), (q * k, jnp.min)):
            per_row = jnp.dot(x.astype(BF16), sel, preferred_element_type=F32)
            out_rows.append(reduce_rows(per_row, axis=0, keepdims=True))
        o_ref[0, t] = jnp.concatenate(out_rows + [pad], axis=0)


def _attn_stats(qkv, n_heads, head_dim, tq, rows_per_step):
    t, v_dim = qkv.shape[1], qkv.shape[2]
    tiles = rows_per_step // tq
    return pl.pallas_call(
        functools.partial(_attn_stats_kernel, head_dim=head_dim, tq=tq),
        grid=(n_heads, t // rows_per_step),
        in_specs=[
            pl.BlockSpec((1, rows_per_step, v_dim), lambda h, i: (h, i, 0)),
            pl.BlockSpec((1, rows_per_step, v_dim), lambda h, i: (n_heads + h, i, 0)),
        ],
        out_specs=pl.BlockSpec((1, tiles, SUBLANES, LANES), lambda h, i: (h, i, 0, 0)),
        out_shape=jax.ShapeDtypeStruct((n_heads, t // tq, SUBLANES, LANES), F32),
        compiler_params=_params(("arbitrary", "arbitrary")),
        name="attn_stats",
    )(qkv, qkv)


def _attn_entries(items, stats, slopes_log2, n_sub):
    n_heads = stats.shape[0]
    n_items = items["qblk"].shape[0]
    q_norm = jnp.sqrt(stats[:, :, 0, 0:2])
    k_norm = jnp.sqrt(stats[:, :, 1, 0:2])
    self_min = stats[:, :, 2, 0:2]
    kv_tiles = items["kvblk"][:, None] * n_sub + np.arange(n_sub)[None, :]
    k_norm_item = jnp.max(k_norm[:, kv_tiles, :], axis=2)
    upper = (q_norm[:, items["qblk"], :] * k_norm_item * SKIP_NORM_SLACK
             - slopes_log2[:, None, None] * items["min_dist"][None, :, None].astype(np.float32))
    lower = self_min - SKIP_SELF_SLACK * q_norm * k_norm
    gap = upper - lower[:, items["qblk"], :]
    drop = jnp.logical_and(jnp.all(gap < -SKIP_LOG2_GAP, axis=-1), (items["min_dist"] > 0)[None, :])
    keep = jnp.logical_not(drop).reshape(-1)
    n_all = n_heads * n_items
    order = jnp.argsort(jnp.logical_not(keep), stable=True).astype(jnp.int32)
    n_kept = jnp.sum(keep.astype(jnp.int32))
    pos = jnp.arange(n_all, dtype=jnp.int32)
    src = order[jnp.minimum(pos, n_kept - 1)]
    head = src // n_items
    it = src % n_items
    qblk = jnp.asarray(items["qblk"])[it]
    tile_id = head * (int(items["qblk"].max()) + 1) + qblk
    first = jnp.logical_or(pos == 0, tile_id != jnp.roll(tile_id, 1))
    last = jnp.logical_or(pos >= n_kept - 1, tile_id != jnp.roll(tile_id, -1))
    i32 = lambda a: a.astype(jnp.int32)
    return (i32(head), qblk, jnp.asarray(items["kvblk"])[it], i32(first), i32(last),
            jnp.asarray(items["d0"])[it], jnp.asarray(items["dc"])[it], n_kept.reshape(1))


def _bf16_parts(x, n):
    parts, rest = [], np.asarray(x, np.float64)
    for _ in range(n):
        p = rest.astype(BF16).astype(np.float64)
        parts.append(p)
        rest = rest - p
    return parts


def _attn_consts(n_heads, head_dim, tq, chunk, max_d0):
    v_dim = 2 * head_dim
    slopes = 2.0 ** (-ALIBI_MAX * np.arange(1, n_heads + 1, dtype=np.float64) / n_heads) * LOG2E
    sl = _bf16_parts(slopes, SLOPE_PARTS)
    r = np.arange(tq)
    j = np.arange(chunk)
    i_parts = [None, (r // BF16_EXACT_INT) * BF16_EXACT_INT, r % BF16_EXACT_INT]
    j_parts = [(j // BF16_EXACT_INT) * BF16_EXACT_INT, j % BF16_EXACT_INT]
    assert chunk <= BF16_EXACT_INT ** 2 and max_d0 % BF16_EXACT_INT == 0 and max_d0 <= BF16_EXACT_INT ** 2
    n_cols = SLOPE_PARTS * (len(i_parts) + len(j_parts))
    assert n_cols <= head_dim
    qc = np.zeros((n_heads, 2, tq, v_dim), np.float64)
    kc = np.zeros((n_heads, 2, chunk, v_dim), np.float64)
    ed0 = np.zeros((2, 1, v_dim), np.float32)
    for m in range(2):
        base = head_dim if m == 0 else 0
        col = base
        for a in range(SLOPE_PARTS):
            for jp in j_parts:
                qc[:, m, :, col] = sl[a][:, None]
                kc[:, m, :, col] = jp[None, :]
                col += 1
            for ip in i_parts:
                if ip is None:
                    ed0[m, 0, col] = 1.0
                else:
                    qc[:, m, :, col] = ip[None, :]
                kc[:, m, :, col] = -sl[a][:, None]
                col += 1
    bd = slopes[:, None, None] * np.abs(r[:, None] - r[None, :])[None]
    for arr in (qc, kc):
        assert np.array_equal(arr.astype(BF16).astype(np.float64), arr)
    return (jnp.asarray(qc.astype(BF16)), jnp.asarray(kc.astype(BF16)),
            jnp.asarray(bd.astype(np.float32)), jnp.asarray(ed0), jnp.asarray(slopes.astype(np.float32)))


def _attention(geom, qkv, consts, items, lam_pack_l, g_subln_l, n_heads, head_dim, tq, chunk):
    t = qkv.shape[1]
    v_dim = 2 * head_dim
    qc, kc, bd, ed0, slopes_log2 = consts
    n_sub = chunk // tq
    stats = _attn_stats(qkv, n_heads, head_dim, tq, _tile(t, ATTN_STATS_ROWS, tq))
    entries = _attn_entries(items, stats, slopes_log2, n_sub)
    n_entries = entries[0].shape[0]
    k_head0 = n_heads
    v_head0 = 2 * n_heads

    def score_entry(g, nk):
        return jnp.minimum(g, nk[0] - 1)

    def finish_entry(g, nk):
        return jnp.minimum(jnp.maximum(g - 1, 0), nk[0] - 1)

    def q_map(g, hd, qb, kb, fi, la, d0, dc, nk):
        e = score_entry(g, nk)
        return hd[e], qb[e], 0

    def k_map(g, hd, qb, kb, fi, la, d0, dc, nk):
        e = score_entry(g, nk)
        return k_head0 + hd[e], kb[e], 0

    def v_map(g, hd, qb, kb, fi, la, d0, dc, nk):
        e = finish_entry(g, nk)
        return v_head0 + hd[e], kb[e], 0

    def o_map(g, hd, qb, kb, fi, la, d0, dc, nk):
        e = finish_entry(g, nk)
        return hd[e], qb[e], 0

    def head_map(n_trailing):
        def fn(g, hd, qb, kb, fi, la, d0, dc, nk):
            return (hd[score_entry(g, nk)],) + (0,) * n_trailing
        return fn

    grid_spec = pltpu.PrefetchScalarGridSpec(
        num_scalar_prefetch=8,
        grid=(n_entries + 1,),
        in_specs=[
            pl.BlockSpec((1, tq, v_dim), q_map),
            pl.BlockSpec((1, chunk, v_dim), k_map),
            pl.BlockSpec((1, chunk, v_dim), v_map),
            pl.BlockSpec((1, 2, tq, v_dim), head_map(3)),
            pl.BlockSpec((1, 2, chunk, v_dim), head_map(3)),
            pl.BlockSpec((1, tq, tq), head_map(2)),
            pl.BlockSpec((2, 1, v_dim), lambda g, *_: (0, 0, 0)),
            pl.BlockSpec((8, head_dim), lambda g, *_: (0, 0)),
            _row_spec(v_dim),
        ],
        out_specs=pl.BlockSpec((1, tq, v_dim), o_map),
        scratch_shapes=[
            pltpu.VMEM((2, tq, chunk), F32),
            pltpu.VMEM((2, tq, chunk), F32),
            pltpu.VMEM((2, tq, 1), F32),
            pltpu.VMEM((2, tq, 2 * v_dim), F32),
        ],
    )
    return pl.pallas_call(
        functools.partial(_attn_kernel, head_dim=head_dim),
        grid_spec=grid_spec,
        out_shape=jax.ShapeDtypeStruct((n_heads, t, v_dim), BF16),
        compiler_params=_params(("arbitrary",)),
        name="diff_attn",
    )(*entries, qkv, qkv, qkv, qc, kc, bd, ed0, lam_pack_l, g_subln_l)


def _outproj_kernel(attn_ref, cb_ref, cc_ref, cx_ref, ga_ref, gc_ref,
                    ccp_ref, cxp_ref, ccn_ref, cxn_ref,
                    x_ref, convw_ref, wout_ref, gpost_ref, gatem_ref, gpre_ref, scalef_ref, shiftf_ref,
                    *rest, geom, moe):
    if moe:
        wr_ref, xo_ref, h_ref, tope_ref, topw_ref = rest
    else:
        xo_ref, h_ref = rest
    tm = x_ref.shape[0]
    row0 = pl.program_id(0) * tm
    seq = jnp.where(row0 < geom.tp, geom.sp, geom.ss)
    rel0 = jnp.where(row0 < geom.tp, row0, row0 - geom.tp)
    not_start = (rel0 % seq != 0).astype(F32)
    not_end = ((rel0 + tm) % seq != 0).astype(F32)

    def rows(ref, lo=0, hi=None):
        hi = ref.shape[1] if hi is None else hi
        return jnp.concatenate([ref[s, lo:hi, :] for s in range(ref.shape[0])], axis=1).astype(F32)

    u = rows(cc_ref) * rows(cx_ref)
    halo = ccp_ref.shape[1]
    u_prev = (rows(ccp_ref, halo - 1, halo) * rows(cxp_ref, halo - 1, halo)) * not_start
    u_next = (rows(ccn_ref, 0, 1) * rows(cxn_ref, 0, 1)) * not_end
    row = lax.broadcasted_iota(jnp.int32, u.shape, 0)
    u_m1 = jnp.where(row == 0, u_prev, pltpu.roll(u, 1, 0))
    u_p1 = jnp.where(row == tm - 1, u_next, pltpu.roll(u, tm - 1, 0))
    cw = convw_ref[...]
    conv = u_m1 * cw[0:1] + u * cw[1:2] + u_p1 * cw[2:3]
    short = rows(cb_ref) * conv
    merged = jax.nn.sigmoid(rows(ga_ref)) * rows(attn_ref) + jax.nn.sigmoid(rows(gc_ref)) * short
    o = jnp.dot(merged.astype(BF16), wout_ref[...], preferred_element_type=F32)
    x_new = x_ref[...] + gatem_ref[0] * (_rms(o) * gpost_ref[...])
    xo_ref[...] = x_new
    h = _rms(x_new) * gpre_ref[...] * (1.0 + scalef_ref[0]) + shiftf_ref[0]
    h_ref[...] = h.astype(h_ref.dtype)

    if moe:
        n_e = wr_ref.shape[1]
        wr = wr_ref[...]
        h_hi = h.astype(BF16)
        h_lo = (h - h_hi.astype(F32)).astype(BF16)
        w_hi = wr.astype(BF16)
        w_lo = (wr - w_hi.astype(F32)).astype(BF16)
        logits = (jnp.dot(h_hi, w_hi, preferred_element_type=F32) + jnp.dot(h_lo, w_hi, preferred_element_type=F32)
                  + jnp.dot(h_hi, w_lo, preferred_element_type=F32))
        lane = lax.broadcasted_iota(jnp.int32, logits.shape, 1).astype(F32)
        m1 = jnp.max(logits, axis=-1, keepdims=True)
        i1 = jnp.min(jnp.where(logits == m1, lane, float(n_e)), axis=-1, keepdims=True)
        rest_l = jnp.where(lane == i1, -jnp.inf, logits)
        m2 = jnp.max(rest_l, axis=-1, keepdims=True)
        i2 = jnp.min(jnp.where(rest_l == m2, lane, float(n_e)), axis=-1, keepdims=True)
        e = jnp.exp(m2 - m1)
        w1 = 1.0 / (1.0 + e)
        w2 = e / (1.0 + e)
        tope_ref[...] = jnp.where(lane == 0.0, i1, jnp.where(lane == 1.0, i2, 0.0)).astype(jnp.int32)
        topw_ref[...] = jnp.where(lane == 0.0, w1, jnp.where(lane == 1.0, w2, 0.0))


def _outproj(geom, attn, z, x, conv_w_l, w_out_l, g_post, g_pre, mod_l, w_router_l):
    t, d = x.shape
    moe = w_router_l is not None
    tm = _tile(geom.row_gcd, OUTPROJ_ROWS, ROW_ALIGN)
    halo = ROW_ALIGN
    nh = tm // halo
    last_halo = t // halo - 1
    n_heads, _, v_dim = attn.shape
    spt = d // LANES
    zcol = lambda c: pl.BlockSpec((spt, tm, LANES), lambda i: (c, i, 0))
    zhalo = lambda c, fn: pl.BlockSpec((spt, halo, LANES), lambda i: (c, fn(i), 0))
    prev_blk = lambda i: jnp.maximum(i * nh - 1, 0)
    next_blk = lambda i: jnp.minimum((i + 1) * nh, last_halo)
    in_specs = [
        pl.BlockSpec((n_heads, tm, v_dim), lambda i: (0, i, 0)),
        zcol(3), zcol(4), zcol(5), zcol(6), zcol(7),
        zhalo(4, prev_blk), zhalo(5, prev_blk), zhalo(4, next_blk), zhalo(5, next_blk),
        pl.BlockSpec((tm, d), lambda i: (i, 0)),
        pl.BlockSpec(conv_w_l.shape, lambda i: (0, 0)),
        pl.BlockSpec((d, d), lambda i: (0, 0)),
        _row_spec(d),
        geom.mod_spec(2, tm),
        _row_spec(d),
        geom.mod_spec(4, tm), geom.mod_spec(3, tm),
    ]
    args = [attn, z, z, z, z, z, z, z, z, z, x, conv_w_l, w_out_l, g_post, mod_l, g_pre, mod_l, mod_l]
    out_specs = [pl.BlockSpec((tm, d), lambda i: (i, 0)), pl.BlockSpec((tm, d), lambda i: (i, 0))]
    out_shape = [jax.ShapeDtypeStruct((t, d), F32), jax.ShapeDtypeStruct((t, d), F32 if moe else BF16)]
    if moe:
        n_e = w_router_l.shape[1]
        in_specs.append(pl.BlockSpec((d, n_e), lambda i: (0, 0)))
        args.append(w_router_l)
        out_specs += [pl.BlockSpec((tm, n_e), lambda i: (i, 0))] * 2
        out_shape += [jax.ShapeDtypeStruct((t, n_e), jnp.int32), jax.ShapeDtypeStruct((t, n_e), F32)]
    return pl.pallas_call(
        functools.partial(_outproj_kernel, geom=geom, moe=moe),
        grid=(t // tm,),
        in_specs=in_specs,
        out_specs=out_specs,
        out_shape=out_shape,
        compiler_params=_params(("arbitrary",)),
        name="out_proj_moe" if moe else "out_proj",
    )(*args)


def _swiglu_partial(h, wg, wu, wd):
    g = jnp.dot(h, wg, preferred_element_type=F32)
    u = jnp.dot(h, wu, preferred_element_type=F32)
    a = (g * jax.nn.sigmoid(g)) * u
    return jnp.dot(a.astype(BF16), wd, preferred_element_type=F32)


def _ffn_kernel(h_ref, x_ref, wg_ref, wu_ref, wd_ref, gpost_ref, gate_ref, xo_ref, acc_scr, *, n_f):
    f = pl.program_id(1)
    part = _swiglu_partial(h_ref[...], wg_ref[...], wu_ref[...], wd_ref[...])
    if n_f == 1:
        xo_ref[...] = x_ref[...] + gate_ref[0] * (_rms(part) * gpost_ref[...])
        return

    @pl.when(f == 0)
    def _():
        acc_scr[...] = part

    @pl.when(f > 0)
    def _():
        acc_scr[...] += part

    @pl.when(f == pl.num_programs(1) - 1)
    def _():
        xo_ref[...] = x_ref[...] + gate_ref[0] * (_rms(acc_scr[...]) * gpost_ref[...])


def _ffn(geom, h, x, wg, wu, wd, g_post, mod_l):
    t, d = x.shape
    ff = wg.shape[1]
    tm = _tile(geom.row_gcd, FFN_ROWS, ROW_ALIGN)
    tf = _tile(ff, FFN_COLS, LANES)
    n_f = ff // tf
    weight_mode = dict(pipeline_mode=pl.Buffered(1)) if n_f == 1 else {}
    return pl.pallas_call(
        functools.partial(_ffn_kernel, n_f=n_f),
        grid=(t // tm, n_f),
        in_specs=[
            pl.BlockSpec((tm, d), lambda i, f: (i, 0)),
            pl.BlockSpec((tm, d), lambda i, f: (i, 0)),
            pl.BlockSpec((d, tf), lambda i, f: (0, f), **weight_mode),
            pl.BlockSpec((d, tf), lambda i, f: (0, f), **weight_mode),
            pl.BlockSpec((tf, d), lambda i, f: (f, 0), **weight_mode),
            _row_spec(d),
            geom.mod_spec(5, tm),
        ],
        out_specs=pl.BlockSpec((tm, d), lambda i, f: (i, 0)),
        out_shape=jax.ShapeDtypeStruct((t, d), F32),
        scratch_shapes=[pltpu.VMEM((tm, d), F32)],
        compiler_params=_params(("arbitrary", "arbitrary")),
        name="dense_ffn",
    )(h, x, wg, wu, wd, g_post, mod_l)


def _expert_kernel(be_ref, idx0_ref, idxn_ref, h_ref, wg_ref, wu_ref, wd_ref, y_ref,
                   xf_scr, xb_scr, sem):
    b = pl.program_id(0)
    f = pl.program_id(1)
    n_blk = pl.num_programs(0)
    n_f = pl.num_programs(1)
    blk, d = xb_scr.shape
    slot = b % 2

    def row_copy(idx_ref, r, s, tile, sub):
        return pltpu.make_async_copy(h_ref.at[pl.ds(idx_ref[r], 1)], xf_scr.at[s, tile, pl.ds(sub, 1)], sem.at[s])

    def block_copy(s):
        return pltpu.make_async_copy(xf_scr.at[s], xf_scr.at[s], sem.at[s])

    @pl.when(jnp.logical_and(b == 0, f == 0))
    def _():
        def issue(r, carry):
            row_copy(idx0_ref, r, 0, r // SUBLANES, r % SUBLANES).start()
            return carry

        lax.fori_loop(0, blk, issue, 0, unroll=DMA_ISSUE_UNROLL)

    @pl.when(f == 0)
    def _():
        block_copy(slot).wait()
        xb_scr[...] = xf_scr[slot].reshape(blk, d).astype(BF16)

    def prefetch_slice():
        per_step = blk // EXPERT_F_STEPS
        for u in range(per_step):
            row_copy(idxn_ref, f * per_step + u, 1 - slot,
                     f * (per_step // SUBLANES) + u // SUBLANES, u % SUBLANES).start(priority=u % DMA_PRIORITIES)

    part = _swiglu_partial(xb_scr[...], wg_ref[0], wu_ref[0], wd_ref[0])

    @pl.when(f == 0)
    def _():
        prefetch_slice()
        y_ref[...] = part

    @pl.when(f > 0)
    def _():
        prefetch_slice()
        y_ref[...] += part

    @pl.when(jnp.logical_and(b == n_blk - 1, f == n_f - 1))
    def _():
        block_copy(1 - slot).wait()


def _experts(h, row_tok, block_e, wg, wu, wd, blk):
    n_pad = row_tok.shape[0]
    d = h.shape[1]
    ff = wg.shape[2]
    tf = ff // EXPERT_F_STEPS
    n_blk = n_pad // blk
    assert ff % EXPERT_F_STEPS == 0 and tf % LANES == 0 and blk % SMEM_INDEX_ALIGN == 0
    assert blk % (EXPERT_F_STEPS * SUBLANES) == 0 and h.dtype == F32
    grid_spec = pltpu.PrefetchScalarGridSpec(
        num_scalar_prefetch=1,
        grid=(n_blk, EXPERT_F_STEPS),
        in_specs=[
            pl.BlockSpec((blk,), lambda b, f, be: (0,), memory_space=pltpu.SMEM),
            pl.BlockSpec((blk,), lambda b, f, be: (jnp.minimum(b + 1, n_blk - 1),), memory_space=pltpu.SMEM),
            pl.BlockSpec(memory_space=pl.ANY),
            pl.BlockSpec((1, d, tf), lambda b, f, be: (be[b], 0, f)),
            pl.BlockSpec((1, d, tf), lambda b, f, be: (be[b], 0, f)),
            pl.BlockSpec((1, tf, d), lambda b, f, be: (be[b], f, 0)),
        ],
        out_specs=pl.BlockSpec((blk, d), lambda b, f, be: (b, 0)),
        scratch_shapes=[
            pltpu.VMEM((2, blk // SUBLANES, SUBLANES, d), F32),
            pltpu.VMEM((blk, d), BF16),
            pltpu.SemaphoreType.DMA((2,)),
        ],
    )
    return pl.pallas_call(
        _expert_kernel,
        grid_spec=grid_spec,
        out_shape=jax.ShapeDtypeStruct((n_pad, d), F32),
        compiler_params=_params(("arbitrary", "arbitrary")),
        name="expert_ffn",
    )(block_e, row_tok, row_tok, h, wg, wu, wd)


def _combine_kernel(dest_ref, y_ref, w_ref, x_ref, gpost_ref, gate_ref, xo_ref, buf, sem):
    tc, d = x_ref.shape

    def issue(tile, carry):
        for sub in range(SUBLANES):
            for k in range(TOP_K):
                src_row = dest_ref[TOP_K * (tile * SUBLANES + sub) + k]
                pltpu.make_async_copy(y_ref.at[pl.ds(src_row, 1)], buf.at[k, tile, pl.ds(sub, 1)],
                                      sem).start(priority=k % DMA_PRIORITIES)
        return carry

    lax.fori_loop(0, tc // SUBLANES, issue, 0)
    pltpu.make_async_copy(buf, buf, sem).wait()
    w = w_ref[...]
    o = w[:, 0:1] * buf[0].reshape(tc, d)
    for k in range(1, TOP_K):
        o = o + w[:, k:k + 1] * buf[k].reshape(tc, d)
    xo_ref[...] = x_ref[...] + gate_ref[0] * (_rms(o) * gpost_ref[...])


def _combine(geom, y_rows, dest, top_w, x, g_post, mod_l):
    t, d = x.shape
    n_e = top_w.shape[1]
    tc = _tile(geom.row_gcd, COMBINE_ROWS, SMEM_INDEX_ALIGN // TOP_K)
    return pl.pallas_call(
        _combine_kernel,
        grid=(t // tc,),
        in_specs=[
            pl.BlockSpec((TOP_K * tc,), lambda i: (i,), memory_space=pltpu.SMEM),
            pl.BlockSpec(memory_space=pl.ANY),
            pl.BlockSpec((tc, n_e), lambda i: (i, 0)),
            pl.BlockSpec((tc, d), lambda i: (i, 0)),
            _row_spec(d),
            geom.mod_spec(5, tc),
        ],
        out_specs=pl.BlockSpec((tc, d), lambda i: (i, 0)),
        out_shape=jax.ShapeDtypeStruct((t, d), F32),
        scratch_shapes=[pltpu.VMEM((TOP_K, tc // SUBLANES, SUBLANES, d), F32), pltpu.SemaphoreType.DMA(())],
        compiler_params=_params(("arbitrary",)),
        name="moe_combine",
    )(dest, y_rows, top_w, x, g_post, mod_l)


def _route(top_e, n_experts, blk):
    t = top_e.shape[0]
    n_assign = t * TOP_K
    n_blocks = -(-(n_assign + n_experts * (blk - 1)) // blk)
    n_pad = n_blocks * blk
    flat_e = top_e[:, :TOP_K].reshape(-1)
    onehot = (flat_e[:, None] == jnp.arange(n_experts, dtype=jnp.int32)[None, :]).astype(jnp.int32)
    rank = jnp.sum((jnp.cumsum(onehot, axis=0) - onehot) * onehot, axis=1)
    counts = jnp.sum(onehot, axis=0)
    padded = (counts + blk - 1) // blk * blk
    end_pad = jnp.cumsum(padded)
    start_pad = end_pad - padded
    dest = (start_pad[flat_e] + rank).astype(jnp.int32)
    flat_tok = jnp.arange(n_assign, dtype=jnp.int32) // TOP_K
    row_tok = jnp.zeros((n_pad,), jnp.int32).at[dest].set(flat_tok)
    block_e = jnp.minimum(
        jnp.searchsorted(end_pad, jnp.arange(n_blocks, dtype=jnp.int32) * blk, side='right'),
        n_experts - 1).astype(jnp.int32)
    return dest, row_tok, block_e


def kernel(x_prompt, x_sample, c_prompt, c_sample, w_ada, b_ada, g_mix_pre, g_mix_post, g_ffn_pre, g_ffn_post,
           w_in, lam_q1, lam_k1, lam_q2, lam_k2, g_subln, conv_w, w_out, w_ffn_gate, w_ffn_up, w_ffn_down,
           w_router, w_exp_gate, w_exp_up, w_exp_down):
    bp, sp, d = x_prompt.shape
    bs, ss, _ = x_sample.shape
    depth = w_in.shape[0]
    head_dim = lam_q1.shape[1]
    v_dim = g_subln.shape[1]
    n_heads = d // v_dim
    n_experts = w_router.shape[2]
    n_in = w_in.shape[2]
    assert v_dim == 2 * head_dim and v_dim == LANES and n_in == 8 * d
    geom = _Geom(bp, sp, bs, ss, d)

    x = jnp.concatenate([x_prompt.reshape(bp * sp, d), x_sample.reshape(bs * ss, d)], axis=0)
    c_all = jnp.concatenate([c_prompt, c_sample, jnp.zeros((geom.nb_pad - geom.nb, d), F32)], axis=0)
    mod = _ada(c_all, w_ada, b_ada)

    tq = _tile(geom.row_gcd, ATTN_Q_ROWS, ROW_ALIGN)
    chunk = _tile(geom.row_gcd, ATTN_KV_CHUNK, tq)
    attn_consts = _attn_consts(n_heads, head_dim, tq, chunk, max(sp, ss))
    attn_items = _attn_items(geom, tq, chunk)
    col_scale = jnp.asarray(np.where(np.arange(n_in) < n_heads * v_dim, head_dim ** -0.5 * LOG2E, 1.0)
                            .astype(np.float32))[None, :]
    lam_init = np.asarray([0.8 - 0.6 * math.exp(-0.3 * l) for l in range(depth)], np.float32)
    lam_pack = jnp.stack(
        [lam_q1, lam_k1, lam_q2, lam_k2, jnp.broadcast_to(jnp.asarray(lam_init)[:, None], lam_q1.shape)]
        + [jnp.zeros_like(lam_q1)] * 3, axis=1)

    per_layer = dict(mod=mod, g_mix_pre=g_mix_pre[:, None], g_mix_post=g_mix_post[:, None],
                     g_ffn_pre=g_ffn_pre[:, None], g_ffn_post=g_ffn_post[:, None], g_subln=g_subln[:, None],
                     w_in=w_in, lam=lam_pack, conv_w=conv_w, w_out=w_out)

    def mixer(x, p, w_router_l):
        z = _inproj(geom, x, p["g_mix_pre"], p["mod"], p["w_in"].astype(BF16), col_scale)
        attn = _attention(geom, z, attn_consts, attn_items, p["lam"], p["g_subln"], n_heads, head_dim, tq, chunk)
        return _outproj(geom, attn, z, x, p["conv_w"], p["w_out"].astype(BF16), p["g_mix_post"], p["g_ffn_pre"],
                        p["mod"], w_router_l)

    def dense_layer(x, p, w):
        x, h = mixer(x, p, None)
        return _ffn(geom, h, x, w["gate"].astype(BF16), w["up"].astype(BF16), w["down"].astype(BF16),
                    p["g_ffn_post"], p["mod"])

    def expert_layer(x, p, w):
        x, h, top_e, top_w = mixer(x, p, w["router"])
        dest, row_tok, block_e = _route(top_e, n_experts, MOE_BLOCK)
        y_rows = _experts(h, row_tok, block_e, w["gate"].astype(BF16), w["up"].astype(BF16),
                          w["down"].astype(BF16), MOE_BLOCK)
        return _combine(geom, y_rows, dest, top_w, x, p["g_ffn_post"], p["mod"])

    n_pairs = depth // 2
    even = jax.tree.map(lambda a: a[0:2 * n_pairs:2], per_layer)
    odd = jax.tree.map(lambda a: a[1:2 * n_pairs:2], per_layer)
    dense_w = dict(gate=w_ffn_gate, up=w_ffn_up, down=w_ffn_down)
    exp_w = dict(router=w_router, gate=w_exp_gate, up=w_exp_up, down=w_exp_down)

    def pair(x, xs):
        p_even, p_odd, dw, ew = xs
        x = dense_layer(x, p_even, dw)
        x = expert_layer(x, p_odd, ew)
        return x, None

    x, _ = lax.scan(pair, x, (even, odd, jax.tree.map(lambda a: a[:n_pairs], dense_w), exp_w))
    if depth % 2 == 1:
        x = dense_layer(x, jax.tree.map(lambda a: a[depth - 1], per_layer),
                        jax.tree.map(lambda a: a[n_pairs], dense_w))

    y_prompt = x[:geom.tp].reshape(bp, sp, d)
    y_sample = x[geom.tp:].reshape(bs, ss, d)
    return (y_prompt, y_sample)
```

```python
import functools
import math

import numpy as np
import jax
import jax.numpy as jnp
from jax import lax
from jax.experimental import pallas as pl
from jax.experimental.pallas import tpu as pltpu

F32 = jnp.float32
BF16 = jnp.bfloat16

ALIBI_MAX = 8.0
NORM_EPS = 1e-6
TOP_K = 2
N_MOD = 6
LOG2E = math.log2(math.e)

VMEM_LIMIT_BYTES = 56 * 1024 * 1024
ROW_ALIGN = 16
SUBLANES = 8
LANES = 128
BF16_EXACT_INT = 256
SLOPE_PARTS = 3
SKIP_LOG2_GAP = 160.0
SKIP_NORM_SLACK = 1.01
SKIP_SELF_SLACK = 2.0 ** -7

INPROJ_ROWS, INPROJ_COLS = 1024, 2048
ATTN_Q_ROWS, ATTN_KV_CHUNK = 512, 2048
ATTN_STATS_ROWS = 8192
OUTPROJ_ROWS = 256
FFN_ROWS, FFN_COLS = 512, 2816
MOE_BLOCK = 1024
EXPERT_F_STEPS = 2
COMBINE_ROWS = 512
DMA_ISSUE_UNROLL = 8
DMA_PRIORITIES = 2
SMEM_INDEX_ALIGN = 1024


def _tile(n, pref, align=1):
    t = min(n, pref)
    while t > 0:
        if n % t == 0 and t % align == 0:
            return t
        t -= 1
    raise ValueError(f"no tile for {n} (pref {pref}, align {align})")


def _params(sem):
    return pltpu.CompilerParams(dimension_semantics=sem, vmem_limit_bytes=VMEM_LIMIT_BYTES)


def _rms(x):
    return x * lax.rsqrt(jnp.mean(x * x, axis=-1, keepdims=True) + NORM_EPS)


def _row_spec(d, fn=None):
    return pl.BlockSpec((1, d), fn if fn is not None else (lambda *_: (0, 0)))


def _ada_kernel(c_ref, w_ref, b_ref, o_ref):
    c = c_ref[...]
    cond = c * jax.nn.sigmoid(c)
    o_ref[0] = jnp.dot(cond, w_ref[0], preferred_element_type=F32,
                       precision=lax.Precision.HIGHEST) + b_ref[0]


def _ada(c_all, w_ada, b_ada):
    depth, d, _ = w_ada.shape
    bp = c_all.shape[0]
    b3 = b_ada.reshape(depth * N_MOD, 1, d)
    out = pl.pallas_call(
        _ada_kernel,
        grid=(depth, N_MOD),
        in_specs=[
            pl.BlockSpec((bp, d), lambda l, k: (0, 0)),
            pl.BlockSpec((1, d, d), lambda l, k: (l, 0, k)),
            pl.BlockSpec((1, 1, d), lambda l, k: (l * N_MOD + k, 0, 0)),
        ],
        out_specs=pl.BlockSpec((1, bp, d), lambda l, k: (l * N_MOD + k, 0, 0)),
        out_shape=jax.ShapeDtypeStruct((depth * N_MOD, bp, d), F32),
        compiler_params=_params(("arbitrary", "arbitrary")),
        name="ada_mod",
    )(c_all, w_ada, b3)
    return out.reshape(depth, N_MOD * bp, 1, d)


class _Geom:
    def __init__(self, bp, sp, bs, ss, d):
        self.bp, self.sp, self.bs, self.ss, self.d = bp, sp, bs, ss, d
        self.tp = bp * sp
        self.t = self.tp + bs * ss
        self.nb = bp + bs
        self.nb_pad = -(-self.nb // 8) * 8
        self.row_gcd = math.gcd(sp, ss)

    def batch_of_tile(self, i, rows):
        npt = self.tp // rows
        return jnp.where(i < npt, i // (self.sp // rows), self.bp + (i - npt) // (self.ss // rows))

    def mod_spec(self, k, rows):
        return pl.BlockSpec((1, 1, self.d), lambda i, *_: (k * self.nb_pad + self.batch_of_tile(i, rows), 0, 0))


def _inproj_kernel(x_ref, g_ref, scale_ref, shift_ref, w_ref, cs_ref, z_ref, h_scr):
    @pl.when(pl.program_id(1) == 0)
    def _():
        h = _rms(x_ref[...]) * g_ref[...] * (1.0 + scale_ref[0]) + shift_ref[0]
        h_scr[...] = h.astype(BF16)

    z = (jnp.dot(h_scr[...], w_ref[...], preferred_element_type=F32) * cs_ref[...]).astype(BF16)
    for s in range(z_ref.shape[0]):
        z_ref[s] = z[:, s * LANES:(s + 1) * LANES]


def _inproj(geom, x, g_pre, mod_l, w_in_l, col_scale):
    t, d = x.shape
    n = w_in_l.shape[1]
    tm = _tile(geom.row_gcd, INPROJ_ROWS, ROW_ALIGN)
    tn = _tile(n, INPROJ_COLS, LANES)
    return pl.pallas_call(
        _inproj_kernel,
        grid=(t // tm, n // tn),
        in_specs=[
            pl.BlockSpec((tm, d), lambda i, j: (i, 0)),
            _row_spec(d),
            geom.mod_spec(1, tm),
            geom.mod_spec(0, tm),
            pl.BlockSpec((d, tn), lambda i, j: (0, j)),
            pl.BlockSpec((1, tn), lambda i, j: (0, j)),
        ],
        out_specs=pl.BlockSpec((tn // LANES, tm, LANES), lambda i, j: (j, i, 0)),
        out_shape=jax.ShapeDtypeStruct((n // LANES, t, LANES), BF16),
        scratch_shapes=[pltpu.VMEM((tm, d), BF16)],
        compiler_params=_params(("arbitrary", "arbitrary")),
        name="in_proj",
    )(x, g_pre, mod_l, mod_l, w_in_l, col_scale)


def _attn_scores(s_ref, it, q_ref, k_ref, qc_ref, kc_ref, bd_ref, ed0_ref, d0_ref, dc_ref, head_dim):
    _, tq, v_dim = q_ref.shape
    n_sub = k_ref.shape[1] // tq
    q = q_ref[0]
    lane = lax.broadcasted_iota(jnp.int32, (tq, v_dim), 1)
    half = (lane < head_dim, lane >= head_dim)
    d0 = d0_ref[it]
    d0_v = jnp.full((1, v_dim), d0, jnp.int32).astype(F32)
    zero = jnp.zeros_like(q)
    for m in range(2):
        q_aug = jnp.where(half[m], q, zero) + qc_ref[0, m] + (d0_v * ed0_ref[m]).astype(BF16)
        for c in range(n_sub):
            delta = d0 - c * tq
            sgn = (delta > 0).astype(jnp.int32) - (delta < 0).astype(jnp.int32)
            sgn_v = jnp.full((1, v_dim), sgn, jnp.int32).astype(F32).astype(BF16)
            k_aug = jnp.where(half[m], k_ref[0, c * tq:(c + 1) * tq, :], sgn_v * kc_ref[0, m, c * tq:(c + 1) * tq, :])
            s_ref[m, :, c * tq:(c + 1) * tq] = lax.dot_general(
                q_aug, k_aug, (((1,), (1,)), ((), ())), preferred_element_type=F32)
    dc = dc_ref[it]
    flag = jnp.full((1, 1), (dc >= 0).astype(jnp.int32), jnp.int32).astype(F32)
    off = pl.multiple_of(jnp.maximum(dc, 0) * tq, tq)
    bias = flag * bd_ref[0]
    for m in range(2):
        s_ref[m, :, pl.ds(off, tq)] = s_ref[m, :, pl.ds(off, tq)] - bias


def _attn_softmax_pv(s_ref, is_first, v_ref, m_scr, acc_scr):
    _, chunk, v_dim = v_ref.shape
    ones_col = jnp.where(lax.broadcasted_iota(jnp.int32, (chunk, v_dim), 1) == 0, 1.0, 0.0).astype(BF16)
    v_aug = jnp.concatenate([v_ref[0], ones_col], axis=1)
    for m in range(2):
        s = s_ref[m]
        m_old = jnp.where(is_first, -jnp.inf, m_scr[m])
        acc_old = jnp.where(is_first, 0.0, acc_scr[m])
        m_new = jnp.maximum(m_old, jnp.max(s, axis=-1, keepdims=True))
        p = jnp.exp2(s - m_new).astype(BF16)
        alpha = jnp.exp2(m_old - m_new)
        acc_scr[m] = alpha * acc_old + jnp.dot(p, v_aug, preferred_element_type=F32)
        m_scr[m] = m_new


def _attn_kernel(head_ref, qblk_ref, kvblk_ref, first_ref, last_ref, d0_ref, dc_ref, nk_ref,
                 q_ref, k_ref, v_ref, qc_ref, kc_ref, bd_ref, ed0_ref, lam_ref, gsub_ref, o_ref,
                 s_even, s_odd, m_scr, acc_scr, *, head_dim):
    g = pl.program_id(0)
    n_kept = nk_ref[0]
    e_a = jnp.minimum(g, n_kept - 1)
    e_b = jnp.minimum(jnp.maximum(g - 1, 0), n_kept - 1)
    active = g <= n_kept
    v_dim = q_ref.shape[2]

    @pl.when(g == 0)
    def _():
        s_odd[...] = jnp.zeros(s_odd.shape, F32)
        m_scr[...] = jnp.zeros(m_scr.shape, F32)
        acc_scr[...] = jnp.zeros(acc_scr.shape, F32)

    is_first = jnp.full((1, 1), first_ref[e_b], jnp.int32) > 0

    def step(s_write, s_read):
        _attn_scores(s_write, e_a, q_ref, k_ref, qc_ref, kc_ref, bd_ref, ed0_ref, d0_ref, dc_ref, head_dim)
        _attn_softmax_pv(s_read, is_first, v_ref, m_scr, acc_scr)

    @pl.when(jnp.logical_and(active, g % 2 == 0))
    def _():
        step(s_even, s_odd)

    @pl.when(jnp.logical_and(active, g % 2 == 1))
    def _():
        step(s_odd, s_even)

    @pl.when(jnp.logical_and(active, last_ref[e_b] == 1))
    def _():
        lp = lam_ref[...]
        lam_init = lp[4:5, :1]
        lam = (jnp.exp(jnp.sum(lp[0:1] * lp[1:2], axis=-1, keepdims=True))
               - jnp.exp(jnp.sum(lp[2:3] * lp[3:4], axis=-1, keepdims=True)) + lam_init)
        a0 = acc_scr[0]
        a1 = acc_scr[1]
        o = a0[:, :v_dim] / a0[:, v_dim:v_dim + 1] - lam * (a1[:, :v_dim] / a1[:, v_dim:v_dim + 1])
        o_ref[0] = (_rms(o) * gsub_ref[...] * (1.0 - lam_init)).astype(o_ref.dtype)


def _attn_items(geom, tq, chunk):
    qblk, kvblk, d0s, dcs, dist = [], [], [], [], []
    for off, nb, s in ((0, geom.bp, geom.sp), (geom.tp, geom.bs, geom.ss)):
        for b in range(nb):
            for qt in range(s // tq):
                for c in range(s // chunk):
                    qblk.append((off + b * s + qt * tq) // tq)
                    kvblk.append((off + b * s + c * chunk) // chunk)
                    d0 = qt * tq - c * chunk
                    d0s.append(d0)
                    dcs.append(d0 // tq if 0 <= d0 < chunk else -1)
                    dist.append(0 if 0 <= d0 < chunk else (d0 - chunk + 1 if d0 > 0 else -d0 - tq + 1))
    names = ("qblk", "kvblk", "d0", "dc", "min_dist")
    return {n: np.asarray(a, np.int32) for n, a in zip(names, (qblk, kvblk, d0s, dcs, dist))}


def _attn_stats_kernel(q_ref, k_ref, o_ref, *, head_dim, tq):
    v_dim = q_ref.shape[2]
    r = lax.broadcasted_iota(jnp.int32, (v_dim, v_dim), 0)
    c = lax.broadcasted_iota(jnp.int32, (v_dim, v_dim), 1)
    sel = jnp.where(jnp.logical_or(jnp.logical_and(c == 0, r < head_dim), jnp.logical_and(c == 1, r >= head_dim)),
                    1.0, 0.0).astype(BF16)
    pad = jnp.zeros((o_ref.shape[2] - 3, v_dim), F32)
    for t in range(q_ref.shape[1] // tq):
        q = q_ref[0, t * tq:(t + 1) * tq, :].astype(F32)
        k = k_ref[0, t * tq:(t + 1) * tq, :].astype(F32)
        out_rows = []
        for x, reduce_rows in ((q * q, jnp.max), (k * k, jnp.max), (q * k, jnp.min)):
            per_row = jnp.dot(x.astype(BF16), sel, preferred_element_type=F32)
            out_rows.append(reduce_rows(per_row, axis=0, keepdims=True))
        o_ref[0, t] = jnp.concatenate(out_rows + [pad], axis=0)


def _attn_stats(qkv, n_heads, head_dim, tq, rows_per_step):
    t, v_dim = qkv.shape[1], qkv.shape[2]
    tiles = rows_per_step // tq
    return pl.pallas_call(
        functools.partial(_attn_stats_kernel, head_dim=head_dim, tq=tq),
        grid=(n_heads, t // rows_per_step),
        in_specs=[
            pl.BlockSpec((1, rows_per_step, v_dim), lambda h, i: (h, i, 0)),
            pl.BlockSpec((1, rows_per_step, v_dim), lambda h, i: (n_heads + h, i, 0)),
        ],
        out_specs=pl.BlockSpec((1, tiles, SUBLANES, LANES), lambda h, i: (h, i, 0, 0)),
        out_shape=jax.ShapeDtypeStruct((n_heads, t // tq, SUBLANES, LANES), F32),
        compiler_params=_params(("arbitrary", "arbitrary")),
        name="attn_stats",
    )(qkv, qkv)


def _attn_entries(items, stats, slopes_log2, n_sub):
    n_heads = stats.shape[0]
    n_items = items["qblk"].shape[0]
    q_norm = jnp.sqrt(stats[:, :, 0, 0:2])
    k_norm = jnp.sqrt(stats[:, :, 1, 0:2])
    self_min = stats[:, :, 2, 0:2]
    kv_tiles = items["kvblk"][:, None] * n_sub + np.arange(n_sub)[None, :]
    k_norm_item = jnp.max(k_norm[:, kv_tiles, :], axis=2)
    upper = (q_norm[:, items["qblk"], :] * k_norm_item * SKIP_NORM_SLACK
             - slopes_log2[:, None, None] * items["min_dist"][None, :, None].astype(np.float32))
    lower = self_min - SKIP_SELF_SLACK * q_norm * k_norm
    gap = upper - lower[:, items["qblk"], :]
    drop = jnp.logical_and(jnp.all(gap < -SKIP_LOG2_GAP, axis=-1), (items["min_dist"] > 0)[None, :])
    keep = jnp.logical_not(drop).reshape(-1)
    n_all = n_heads * n_items
    order = jnp.argsort(jnp.logical_not(keep), stable=True).astype(jnp.int32)
    n_kept = jnp.sum(keep.astype(jnp.int32))
    pos = jnp.arange(n_all, dtype=jnp.int32)
    src = order[jnp.minimum(pos, n_kept - 1)]
    head = src // n_items
    it = src % n_items
    qblk = jnp.asarray(items["qblk"])[it]
    tile_id = head * (int(items["qblk"].max()) + 1) + qblk
    first = jnp.logical_or(pos == 0, tile_id != jnp.roll(tile_id, 1))
    last = jnp.logical_or(pos >= n_kept - 1, tile_id != jnp.roll(tile_id, -1))
    i32 = lambda a: a.astype(jnp.int32)
    return (i32(head), qblk, jnp.asarray(items["kvblk"])[it], i32(first), i32(last),
            jnp.asarray(items["d0"])[it], jnp.asarray(items["dc"])[it], n_kept.reshape(1))


def _bf16_parts(x, n):
    parts, rest = [], np.asarray(x, np.float64)
    for _ in range(n):
        p = rest.astype(BF16).astype(np.float64)
        parts.append(p)
        rest = rest - p
    return parts


def _attn_consts(n_heads, head_dim, tq, chunk, max_d0):
    v_dim = 2 * head_dim
    slopes = 2.0 ** (-ALIBI_MAX * np.arange(1, n_heads + 1, dtype=np.float64) / n_heads) * LOG2E
    sl = _bf16_parts(slopes, SLOPE_PARTS)
    r = np.arange(tq)
    j = np.arange(chunk)
    i_parts = [None, (r // BF16_EXACT_INT) * BF16_EXACT_INT, r % BF16_EXACT_INT]
    j_parts = [(j // BF16_EXACT_INT) * BF16_EXACT_INT, j % BF16_EXACT_INT]
    assert chunk <= BF16_EXACT_INT ** 2 and max_d0 % BF16_EXACT_INT == 0 and max_d0 <= BF16_EXACT_INT ** 2
    n_cols = SLOPE_PARTS * (len(i_parts) + len(j_parts))
    assert n_cols <= head_dim
    qc = np.zeros((n_heads, 2, tq, v_dim), np.float64)
    kc = np.zeros((n_heads, 2, chunk, v_dim), np.float64)
    ed0 = np.zeros((2, 1, v_dim), np.float32)
    for m in range(2):
        base = head_dim if m == 0 else 0
        col = base
        for a in range(SLOPE_PARTS):
            for jp in j_parts:
                qc[:, m, :, col] = sl[a][:, None]
                kc[:, m, :, col] = jp[None, :]
                col += 1
            for ip in i_parts:
                if ip is None:
                    ed0[m, 0, col] = 1.0
                else:
                    qc[:, m, :, col] = ip[None, :]
                kc[:, m, :, col] = -sl[a][:, None]
                col += 1
    bd = slopes[:, None, None] * np.abs(r[:, None] - r[None, :])[None]
    for arr in (qc, kc):
        assert np.array_equal(arr.astype(BF16).astype(np.float64), arr)
    return (jnp.asarray(qc.astype(BF16)), jnp.asarray(kc.astype(BF16)),
            jnp.asarray(bd.astype(np.float32)), jnp.asarray(ed0), jnp.asarray(slopes.astype(np.float32)))


def _attention(geom, qkv, consts, items, lam_pack_l, g_subln_l, n_heads, head_dim, tq, chunk):
    t = qkv.shape[1]
    v_dim = 2 * head_dim
    qc, kc, bd, ed0, slopes_log2 = consts
    n_sub = chunk // tq
    stats = _attn_stats(qkv, n_heads, head_dim, tq, _tile(t, ATTN_STATS_ROWS, tq))
    entries = _attn_entries(items, stats, slopes_log2, n_sub)
    n_entries = entries[0].shape[0]
    k_head0 = n_heads
    v_head0 = 2 * n_heads

    def score_entry(g, nk):
        return jnp.minimum(g, nk[0] - 1)

    def finish_entry(g, nk):
        return jnp.minimum(jnp.maximum(g - 1, 0), nk[0] - 1)

    def q_map(g, hd, qb, kb, fi, la, d0, dc, nk):
        e = score_entry(g, nk)
        return hd[e], qb[e], 0

    def k_map(g, hd, qb, kb, fi, la, d0, dc, nk):
        e = score_entry(g, nk)
        return k_head0 + hd[e], kb[e], 0

    def v_map(g, hd, qb, kb, fi, la, d0, dc, nk):
        e = finish_entry(g, nk)
        return v_head0 + hd[e], kb[e], 0

    def o_map(g, hd, qb, kb, fi, la, d0, dc, nk):
        e = finish_entry(g, nk)
        return hd[e], qb[e], 0

    def head_map(n_trailing):
        def fn(g, hd, qb, kb, fi, la, d0, dc, nk):
            return (hd[score_entry(g, nk)],) + (0,) * n_trailing
        return fn

    grid_spec = pltpu.PrefetchScalarGridSpec(
        num_scalar_prefetch=8,
        grid=(n_entries + 1,),
        in_specs=[
            pl.BlockSpec((1, tq, v_dim), q_map),
            pl.BlockSpec((1, chunk, v_dim), k_map),
            pl.BlockSpec((1, chunk, v_dim), v_map),
            pl.BlockSpec((1, 2, tq, v_dim), head_map(3)),
            pl.BlockSpec((1, 2, chunk, v_dim), head_map(3)),
            pl.BlockSpec((1, tq, tq), head_map(2)),
            pl.BlockSpec((2, 1, v_dim), lambda g, *_: (0, 0, 0)),
            pl.BlockSpec((8, head_dim), lambda g, *_: (0, 0)),
            _row_spec(v_dim),
        ],
        out_specs=pl.BlockSpec((1, tq, v_dim), o_map),
        scratch_shapes=[
            pltpu.VMEM((2, tq, chunk), F32),
            pltpu.VMEM((2, tq, chunk), F32),
            pltpu.VMEM((2, tq, 1), F32),
            pltpu.VMEM((2, tq, 2 * v_dim), F32),
        ],
    )
    return pl.pallas_call(
        functools.partial(_attn_kernel, head_dim=head_dim),
        grid_spec=grid_spec,
        out_shape=jax.ShapeDtypeStruct((n_heads, t, v_dim), BF16),
        compiler_params=_params(("arbitrary",)),
        name="diff_attn",
    )(*entries, qkv, qkv, qkv, qc, kc, bd, ed0, lam_pack_l, g_subln_l)


def _outproj_kernel(attn_ref, cb_ref, cc_ref, cx_ref, ga_ref, gc_ref,
                    ccp_ref, cxp_ref, ccn_ref, cxn_ref,
                    x_ref, convw_ref, wout_ref, gpost_ref, gatem_ref, gpre_ref, scalef_ref, shiftf_ref,
                    *rest, geom, moe):
    if moe:
        wr_ref, xo_ref, h_ref, tope_ref, topw_ref = rest
    else:
        xo_ref, h_ref = rest
    tm = x_ref.shape[0]
    row0 = pl.program_id(0) * tm
    seq = jnp.where(row0 < geom.tp, geom.sp, geom.ss)
    rel0 = jnp.where(row0 < geom.tp, row0, row0 - geom.tp)
    not_start = (rel0 % seq != 0).astype(F32)
    not_end = ((rel0 + tm) % seq != 0).astype(F32)

    def rows(ref, lo=0, hi=None):
        hi = ref.shape[1] if hi is None else hi
        return jnp.concatenate([ref[s, lo:hi, :] for s in range(ref.shape[0])], axis=1).astype(F32)

    u = rows(cc_ref) * rows(cx_ref)
    halo = ccp_ref.shape[1]
    u_prev = (rows(ccp_ref, halo - 1, halo) * rows(cxp_ref, halo - 1, halo)) * not_start
    u_next = (rows(ccn_ref, 0, 1) * rows(cxn_ref, 0, 1)) * not_end
    row = lax.broadcasted_iota(jnp.int32, u.shape, 0)
    u_m1 = jnp.where(row == 0, u_prev, pltpu.roll(u, 1, 0))
    u_p1 = jnp.where(row == tm - 1, u_next, pltpu.roll(u, tm - 1, 0))
    cw = convw_ref[...]
    conv = u_m1 * cw[0:1] + u * cw[1:2] + u_p1 * cw[2:3]
    short = rows(cb_ref) * conv
    merged = jax.nn.sigmoid(rows(ga_ref)) * rows(attn_ref) + jax.nn.sigmoid(rows(gc_ref)) * short
    o = jnp.dot(merged.astype(BF16), wout_ref[...], preferred_element_type=F32)
    x_new = x_ref[...] + gatem_ref[0] * (_rms(o) * gpost_ref[...])
    xo_ref[...] = x_new
    h = _rms(x_new) * gpre_ref[...] * (1.0 + scalef_ref[0]) + shiftf_ref[0]
    h_ref[...] = h.astype(h_ref.dtype)

    if moe:
        n_e = wr_ref.shape[1]
        wr = wr_ref[...]
        h_hi = h.astype(BF16)
        h_lo = (h - h_hi.astype(F32)).astype(BF16)
        w_hi = wr.astype(BF16)
        w_lo = (wr - w_hi.astype(F32)).astype(BF16)
        logits = (jnp.dot(h_hi, w_hi, preferred_element_type=F32) + jnp.dot(h_lo, w_hi, preferred_element_type=F32)
                  + jnp.dot(h_hi, w_lo, preferred_element_type=F32))
        lane = lax.broadcasted_iota(jnp.int32, logits.shape, 1).astype(F32)
        m1 = jnp.max(logits, axis=-1, keepdims=True)
        i1 = jnp.min(jnp.where(logits == m1, lane, float(n_e)), axis=-1, keepdims=True)
        rest_l = jnp.where(lane == i1, -jnp.inf, logits)
        m2 = jnp.max(rest_l, axis=-1, keepdims=True)
        i2 = jnp.min(jnp.where(rest_l == m2, lane, float(n_e)), axis=-1, keepdims=True)
        e = jnp.exp(m2 - m1)
        w1 = 1.0 / (1.0 + e)
        w2 = e / (1.0 + e)
        tope_ref[...] = jnp.where(lane == 0.0, i1, jnp.where(lane == 1.0, i2, 0.0)).astype(jnp.int32)
        topw_ref[...] = jnp.where(lane == 0.0, w1, jnp.where(lane == 1.0, w2, 0.0))


def _outproj(geom, attn, z, x, conv_w_l, w_out_l, g_post, g_pre, mod_l, w_router_l):
    t, d = x.shape
    moe = w_router_l is not None
    tm = _tile(geom.row_gcd, OUTPROJ_ROWS, ROW_ALIGN)
    halo = ROW_ALIGN
    nh = tm // halo
    last_halo = t // halo - 1
    n_heads, _, v_dim = attn.shape
    spt = d // LANES
    zcol = lambda c: pl.BlockSpec((spt, tm, LANES), lambda i: (c, i, 0))
    zhalo = lambda c, fn: pl.BlockSpec((spt, halo, LANES), lambda i: (c, fn(i), 0))
    prev_blk = lambda i: jnp.maximum(i * nh - 1, 0)
    next_blk = lambda i: jnp.minimum((i + 1) * nh, last_halo)
    in_specs = [
        pl.BlockSpec((n_heads, tm, v_dim), lambda i: (0, i, 0)),
        zcol(3), zcol(4), zcol(5), zcol(6), zcol(7),
        zhalo(4, prev_blk), zhalo(5, prev_blk), zhalo(4, next_blk), zhalo(5, next_blk),
        pl.BlockSpec((tm, d), lambda i: (i, 0)),
        pl.BlockSpec(conv_w_l.shape, lambda i: (0, 0)),
        pl.BlockSpec((d, d), lambda i: (0, 0)),
        _row_spec(d),
        geom.mod_spec(2, tm),
        _row_spec(d),
        geom.mod_spec(4, tm), geom.mod_spec(3, tm),
    ]
    args = [attn, z, z, z, z, z, z, z, z, z, x, conv_w_l, w_out_l, g_post, mod_l, g_pre, mod_l, mod_l]
    out_specs = [pl.BlockSpec((tm, d), lambda i: (i, 0)), pl.BlockSpec((tm, d), lambda i: (i, 0))]
    out_shape = [jax.ShapeDtypeStruct((t, d), F32), jax.ShapeDtypeStruct((t, d), F32 if moe else BF16)]
    if moe:
        n_e = w_router_l.shape[1]
        in_specs.append(pl.BlockSpec((d, n_e), lambda i: (0, 0)))
        args.append(w_router_l)
        out_specs += [pl.BlockSpec((tm, n_e), lambda i: (i, 0))] * 2
        out_shape += [jax.ShapeDtypeStruct((t, n_e), jnp.int32), jax.ShapeDtypeStruct((t, n_e), F32)]
    return pl.pallas_call(
        functools.partial(_outproj_kernel, geom=geom, moe=moe),
        grid=(t // tm,),
        in_specs=in_specs,
        out_specs=out_specs,
        out_shape=out_shape,
        compiler_params=_params(("arbitrary",)),
        name="out_proj_moe" if moe else "out_proj",
    )(*args)


def _swiglu_partial(h, wg, wu, wd):
    g = jnp.dot(h, wg, preferred_element_type=F32)
    u = jnp.dot(h, wu, preferred_element_type=F32)
    a = (g * jax.nn.sigmoid(g)) * u
    return jnp.dot(a.astype(BF16), wd, preferred_element_type=F32)


def _ffn_kernel(h_ref, x_ref, wg_ref, wu_ref, wd_ref, gpost_ref, gate_ref, xo_ref, acc_scr, *, n_f):
    f = pl.program_id(1)
    part = _swiglu_partial(h_ref[...], wg_ref[...], wu_ref[...], wd_ref[...])
    if n_f == 1:
        xo_ref[...] = x_ref[...] + gate_ref[0] * (_rms(part) * gpost_ref[...])
        return

    @pl.when(f == 0)
    def _():
        acc_scr[...] = part

    @pl.when(f > 0)
    def _():
        acc_scr[...] += part

    @pl.when(f == pl.num_programs(1) - 1)
    def _():
        xo_ref[...] = x_ref[...] + gate_ref[0] * (_rms(acc_scr[...]) * gpost_ref[...])


def _ffn(geom, h, x, wg, wu, wd, g_post, mod_l):
    t, d = x.shape
    ff = wg.shape[1]
    tm = _tile(geom.row_gcd, FFN_ROWS, ROW_ALIGN)
    tf = _tile(ff, FFN_COLS, LANES)
    n_f = ff // tf
    weight_mode = dict(pipeline_mode=pl.Buffered(1)) if n_f == 1 else {}
    return pl.pallas_call(
        functools.partial(_ffn_kernel, n_f=n_f),
        grid=(t // tm, n_f),
        in_specs=[
            pl.BlockSpec((tm, d), lambda i, f: (i, 0)),
            pl.BlockSpec((tm, d), lambda i, f: (i, 0)),
            pl.BlockSpec((d, tf), lambda i, f: (0, f), **weight_mode),
            pl.BlockSpec((d, tf), lambda i, f: (0, f), **weight_mode),
            pl.BlockSpec((tf, d), lambda i, f: (f, 0), **weight_mode),
            _row_spec(d),
            geom.mod_spec(5, tm),
        ],
        out_specs=pl.BlockSpec((tm, d), lambda i, f: (i, 0)),
        out_shape=jax.ShapeDtypeStruct((t, d), F32),
        scratch_shapes=[pltpu.VMEM((tm, d), F32)],
        compiler_params=_params(("arbitrary", "arbitrary")),
        name="dense_ffn",
    )(h, x, wg, wu, wd, g_post, mod_l)


def _expert_kernel(be_ref, idx0_ref, idxn_ref, h_ref, wg_ref, wu_ref, wd_ref, y_ref,
                   xf_scr, xb_scr, sem):
    b = pl.program_id(0)
    f = pl.program_id(1)
    n_blk = pl.num_programs(0)
    n_f = pl.num_programs(1)
    blk, d = xb_scr.shape
    slot = b % 2

    def row_copy(idx_ref, r, s, tile, sub):
        return pltpu.make_async_copy(h_ref.at[pl.ds(idx_ref[r], 1)], xf_scr.at[s, tile, pl.ds(sub, 1)], sem.at[s])

    def block_copy(s):
        return pltpu.make_async_copy(xf_scr.at[s], xf_scr.at[s], sem.at[s])

    @pl.when(jnp.logical_and(b == 0, f == 0))
    def _():
        def issue(r, carry):
            row_copy(idx0_ref, r, 0, r // SUBLANES, r % SUBLANES).start()
            return carry

        lax.fori_loop(0, blk, issue, 0, unroll=DMA_ISSUE_UNROLL)

    @pl.when(f == 0)
    def _():
        block_copy(slot).wait()
        xb_scr[...] = xf_scr[slot].reshape(blk, d).astype(BF16)

    def prefetch_slice():
        per_step = blk // EXPERT_F_STEPS
        for u in range(per_step):
            row_copy(idxn_ref, f * per_step + u, 1 - slot,
                     f * (per_step // SUBLANES) + u // SUBLANES, u % SUBLANES).start(priority=u % DMA_PRIORITIES)

    part = _swiglu_partial(xb_scr[...], wg_ref[0], wu_ref[0], wd_ref[0])

    @pl.when(f == 0)
    def _():
        prefetch_slice()
        y_ref[...] = part

    @pl.when(f > 0)
    def _():
        prefetch_slice()
        y_ref[...] += part

    @pl.when(jnp.logical_and(b == n_blk - 1, f == n_f - 1))
    def _():
        block_copy(1 - slot).wait()


def _experts(h, row_tok, block_e, wg, wu, wd, blk):
    n_pad = row_tok.shape[0]
    d = h.shape[1]
    ff = wg.shape[2]
    tf = ff // EXPERT_F_STEPS
    n_blk = n_pad // blk
    assert ff % EXPERT_F_STEPS == 0 and tf % LANES == 0 and blk % SMEM_INDEX_ALIGN == 0
    assert blk % (EXPERT_F_STEPS * SUBLANES) == 0 and h.dtype == F32
    grid_spec = pltpu.PrefetchScalarGridSpec(
        num_scalar_prefetch=1,
        grid=(n_blk, EXPERT_F_STEPS),
        in_specs=[
            pl.BlockSpec((blk,), lambda b, f, be: (0,), memory_space=pltpu.SMEM),
            pl.BlockSpec((blk,), lambda b, f, be: (jnp.minimum(b + 1, n_blk - 1),), memory_space=pltpu.SMEM),
            pl.BlockSpec(memory_space=pl.ANY),
            pl.BlockSpec((1, d, tf), lambda b, f, be: (be[b], 0, f)),
            pl.BlockSpec((1, d, tf), lambda b, f, be: (be[b], 0, f)),
            pl.BlockSpec((1, tf, d), lambda b, f, be: (be[b], f, 0)),
        ],
        out_specs=pl.BlockSpec((blk, d), lambda b, f, be: (b, 0)),
        scratch_shapes=[
            pltpu.VMEM((2, blk // SUBLANES, SUBLANES, d), F32),
            pltpu.VMEM((blk, d), BF16),
            pltpu.SemaphoreType.DMA((2,)),
        ],
    )
    return pl.pallas_call(
        _expert_kernel,
        grid_spec=grid_spec,
        out_shape=jax.ShapeDtypeStruct((n_pad, d), F32),
        compiler_params=_params(("arbitrary", "arbitrary")),
        name="expert_ffn",
    )(block_e, row_tok, row_tok, h, wg, wu, wd)


def _combine_kernel(dest_ref, y_ref, w_ref, x_ref, gpost_ref, gate_ref, xo_ref, buf, sem):
    tc, d = x_ref.shape

    def issue(tile, carry):
        for sub in range(SUBLANES):
            for k in range(TOP_K):
                src_row = dest_ref[TOP_K * (tile * SUBLANES + sub) + k]
                pltpu.make_async_copy(y_ref.at[pl.ds(src_row, 1)], buf.at[k, tile, pl.ds(sub, 1)],
                                      sem).start(priority=k % DMA_PRIORITIES)
        return carry

    lax.fori_loop(0, tc // SUBLANES, issue, 0)
    pltpu.make_async_copy(buf, buf, sem).wait()
    w = w_ref[...]
    o = w[:, 0:1] * buf[0].reshape(tc, d)
    for k in range(1, TOP_K):
        o = o + w[:, k:k + 1] * buf[k].reshape(tc, d)
    xo_ref[...] = x_ref[...] + gate_ref[0] * (_rms(o) * gpost_ref[...])


def _combine(geom, y_rows, dest, top_w, x, g_post, mod_l):
    t, d = x.shape
    n_e = top_w.shape[1]
    tc = _tile(geom.row_gcd, COMBINE_ROWS, SMEM_INDEX_ALIGN // TOP_K)
    return pl.pallas_call(
        _combine_kernel,
        grid=(t // tc,),
        in_specs=[
            pl.BlockSpec((TOP_K * tc,), lambda i: (i,), memory_space=pltpu.SMEM),
            pl.BlockSpec(memory_space=pl.ANY),
            pl.BlockSpec((tc, n_e), lambda i: (i, 0)),
            pl.BlockSpec((tc, d), lambda i: (i, 0)),
            _row_spec(d),
            geom.mod_spec(5, tc),
        ],
        out_specs=pl.BlockSpec((tc, d), lambda i: (i, 0)),
        out_shape=jax.ShapeDtypeStruct((t, d), F32),
        scratch_shapes=[pltpu.VMEM((TOP_K, tc // SUBLANES, SUBLANES, d), F32), pltpu.SemaphoreType.DMA(())],
        compiler_params=_params(("arbitrary",)),
        name="moe_combine",
    )(dest, y_rows, top_w, x, g_post, mod_l)


def _route(top_e, n_experts, blk):
    t = top_e.shape[0]
    n_assign = t * TOP_K
    n_blocks = -(-(n_assign + n_experts * (blk - 1)) // blk)
    n_pad = n_blocks * blk
    flat_e = top_e[:, :TOP_K].reshape(-1)
    onehot = (flat_e[:, None] == jnp.arange(n_experts, dtype=jnp.int32)[None, :]).astype(jnp.int32)
    rank = jnp.sum((jnp.cumsum(onehot, axis=0) - onehot) * onehot, axis=1)
    counts = jnp.sum(onehot, axis=0)
    padded = (counts + blk - 1) // blk * blk
    end_pad = jnp.cumsum(padded)
    start_pad = end_pad - padded
    dest = (start_pad[flat_e] + rank).astype(jnp.int32)
    flat_tok = jnp.arange(n_assign, dtype=jnp.int32) // TOP_K
    row_tok = jnp.zeros((n_pad,), jnp.int32).at[dest].set(flat_tok)
    block_e = jnp.minimum(
        jnp.searchsorted(end_pad, jnp.arange(n_blocks, dtype=jnp.int32) * blk, side='right'),
        n_experts - 1).astype(jnp.int32)
    return dest, row_tok, block_e


def kernel(x_prompt, x_sample, c_prompt, c_sample, w_ada, b_ada, g_mix_pre, g_mix_post, g_ffn_pre, g_ffn_post,
           w_in, lam_q1, lam_k1, lam_q2, lam_k2, g_subln, conv_w, w_out, w_ffn_gate, w_ffn_up, w_ffn_down,
           w_router, w_exp_gate, w_exp_up, w_exp_down):
    bp, sp, d = x_prompt.shape
    bs, ss, _ = x_sample.shape
    depth = w_in.shape[0]
    head_dim = lam_q1.shape[1]
    v_dim = g_subln.shape[1]
    n_heads = d // v_dim
    n_experts = w_router.shape[2]
    n_in = w_in.shape[2]
    assert v_dim == 2 * head_dim and v_dim == LANES and n_in == 8 * d
    geom = _Geom(bp, sp, bs, ss, d)

    x = jnp.concatenate([x_prompt.reshape(bp * sp, d), x_sample.reshape(bs * ss, d)], axis=0)
    c_all = jnp.concatenate([c_prompt, c_sample, jnp.zeros((geom.nb_pad - geom.nb, d), F32)], axis=0)
    mod = _ada(c_all, w_ada, b_ada)

    tq = _tile(geom.row_gcd, ATTN_Q_ROWS, ROW_ALIGN)
    chunk = _tile(geom.row_gcd, ATTN_KV_CHUNK, tq)
    attn_consts = _attn_consts(n_heads, head_dim, tq, chunk, max(sp, ss))
    attn_items = _attn_items(geom, tq, chunk)
    col_scale = jnp.asarray(np.where(np.arange(n_in) < n_heads * v_dim, head_dim ** -0.5 * LOG2E, 1.0)
                            .astype(np.float32))[None, :]
    lam_init = np.asarray([0.8 - 0.6 * math.exp(-0.3 * l) for l in range(depth)], np.float32)
    lam_pack = jnp.stack(
        [lam_q1, lam_k1, lam_q2, lam_k2, jnp.broadcast_to(jnp.asarray(lam_init)[:, None], lam_q1.shape)]
        + [jnp.zeros_like(lam_q1)] * 3, axis=1)

    per_layer = dict(mod=mod, g_mix_pre=g_mix_pre[:, None], g_mix_post=g_mix_post[:, None],
                     g_ffn_pre=g_ffn_pre[:, None], g_ffn_post=g_ffn_post[:, None], g_subln=g_subln[:, None],
                     w_in=w_in, lam=lam_pack, conv_w=conv_w, w_out=w_out)

    def mixer(x, p, w_router_l):
        z = _inproj(geom, x, p["g_mix_pre"], p["mod"], p["w_in"].astype(BF16), col_scale)
        attn = _attention(geom, z, attn_consts, attn_items, p["lam"], p["g_subln"], n_heads, head_dim, tq, chunk)
        return _outproj(geom, attn, z, x, p["conv_w"], p["w_out"].astype(BF16), p["g_mix_post"], p["g_ffn_pre"],
                        p["mod"], w_router_l)

    def dense_layer(x, p, w):
        x, h = mixer(x, p, None)
        return _ffn(geom, h, x, w["gate"].astype(BF16), w["up"].astype(BF16), w["down"].astype(BF16),
                    p["g_ffn_post"], p["mod"])

    def expert_layer(x, p, w):
        x, h, top_e, top_w = mixer(x, p, w["router"])
        dest, row_tok, block_e = _route(top_e, n_experts, MOE_BLOCK)
        y_rows = _experts(h, row_tok, block_e, w["gate"].astype(BF16), w["up"].astype(BF16),
                          w["down"].astype(BF16), MOE_BLOCK)
        return _combine(geom, y_rows, dest, top_w, x, p["g_ffn_post"], p["mod"])

    n_pairs = depth // 2
    even = jax.tree.map(lambda a: a[0:2 * n_pairs:2], per_layer)
    odd = jax.tree.map(lambda a: a[1:2 * n_pairs:2], per_layer)
    dense_w = dict(gate=w_ffn_gate, up=w_ffn_up, down=w_ffn_down)
    exp_w = dict(router=w_router, gate=w_exp_gate, up=w_exp_up, down=w_exp_down)

    def pair(x, xs):
        p_even, p_odd, dw, ew = xs
        x = dense_layer(x, p_even, dw)
        x = expert_layer(x, p_odd, ew)
        return x, None

    x, _ = lax.scan(pair, x, (even, odd, jax.tree.map(lambda a: a[:n_pairs], dense_w), exp_w))
    if depth % 2 == 1:
        x = dense_layer(x, jax.tree.map(lambda a: a[depth - 1], per_layer),
                        jax.tree.map(lambda a: a[n_pairs], dense_w))

    y_prompt = x[:geom.tp].reshape(bp, sp, d)
    y_sample = x[geom.tp:].reshape(bs, ss, d)
    return (y_prompt, y_sample)
```
